```python
import math
import jax, jax.numpy as jnp
from jax import lax
import numpy as np

D_MODEL = 1024
BATCH = 32
SEQ = 256
DEPTH = 4
DEC_BATCH = 2
DEC_SEQ = 2048
PAST_LEN = 256

F32 = jnp.float32
GRID_W = 64
S5_WIDTH = 256
S5_CH = 16
S5_GROUPS = S5_WIDTH // S5_CH
S5_STATE = 64
NA_HEADS = 4
NA_HD = 64
NA_WIDTH = NA_HEADS * NA_HD
WIN_H = 8
WIN_W = 16
MLA_HEADS = 8
MLA_NOPE = 64
MLA_ROPE = 32
MLA_V = 64
MLA_Q_LORA = 256
MLA_KV_LORA = 128
MLA_WIDTH = MLA_HEADS * MLA_V
MLA_SCALE = (MLA_NOPE + MLA_ROPE) ** -0.5
D_MIX = S5_WIDTH + NA_WIDTH + MLA_WIDTH
OFF_NA = S5_WIDTH
OFF_QA = OFF_NA + 3 * NA_WIDTH
OFF_KVA = OFF_QA + MLA_Q_LORA
D_IN = OFF_KVA + MLA_KV_LORA + MLA_ROPE
N_EXPERTS = 32
TOP_K = 4
D_FF = 1024
SWIGLU_ALPHA = 1.702
SWIGLU_LIMIT = 7.0
MOE_BLOCK = 128
Q_BLOCK = 128
ROPE_BASE = 10000.0
EPS = 1e-6

kernel_name = 'hybrid_s5_natten_mla_moe_diffusion_step'


def rmsnorm(x, g):
    xf = x.astype(F32)
    y = xf * lax.rsqrt(jnp.mean(xf * xf, axis=-1, keepdims=True) + EPS)
    return (y * g.astype(F32)).astype(x.dtype)


def modulation(cvec, w, b):
    m = jax.nn.silu(cvec) @ w + b
    return jnp.split(m[:, None, :], 6, axis=-1)


def adaln(x, g, shift, scale):
    return rmsnorm(x, g) * (1 + scale) + shift


def merge_heads(o):
    n, h, L, d = o.shape
    return o.transpose(0, 2, 1, 3).reshape(n, L, h * d)


def block_attention(q, k, v, scale):
    n, h, lq, dk = q.shape
    nb = lq // Q_BLOCK
    qb = jnp.moveaxis(q.reshape(n, h, nb, Q_BLOCK, dk), 2, 0)

    def one(qi):
        s = jnp.einsum('bhqd,bhkd->bhqk', qi, k).astype(F32) * scale
        p = jax.nn.softmax(s, axis=-1)
        return jnp.einsum('bhqk,bhkd->bhqd', p.astype(v.dtype), v)

    o = lax.map(one, qb)
    return jnp.moveaxis(o, 0, 2).reshape(n, h, lq, v.shape[-1])


def axial_rope(x):
    L = x.shape[-2]
    t = jnp.arange(L)
    row = (t // GRID_W).astype(F32)
    col = (t % GRID_W).astype(F32)
    half = MLA_ROPE // 2
    inv = ROPE_BASE ** (-jnp.arange(0, half, 2, dtype=F32) / half)

    def rot(xa, pos):
        ang = pos[:, None] * inv[None, :]
        cos, sin = jnp.cos(ang), jnp.sin(ang)
        x1, x2 = xa[..., :half // 2], xa[..., half // 2:]
        return jnp.concatenate([x1 * cos - x2 * sin, x2 * cos + x1 * sin], axis=-1)

    xf = x.astype(F32)
    return jnp.concatenate([rot(xf[..., :half], row), rot(xf[..., half:], col)], axis=-1).astype(x.dtype)


def s5_discretize(lam_re, lam_im, log_dt):
    lr, li = lam_re.astype(F32), lam_im.astype(F32)
    dt = jnp.exp(log_dt.astype(F32))[..., None]
    mag = jnp.exp(lr * dt)
    a_re, a_im = mag * jnp.cos(li * dt), mag * jnp.sin(li * dt)
    den = lr * lr + li * li
    nr = a_re - 1.0
    k_re = (nr * lr + a_im * li) / den
    k_im = (a_im * lr - nr * li) / den
    return a_re, a_im, k_re, k_im


def _linrec_combine(e1, e2):
    a1r, a1i, b1r, b1i = e1
    a2r, a2i, b2r, b2i = e2
    return (a2r * a1r - a2i * a1i, a2r * a1i + a2i * a1r,
            a2r * b1r - a2i * b1i + b2r, a2r * b1i + a2i * b1r + b2i)


def s5_scan(u, a_re, a_im, k_re, k_im, b_re, b_im, c_re, c_im, h0_re, h0_im):
    bu_re = jnp.einsum('nlgc,gpc->nlgp', u, b_re.astype(F32))
    bu_im = jnp.einsum('nlgc,gpc->nlgp', u, b_im.astype(F32))
    x_re = k_re * bu_re - k_im * bu_im
    x_im = k_re * bu_im + k_im * bu_re
    x_re = x_re.at[:, 0].add(a_re * h0_re - a_im * h0_im)
    x_im = x_im.at[:, 0].add(a_re * h0_im + a_im * h0_re)
    ar = jnp.broadcast_to(a_re, x_re.shape)
    ai = jnp.broadcast_to(a_im, x_re.shape)
    _, _, h_re, h_im = lax.associative_scan(_linrec_combine, (ar, ai, x_re, x_im), axis=1)
    y = (jnp.einsum('nlgp,gcp->nlgc', h_re, c_re.astype(F32))
         - jnp.einsum('nlgp,gcp->nlgc', h_im, c_im.astype(F32)))
    return y, h_re[:, -1], h_im[:, -1]


def s5_mixer(u, lw, h0_re, h0_im):
    n, L, _ = u.shape
    uf = u.astype(F32).reshape(n, L, S5_GROUPS, S5_CH)
    a_re, a_im, k_re, k_im = s5_discretize(lw['lam_re'], lw['lam_im'], lw['log_dt'])
    y_f, hf_re, hf_im = s5_scan(uf, a_re[0], a_im[0], k_re[0], k_im[0], lw['b_re'][0], lw['b_im'][0],
                                lw['c_re'][0], lw['c_im'][0], h0_re[:, 0], h0_im[:, 0])
    y_b, hb_re, hb_im = s5_scan(jnp.flip(uf, 1), a_re[1], a_im[1], k_re[1], k_im[1], lw['b_re'][1], lw['b_im'][1],
                                lw['c_re'][1], lw['c_im'][1], h0_re[:, 1], h0_im[:, 1])
    y = y_f + jnp.flip(y_b, 1) + uf * lw['s5_d'].astype(F32).reshape(S5_GROUPS, S5_CH)
    g = jax.nn.gelu(y.reshape(n, L, S5_WIDTH))
    out = g * jax.nn.sigmoid(g @ lw['glu_w'].astype(F32) + lw['glu_b'].astype(F32))
    return (out.astype(u.dtype), jnp.stack([hf_re, hb_re], axis=1), jnp.stack([hf_im, hb_im], axis=1))


def na_heads(p):
    n, L, _ = p.shape
    qkv = p[..., OFF_NA:OFF_QA].reshape(n, L, 3, NA_HEADS, NA_HD).transpose(2, 0, 3, 1, 4)
    return qkv[0], qkv[1], qkv[2]


def natten_latent(q, k, v, k_ctx, v_ctx, rpb):
    n, h, L, dh = q.shape
    rows = L // GRID_W
    kh = min(WIN_H, rows)
    qg = q.reshape(n, h, rows, GRID_W, dh)
    kg = k.reshape(n, h, rows, GRID_W, dh)
    vg = v.reshape(n, h, rows, GRID_W, dh)
    col = jnp.arange(GRID_W)
    cs = jnp.clip(col - WIN_W // 2, 0, GRID_W - WIN_W)
    col_idx = cs[:, None] + jnp.arange(WIN_W)
    dc = col_idx - col[:, None] + WIN_W - 1
    scale = dh ** -0.5
    nwin = kh * WIN_W

    def one_row(r):
        rs = jnp.clip(r - kh // 2, 0, rows - kh)
        q_r = lax.dynamic_index_in_dim(qg, r, axis=2, keepdims=False)
        k_band = lax.dynamic_slice_in_dim(kg, rs, kh, axis=2)
        v_band = lax.dynamic_slice_in_dim(vg, rs, kh, axis=2)
        k_win = k_band[:, :, :, col_idx]
        v_win = v_band[:, :, :, col_idx]
        dr = rs + jnp.arange(kh) - r + WIN_H - 1
        bias = rpb[:, dr[None, :, None], dc[:, None, :]].astype(F32)
        s_win = jnp.einsum('bhwd,bhrwkd->bhwrk', q_r, k_win).astype(F32) * scale + bias
        s_ctx = jnp.einsum('bhwd,bhpd->bhwp', q_r, k_ctx).astype(F32) * scale
        s = jnp.concatenate([s_win.reshape(n, h, GRID_W, nwin), s_ctx], axis=-1)
        pr = jax.nn.softmax(s, axis=-1).astype(v.dtype)
        p_win = pr[..., :nwin].reshape(n, h, GRID_W, kh, WIN_W)
        return (jnp.einsum('bhwrk,bhrwkd->bhwd', p_win, v_win)
                + jnp.einsum('bhwp,bhpd->bhwd', pr[..., nwin:], v_ctx))

    o = lax.map(one_row, jnp.arange(rows))
    return jnp.moveaxis(o, 0, 2).reshape(n, h, L, dh)


def mla_queries(p, lw):
    n, L, _ = p.shape
    q = rmsnorm(p[..., OFF_QA:OFF_KVA], lw['q_norm_g']) @ lw['w_qb']
    q = q.reshape(n, L, MLA_HEADS, MLA_NOPE + MLA_ROPE).transpose(0, 2, 1, 3)
    return q[..., :MLA_NOPE], q[..., MLA_NOPE:]


def mla_compress(p, lw):
    kva = p[..., OFF_KVA:]
    return rmsnorm(kva[..., :MLA_KV_LORA], lw['kv_norm_g']), kva[..., MLA_KV_LORA:]


def mla_kv_up(ckv, w_kvb):
    n, L, _ = ckv.shape
    kv = (ckv @ w_kvb).reshape(n, L, MLA_HEADS, MLA_NOPE + MLA_V).transpose(0, 2, 1, 3)
    return kv[..., :MLA_NOPE], kv[..., MLA_NOPE:]


def mla_keys(k_nope, kr):
    return jnp.concatenate([k_nope, jnp.broadcast_to(kr[:, None], k_nope.shape[:3] + (MLA_ROPE,))], axis=-1)


def mixers_context(h, lw):
    n, L, _ = h.shape
    p = h @ lw['w_in']
    z = jnp.zeros((n, 2, S5_GROUPS, S5_STATE), F32)
    s5_out, st_re, st_im = s5_mixer(p[..., :OFF_NA], lw, z, z)
    q, k, v = na_heads(p)
    na_out = block_attention(q, k, v, NA_HD ** -0.5)
    q_nope, q_rope = mla_queries(p, lw)
    ckv, kr = mla_compress(p, lw)
    k_nope, v_m = mla_kv_up(ckv, lw['w_kvb'])
    mla_out = block_attention(jnp.concatenate([q_nope, q_rope], axis=-1), mla_keys(k_nope, kr), v_m, MLA_SCALE)
    out = jnp.concatenate([s5_out, merge_heads(na_out), merge_heads(mla_out)], axis=-1) @ lw['w_out']
    return out, k, v, jnp.concatenate([ckv, kr], axis=-1), st_re, st_im


def mixers_latent(h, lw, na_k_ctx, na_v_ctx, mla_ctx, s5_re0, s5_im0):
    p = h @ lw['w_in']
    s5_out, _, _ = s5_mixer(p[..., :OFF_NA], lw, s5_re0, s5_im0)
    q, k, v = na_heads(p)
    na_out = natten_latent(q, k, v, na_k_ctx, na_v_ctx, lw['rpb'])
    q_nope, q_rope = mla_queries(p, lw)
    ckv, kr = mla_compress(p, lw)
    k_nope, v_m = mla_kv_up(ckv, lw['w_kvb'])
    k_nope_c, v_c = mla_kv_up(mla_ctx[..., :MLA_KV_LORA], lw['w_kvb'])
    k_all = jnp.concatenate([mla_keys(k_nope, axial_rope(kr)),
                             mla_keys(k_nope_c, mla_ctx[..., MLA_KV_LORA:])], axis=2)
    v_all = jnp.concatenate([v_m, v_c], axis=2)
    q_all = jnp.concatenate([q_nope, axial_rope(q_rope)], axis=-1)
    mla_out = block_attention(q_all, k_all, v_all, MLA_SCALE)
    return jnp.concatenate([s5_out, merge_heads(na_out), merge_heads(mla_out)], axis=-1) @ lw['w_out']


def moe(h, router_w, router_b, w_gu, b_gu, w_down, b_down):
    shp = h.shape
    x = h.reshape(-1, D_MODEL)
    n = x.shape[0]
    logits = (x @ router_w + router_b).astype(F32)
    top_v, top_e = lax.top_k(logits, TOP_K)
    gates = jax.nn.softmax(top_v, axis=-1)
    nk = n * TOP_K
    flat_e = top_e.reshape(-1)
    order = jnp.argsort(flat_e)
    e_sorted = flat_e[order]
    tok_sorted = order // TOP_K
    g_sorted = gates.reshape(-1)[order]
    counts = jnp.bincount(flat_e, length=N_EXPERTS)
    padded = (counts + MOE_BLOCK - 1) // MOE_BLOCK * MOE_BLOCK
    pad_end = jnp.cumsum(padded)
    pad_start = pad_end - padded
    start = jnp.cumsum(counts) - counts
    dest = pad_start[e_sorted] + jnp.arange(nk) - start[e_sorted]
    n_blocks = -(-nk // MOE_BLOCK) + N_EXPERTS
    xp = jnp.zeros((n_blocks * MOE_BLOCK, D_MODEL), x.dtype).at[dest].set(x[tok_sorted])
    block_e = jnp.minimum(jnp.searchsorted(pad_end, jnp.arange(n_blocks) * MOE_BLOCK, side='right'), N_EXPERTS - 1)

    def expert_block(args):
        xb, e = args
        gu = xb @ w_gu[e] + b_gu[e]
        gate = jnp.minimum(gu[:, :D_FF], SWIGLU_LIMIT)
        up = jnp.clip(gu[:, D_FF:], -SWIGLU_LIMIT, SWIGLU_LIMIT)
        act = (up + 1) * (gate * jax.nn.sigmoid(SWIGLU_ALPHA * gate))
        return act @ w_down[e] + b_down[e]

    yp = lax.map(expert_block, (xp.reshape(n_blocks, MOE_BLOCK, D_MODEL), block_e)).reshape(-1, D_MODEL)
    y = jax.ops.segment_sum(yp[dest] * g_sorted[:, None].astype(yp.dtype), tok_sorted, num_segments=n)
    return y.astype(h.dtype).reshape(shp)


def setup_inputs(seed: int = 0) -> dict:
    key = jax.random.key(seed)
    ks = iter(jax.random.split(key, 48))

    def nrm(shape, s):
        return s * jax.random.normal(next(ks), shape, F32)

    G, P = S5_GROUPS, S5_STATE
    x_prompt = nrm((BATCH, SEQ, D_MODEL), 1.0)
    x_sample = nrm((DEC_BATCH, DEC_SEQ, D_MODEL), 1.0)
    cache_na_k = nrm((DEC_BATCH, DEPTH, NA_HEADS, PAST_LEN, NA_HD), 1.0)
    cache_na_v = nrm((DEC_BATCH, DEPTH, NA_HEADS, PAST_LEN, NA_HD), 1.0)
    cache_mla_kv = nrm((DEC_BATCH, DEPTH, PAST_LEN, MLA_KV_LORA + MLA_ROPE), 1.0)
    state_s5_re = nrm((DEC_BATCH, DEPTH, 2, G, P), 0.3)
    state_s5_im = nrm((DEC_BATCH, DEPTH, 2, G, P), 0.3)
    c = nrm((DEC_BATCH, D_MODEL), 1.0)
    c_ctx = nrm((D_MODEL,), 1.0)
    w_mod = nrm((DEPTH, D_MODEL, 6 * D_MODEL), 0.5 * D_MODEL ** -0.5)
    b_mod = nrm((DEPTH, 6 * D_MODEL), 0.02)
    norm1_g = 1.0 + nrm((DEPTH, D_MODEL), 0.02)
    norm2_g = 1.0 + nrm((DEPTH, D_MODEL), 0.02)
    w_in = nrm((DEPTH, D_MODEL, D_IN), D_MODEL ** -0.5)
    w_out = nrm((DEPTH, D_MIX, D_MODEL), D_MIX ** -0.5)
    s5_lambda_re = -0.5 + nrm((DEPTH, 2, G, P), 0.01)
    s5_lambda_im = jnp.pi * jnp.arange(P, dtype=F32) + nrm((DEPTH, 2, G, P), 0.01)
    s5_log_dt = jax.random.uniform(next(ks), (DEPTH, 2, G), F32, math.log(1e-3), math.log(1e-1))
    s5_b_re = nrm((DEPTH, 2, G, P, S5_CH), (2 * S5_CH) ** -0.5)
    s5_b_im = nrm((DEPTH, 2, G, P, S5_CH), (2 * S5_CH) ** -0.5)
    s5_c_re = nrm((DEPTH, 2, G, S5_CH, P), (2 * P) ** -0.5)
    s5_c_im = nrm((DEPTH, 2, G, S5_CH, P), (2 * P) ** -0.5)
    s5_d = nrm((DEPTH, S5_WIDTH), 1.0)
    s5_glu_w = nrm((DEPTH, S5_WIDTH, S5_WIDTH), S5_WIDTH ** -0.5)
    s5_glu_b = nrm((DEPTH, S5_WIDTH), 0.02)
    na_rpb = nrm((DEPTH, NA_HEADS, 2 * WIN_H - 1, 2 * WIN_W - 1), 0.1)
    mla_q_norm_g = 1.0 + nrm((DEPTH, MLA_Q_LORA), 0.02)
    mla_w_qb = nrm((DEPTH, MLA_Q_LORA, MLA_HEADS * (MLA_NOPE + MLA_ROPE)), MLA_Q_LORA ** -0.5)
    mla_kv_norm_g = 1.0 + nrm((DEPTH, MLA_KV_LORA), 0.02)
    mla_w_kvb = nrm((DEPTH, MLA_KV_LORA, MLA_HEADS * (MLA_NOPE + MLA_V)), MLA_KV_LORA ** -0.5)
    router_w = nrm((DEPTH, D_MODEL, N_EXPERTS), D_MODEL ** -0.5)
    router_b = nrm((DEPTH, N_EXPERTS), 0.01)
    moe_w_gu = nrm((DEPTH, N_EXPERTS, D_MODEL, 2 * D_FF), D_MODEL ** -0.5)
    moe_b_gu = nrm((DEPTH, N_EXPERTS, 2 * D_FF), 0.01)
    moe_w_down = nrm((DEPTH, N_EXPERTS, D_FF, D_MODEL), D_FF ** -0.5)
    moe_b_down = nrm((DEPTH, N_EXPERTS, D_MODEL), 0.01)
    final_norm_g = 1.0 + nrm((D_MODEL,), 0.02)
    return {'x_prompt': x_prompt, 'x_sample': x_sample, 'cache_na_k': cache_na_k, 'cache_na_v': cache_na_v,
            'cache_mla_kv': cache_mla_kv, 'state_s5_re': state_s5_re, 'state_s5_im': state_s5_im,
            'c': c, 'c_ctx': c_ctx, 'w_mod': w_mod, 'b_mod': b_mod, 'norm1_g': norm1_g, 'norm2_g': norm2_g,
            'w_in': w_in, 'w_out': w_out, 's5_lambda_re': s5_lambda_re, 's5_lambda_im': s5_lambda_im,
            's5_log_dt': s5_log_dt, 's5_b_re': s5_b_re, 's5_b_im': s5_b_im, 's5_c_re': s5_c_re,
            's5_c_im': s5_c_im, 's5_d': s5_d, 's5_glu_w': s5_glu_w, 's5_glu_b': s5_glu_b, 'na_rpb': na_rpb,
            'mla_q_norm_g': mla_q_norm_g, 'mla_w_qb': mla_w_qb, 'mla_kv_norm_g': mla_kv_norm_g,
            'mla_w_kvb': mla_w_kvb, 'router_w': router_w, 'router_b': router_b, 'moe_w_gu': moe_w_gu,
            'moe_b_gu': moe_b_gu, 'moe_w_down': moe_w_down, 'moe_b_down': moe_b_down,
            'final_norm_g': final_norm_g}


def reference(x_prompt, x_sample, cache_na_k, cache_na_v, cache_mla_kv, state_s5_re, state_s5_im,
              c, c_ctx, w_mod, b_mod, norm1_g, norm2_g, w_in, w_out, s5_lambda_re, s5_lambda_im,
              s5_log_dt, s5_b_re, s5_b_im, s5_c_re, s5_c_im, s5_d, s5_glu_w, s5_glu_b, na_rpb,
              mla_q_norm_g, mla_w_qb, mla_kv_norm_g, mla_w_kvb, router_w, router_b, moe_w_gu,
              moe_b_gu, moe_w_down, moe_b_down, final_norm_g):
    xp, xs = x_prompt, x_sample
    na_k_list, na_v_list, mla_list, s5_re_list, s5_im_list = [], [], [], [], []
    for l in range(DEPTH):
        lw = {'w_in': w_in[l], 'w_out': w_out[l], 'lam_re': s5_lambda_re[l], 'lam_im': s5_lambda_im[l],
              'log_dt': s5_log_dt[l], 'b_re': s5_b_re[l], 'b_im': s5_b_im[l], 'c_re': s5_c_re[l],
              'c_im': s5_c_im[l], 's5_d': s5_d[l], 'glu_w': s5_glu_w[l], 'glu_b': s5_glu_b[l],
              'rpb': na_rpb[l], 'q_norm_g': mla_q_norm_g[l], 'w_qb': mla_w_qb[l],
              'kv_norm_g': mla_kv_norm_g[l], 'w_kvb': mla_w_kvb[l]}
        moe_args = (router_w[l], router_b[l], moe_w_gu[l], moe_b_gu[l], moe_w_down[l], moe_b_down[l])
        m_ctx = modulation(c_ctx[None, :], w_mod[l], b_mod[l])
        m_lat = modulation(c, w_mod[l], b_mod[l])
        h = adaln(xp, norm1_g[l], m_ctx[0], m_ctx[1])
        out, nk, nv, nm, sr, si = mixers_context(h, lw)
        xp = xp + m_ctx[2] * out
        xp = xp + m_ctx[5] * moe(adaln(xp, norm2_g[l], m_ctx[3], m_ctx[4]), *moe_args)
        na_k_list.append(nk)
        na_v_list.append(nv)
        mla_list.append(nm)
        s5_re_list.append(sr)
        s5_im_list.append(si)
        h = adaln(xs, norm1_g[l], m_lat[0], m_lat[1])
        out = mixers_latent(h, lw, cache_na_k[:, l], cache_na_v[:, l], cache_mla_kv[:, l],
                            state_s5_re[:, l], state_s5_im[:, l])
        xs = xs + m_lat[2] * out
        xs = xs + m_lat[5] * moe(adaln(xs, norm2_g[l], m_lat[3], m_lat[4]), *moe_args)
    y_prompt = rmsnorm(xp, final_norm_g)
    y_sample = rmsnorm(xs, final_norm_g)
    new_na_k = jnp.stack(na_k_list, axis=1)
    new_na_v = jnp.stack(na_v_list, axis=1)
    new_mla_kv = jnp.stack(mla_list, axis=1)
    new_s5_re = jnp.stack(s5_re_list, axis=1)
    new_s5_im = jnp.stack(s5_im_list, axis=1)
    return (y_prompt, y_sample, new_na_k, new_na_v, new_mla_kv, new_s5_re, new_s5_im)
```

```python
import functools
import math

import numpy as np
import jax
import jax.numpy as jnp
from jax import lax
from jax.experimental import pallas as pl
from jax.experimental.pallas import tpu as pltpu

F32 = jnp.float32
BF16 = jnp.bfloat16

D_MODEL = 1024
N_CTX, L_CTX = 32, 256
N_LAT, L_LAT = 2, 2048
DEPTH = 4
PAST = 256
GRID_W = 64
S5_W, S5_G, S5_P, S5_C = 256, 16, 64, 16
NA_H, NA_D = 4, 64
WIN_H, WIN_W = 8, 16
MLA_H, MLA_NOPE, MLA_ROPE, MLA_V = 8, 64, 32, 64
MLA_QL, MLA_KVL = 256, 128
MLA_SCALE = (MLA_NOPE + MLA_ROPE) ** -0.5
NA_SCALE = NA_D ** -0.5
OFF_NA = S5_W
OFF_QA = OFF_NA + 3 * NA_H * NA_D
OFF_KVA = OFF_QA + MLA_QL
D_IN = OFF_KVA + MLA_KVL + MLA_ROPE
N_EXP, TOP_K, D_FF = 32, 4, 1024
SWIGLU_ALPHA, SWIGLU_LIMIT = 1.702, 7.0
ROPE_BASE = 10000.0
EPS = 1e-6

TM = 256
T_CTX = N_CTX * L_CTX
T_LAT = N_LAT * L_LAT
T_ALL = T_CTX + T_LAT
CTX_TILES = T_CTX // TM
LAT_TILES = L_LAT // TM
N_TILES = T_ALL // TM
MOD_ROWS = 8
MLA_PAD = 128
D_IN_EXT = D_IN + MLA_ROPE
SCAN_ROWS = 8
SCAN_CH = 64
S5_STATE_W = 2 * S5_G * S5_P
MOE_BM = 256
MOE_ROWS = T_ALL * TOP_K + N_EXP * MOE_BM
MOE_BLOCKS = MOE_ROWS // MOE_BM
NEG = -1e30
VMEM_LIMIT = 56 * 1024 * 1024


def _cparams(sem):
    return pltpu.CompilerParams(dimension_semantics=sem, vmem_limit_bytes=VMEM_LIMIT)


def _mod_row(i):
    return jnp.maximum(i - (CTX_TILES - LAT_TILES), 0) // LAT_TILES


def _rope_block(i):
    return jnp.where(i < CTX_TILES, 0, 1 + (i - CTX_TILES) % LAT_TILES)


def _rms(x, g):
    return x * lax.rsqrt(jnp.mean(x * x, axis=-1, keepdims=True) + EPS) * g


def _mod_kernel(cv_ref, w_ref, b_ref, o_ref):
    s = jax.nn.silu(cv_ref[...]).astype(BF16)
    o_ref[0] = jnp.dot(s, w_ref[0].astype(BF16), preferred_element_type=F32) + b_ref[0]


def _modulation(cv, w_mod, b_mod):
    tn = 1536
    return pl.pallas_call(
        _mod_kernel,
        grid=(DEPTH, 6 * D_MODEL // tn),
        in_specs=[pl.BlockSpec((MOD_ROWS, D_MODEL), lambda l, j: (0, 0)),
                  pl.BlockSpec((1, D_MODEL, tn), lambda l, j: (l, 0, j)),
                  pl.BlockSpec((1, 1, tn), lambda l, j: (l, 0, j))],
        out_specs=pl.BlockSpec((1, MOD_ROWS, tn), lambda l, j: (l, 0, j)),
        out_shape=jax.ShapeDtypeStruct((DEPTH, MOD_ROWS, 6 * D_MODEL), F32),
        compiler_params=_cparams(("arbitrary", "arbitrary")),
        name="modulation",
    )(cv, w_mod, b_mod.reshape(DEPTH, 1, 6 * D_MODEL))


def _proj_kernel(x_ref, mod_ref, g1_ref, win_ref, qg_ref, wq_ref, kvg_ref, wkk_ref, wv_ref,
                 cq_ref, sq_ref, ck_ref, sk_ref,
                 u_ref, naq_ref, nak_ref, nav_ref, mq_ref, mk_ref, mv_ref, ckv_ref):
    x = x_ref[...]
    mod = mod_ref[0]
    h = _rms(x, g1_ref[...]) * (1.0 + mod[1:2, :]) + mod[0:1, :]
    p = jnp.dot(h.astype(BF16), win_ref[...], preferred_element_type=F32)
    u_ref[...] = p[:, :OFF_NA]
    for hh in range(NA_H):
        naq_ref[0, hh] = p[:, OFF_NA + hh * NA_D: OFF_NA + (hh + 1) * NA_D].astype(BF16)
        nak_ref[0, hh] = p[:, OFF_NA + NA_H * NA_D + hh * NA_D: OFF_NA + NA_H * NA_D + (hh + 1) * NA_D]
        nav_ref[0, hh] = p[:, OFF_NA + 2 * NA_H * NA_D + hh * NA_D: OFF_NA + 2 * NA_H * NA_D + (hh + 1) * NA_D]
    qn = _rms(p[:, OFF_QA:OFF_KVA], qg_ref[...]).astype(BF16)
    qq = jnp.dot(qn, wq_ref[...], preferred_element_type=F32)
    cq = cq_ref[...]
    sq = sq_ref[...]
    for hh in range(MLA_H):
        qa = qq[:, hh * MLA_PAD:(hh + 1) * MLA_PAD]
        qb = qq[:, (MLA_H + hh) * MLA_PAD:(MLA_H + hh + 1) * MLA_PAD]
        mq_ref[0, hh] = (qa * cq + qb * sq).astype(BF16)
    ckv = _rms(p[:, OFF_KVA:OFF_KVA + MLA_KVL], kvg_ref[...])
    kr = p[:, OFF_KVA + MLA_KVL:D_IN]
    kr_partner = p[:, D_IN:D_IN_EXT]
    kr_rot = kr * ck_ref[...] + kr_partner * sk_ref[...]
    ckv_ref[...] = jnp.concatenate([ckv, kr], axis=-1)
    kin = jnp.concatenate([ckv, kr_rot], axis=-1).astype(BF16)
    kk = jnp.dot(kin, wkk_ref[...], preferred_element_type=F32)
    vv = jnp.dot(kin[:, :MLA_KVL], wv_ref[...], preferred_element_type=F32)
    for hh in range(MLA_H):
        mk_ref[0, hh] = kk[:, hh * MLA_PAD:(hh + 1) * MLA_PAD].astype(BF16)
        mv_ref[0, hh] = vv[:, hh * MLA_V:(hh + 1) * MLA_V].astype(BF16)


def _project(x, mod_l, g1, w_in_ext, qg, wq2, kvg, wkk, wv, cq, sq, ck, sk):
    const2 = lambda i: (0, 0)
    tile2 = lambda i: (i, 0)
    head4 = lambda i: (i, 0, 0, 0)
    rope2 = lambda i: (_rope_block(i), 0)
    return pl.pallas_call(
        _proj_kernel,
        grid=(N_TILES,),
        in_specs=[pl.BlockSpec((TM, D_MODEL), tile2),
                  pl.BlockSpec((1, 6, D_MODEL), lambda i: (_mod_row(i), 0, 0)),
                  pl.BlockSpec((1, D_MODEL), const2),
                  pl.BlockSpec((D_MODEL, D_IN_EXT), const2),
                  pl.BlockSpec((1, MLA_QL), const2),
                  pl.BlockSpec((MLA_QL, 2 * MLA_H * MLA_PAD), const2),
                  pl.BlockSpec((1, MLA_KVL), const2),
                  pl.BlockSpec((MLA_KVL + MLA_ROPE, MLA_H * MLA_PAD), const2),
                  pl.BlockSpec((MLA_KVL, MLA_H * MLA_V), const2),
                  pl.BlockSpec((TM, MLA_PAD), rope2),
                  pl.BlockSpec((TM, MLA_PAD), rope2),
                  pl.BlockSpec((TM, MLA_ROPE), rope2),
                  pl.BlockSpec((TM, MLA_ROPE), rope2)],
        out_specs=[pl.BlockSpec((TM, S5_W), tile2),
                   pl.BlockSpec((1, NA_H, TM, NA_D), head4),
                   pl.BlockSpec((1, NA_H, TM, NA_D), head4),
                   pl.BlockSpec((1, NA_H, TM, NA_D), head4),
                   pl.BlockSpec((1, MLA_H, TM, MLA_PAD), head4),
                   pl.BlockSpec((1, MLA_H, TM, MLA_PAD), head4),
                   pl.BlockSpec((1, MLA_H, TM, MLA_V), head4),
                   pl.BlockSpec((TM, MLA_KVL + MLA_ROPE), tile2)],
        out_shape=[jax.ShapeDtypeStruct((T_ALL, S5_W), F32),
                   jax.ShapeDtypeStruct((N_TILES, NA_H, TM, NA_D), BF16),
                   jax.ShapeDtypeStruct((N_TILES, NA_H, TM, NA_D), F32),
                   jax.ShapeDtypeStruct((N_TILES, NA_H, TM, NA_D), F32),
                   jax.ShapeDtypeStruct((N_TILES, MLA_H, TM, MLA_PAD), BF16),
                   jax.ShapeDtypeStruct((N_TILES, MLA_H, TM, MLA_PAD), BF16),
                   jax.ShapeDtypeStruct((N_TILES, MLA_H, TM, MLA_V), BF16),
                   jax.ShapeDtypeStruct((T_ALL, MLA_KVL + MLA_ROPE), F32)],
        compiler_params=_cparams(("arbitrary",)),
        name="proj",
    )(x, mod_l, g1, w_in_ext, qg, wq2, kvg, wkk, wv, cq, sq, ck, sk)


def _cache_kv_kernel(c_ref, wkk_ref, wv_ref, k_ref, v_ref):
    cin = c_ref[0, 0].astype(BF16)
    kk = jnp.dot(cin, wkk_ref[0], preferred_element_type=F32)
    vv = jnp.dot(cin[:, :MLA_KVL], wv_ref[0], preferred_element_type=F32)
    for hh in range(MLA_H):
        k_ref[0, 0, hh] = kk[:, hh * MLA_PAD:(hh + 1) * MLA_PAD].astype(BF16)
        v_ref[0, 0, hh] = vv[:, hh * MLA_V:(hh + 1) * MLA_V].astype(BF16)


def _cache_kv(cache_mla_kv, wkk_all, wv_all):
    return pl.pallas_call(
        _cache_kv_kernel,
        grid=(N_LAT, DEPTH),
        in_specs=[pl.BlockSpec((1, 1, PAST, MLA_KVL + MLA_ROPE), lambda n, l: (n, l, 0, 0)),
                  pl.BlockSpec((1, MLA_KVL + MLA_ROPE, MLA_H * MLA_PAD), lambda n, l: (l, 0, 0)),
                  pl.BlockSpec((1, MLA_KVL, MLA_H * MLA_V), lambda n, l: (l, 0, 0))],
        out_specs=[pl.BlockSpec((1, 1, MLA_H, PAST, MLA_PAD), lambda n, l: (n, l, 0, 0, 0)),
                   pl.BlockSpec((1, 1, MLA_H, PAST, MLA_V), lambda n, l: (n, l, 0, 0, 0))],
        out_shape=[jax.ShapeDtypeStruct((N_LAT, DEPTH, MLA_H, PAST, MLA_PAD), BF16),
                   jax.ShapeDtypeStruct((N_LAT, DEPTH, MLA_H, PAST, MLA_V), BF16)],
        compiler_params=_cparams(("arbitrary", "arbitrary")),
        name="cache_kv",
    )(cache_mla_kv, wkk_all, wv_all)


def _s5_disc_kernel(lr_ref, li_ref, ldt_ref, bre_ref, bim_ref, a_ref, bb_ref):
    lr = lr_ref[...]
    li = li_ref[...]
    dt = jnp.exp(ldt_ref[...])
    mag = jnp.exp(lr * dt)
    a_re = mag * jnp.cos(li * dt)
    a_im = mag * jnp.sin(li * dt)
    den = lr * lr + li * li
    nr = a_re - 1.0
    k_re = (nr * lr + a_im * li) / den
    k_im = (a_im * lr - nr * li) / den
    a_ref[0] = a_re
    a_ref[1] = a_im
    bre = bre_ref[...]
    bim = bim_ref[...]
    bb_ref[0] = k_re * bre - k_im * bim
    bb_ref[1] = k_re * bim + k_im * bre


def _s5_discretize(lam_re, lam_im, log_dt, b_re, b_im):
    g2 = 2 * S5_G
    a, bb = pl.pallas_call(
        _s5_disc_kernel,
        out_shape=[jax.ShapeDtypeStruct((2, g2, 1, S5_P), F32),
                   jax.ShapeDtypeStruct((2, g2, S5_C, S5_P), F32)],
        name="s5_disc",
    )(lam_re.reshape(g2, 1, S5_P), lam_im.reshape(g2, 1, S5_P), log_dt.reshape(g2, 1, 1),
      b_re.reshape(g2, S5_P, S5_C).transpose(0, 2, 1), b_im.reshape(g2, S5_P, S5_C).transpose(0, 2, 1))
    return a, bb


def _s5_matrices(a, bb, c_re, c_im):
    gp = S5_G * S5_P
    a_lanes = a.reshape(2, 2, S5_G, S5_P).transpose(1, 0, 2, 3).reshape(2, 2, gp)
    bbar = bb.reshape(2, 2, S5_G, S5_C, S5_P)
    eye = jnp.eye(S5_G, dtype=F32)
    wb = jnp.einsum('xdgcp,gh->dgcxhp', bbar, eye).reshape(2, S5_W, 2 * gp)
    cc = jnp.stack([c_re, -c_im], axis=1)
    wc = jnp.einsum('dxgcp,gh->dxhpgc', cc, eye).reshape(2, 2 * gp, S5_W)
    return a_lanes, wb.astype(BF16), wc.astype(BF16)


def _scan_kernel(u_ref, wb_ref, wc_ref, a_ref, h0_ref, y_ref, ht_ref, x_s, h_s, st_s):
    d = pl.program_id(0)
    c = pl.program_id(2)
    nc = pl.num_programs(2)
    gp = S5_G * S5_P

    @pl.when(c == 0)
    def _():
        st_s[...] = h0_ref[0]

    u = u_ref[...].reshape(SCAN_CH * SCAN_ROWS, S5_W).astype(BF16)
    x_s[...] = jnp.dot(u, wb_ref[0], preferred_element_type=F32)
    a_re = a_ref[0, 0:1, :]
    a_im = a_ref[0, 1:2, :]
    unroll = 4

    def outer(io, carry):
        h_re, h_im = carry
        for ii in range(unroll):
            i = io * unroll + ii
            t = jnp.where(d == 0, i, SCAN_CH - 1 - i)
            r = pl.multiple_of(t * SCAN_ROWS, SCAN_ROWS)
            x_re = x_s[pl.ds(r, SCAN_ROWS), pl.ds(0, gp)]
            x_im = x_s[pl.ds(r, SCAN_ROWS), pl.ds(gp, gp)]
            n_re = a_re * h_re - a_im * h_im + x_re
            n_im = a_re * h_im + a_im * h_re + x_im
            h_s[pl.ds(r, SCAN_ROWS), pl.ds(0, gp)] = n_re
            h_s[pl.ds(r, SCAN_ROWS), pl.ds(gp, gp)] = n_im
            h_re, h_im = n_re, n_im
        return h_re, h_im

    h_re, h_im = lax.fori_loop(0, SCAN_CH // unroll, outer, (st_s[:, pl.ds(0, gp)], st_s[:, pl.ds(gp, gp)]))
    st_s[:, pl.ds(0, gp)] = h_re
    st_s[:, pl.ds(gp, gp)] = h_im
    y = jnp.dot(h_s[...].astype(BF16), wc_ref[0], preferred_element_type=F32)
    y_ref[0] = y.reshape(SCAN_CH, SCAN_ROWS, S5_W)

    @pl.when(c == nc - 1)
    def _():
        ht_ref[0] = st_s[...]


def _s5_scan(u_tm, wb, wc, a_lanes, h0):
    L, n, _ = u_tm.shape
    nc = L // SCAN_CH
    ng = n // SCAN_ROWS

    def ceff(d, c):
        return c + d * (nc - 1 - 2 * c)

    return pl.pallas_call(
        _scan_kernel,
        grid=(2, ng, nc),
        in_specs=[pl.BlockSpec((SCAN_CH, SCAN_ROWS, S5_W), lambda d, g, c: (ceff(d, c), g, 0)),
                  pl.BlockSpec((1, S5_W, S5_STATE_W), lambda d, g, c: (d, 0, 0)),
                  pl.BlockSpec((1, S5_STATE_W, S5_W), lambda d, g, c: (d, 0, 0)),
                  pl.BlockSpec((1, 2, S5_G * S5_P), lambda d, g, c: (d, 0, 0)),
                  pl.BlockSpec((1, SCAN_ROWS, S5_STATE_W), lambda d, g, c: (d, g, 0))],
        out_specs=[pl.BlockSpec((1, SCAN_CH, SCAN_ROWS, S5_W), lambda d, g, c: (d, ceff(d, c), g, 0)),
                   pl.BlockSpec((1, SCAN_ROWS, S5_STATE_W), lambda d, g, c: (d, g, 0))],
        out_shape=[jax.ShapeDtypeStruct((2, L, n, S5_W), F32),
                   jax.ShapeDtypeStruct((2, n, S5_STATE_W), F32)],
        scratch_shapes=[pltpu.VMEM((SCAN_CH * SCAN_ROWS, S5_STATE_W), F32),
                        pltpu.VMEM((SCAN_CH * SCAN_ROWS, S5_STATE_W), F32),
                        pltpu.VMEM((SCAN_ROWS, S5_STATE_W), F32)],
        compiler_params=_cparams(("arbitrary", "arbitrary", "arbitrary")),
        name="s5_scan",
    )(u_tm, wb, wc, a_lanes, h0)


def _qk(q, k):
    return lax.dot_general(q, k, (((1,), (1,)), ((), ())), preferred_element_type=F32)


def _softmax_pv(s_list, v_list):
    m = s_list[0].max(axis=-1, keepdims=True)
    for s in s_list[1:]:
        m = jnp.maximum(m, s.max(axis=-1, keepdims=True))
    den = None
    acc = None
    for s, v in zip(s_list, v_list):
        e = jnp.exp(s - m)
        ds = e.sum(axis=-1, keepdims=True)
        pv = jnp.dot(e.astype(BF16), v, preferred_element_type=F32)
        den = ds if den is None else den + ds
        acc = pv if acc is None else acc + pv
    return acc / den


def _ctx_attn_kernel(nq_ref, nk_ref, nv_ref, mq_ref, mk_ref, mv_ref, na_ref, mla_ref):
    for hh in range(NA_H):
        s = _qk(nq_ref[0, hh], nk_ref[0, hh].astype(BF16)) * NA_SCALE
        na_ref[:, hh * NA_D:(hh + 1) * NA_D] = _softmax_pv([s], [nv_ref[0, hh].astype(BF16)])
    for hh in range(MLA_H):
        s = _qk(mq_ref[0, hh], mk_ref[0, hh]) * MLA_SCALE
        mla_ref[:, hh * MLA_V:(hh + 1) * MLA_V] = _softmax_pv([s], [mv_ref[0, hh]])


def _ctx_attention(naq, nak, nav, mq, mk, mv):
    head4 = lambda i: (i, 0, 0, 0)
    tile2 = lambda i: (i, 0)
    return pl.pallas_call(
        _ctx_attn_kernel,
        grid=(CTX_TILES,),
        in_specs=[pl.BlockSpec((1, NA_H, TM, NA_D), head4)] * 3
                 + [pl.BlockSpec((1, MLA_H, TM, MLA_PAD), head4)] * 2
                 + [pl.BlockSpec((1, MLA_H, TM, MLA_V), head4)],
        out_specs=[pl.BlockSpec((TM, NA_H * NA_D), tile2), pl.BlockSpec((TM, MLA_H * MLA_V), tile2)],
        out_shape=[jax.ShapeDtypeStruct((T_CTX, NA_H * NA_D), F32),
                   jax.ShapeDtypeStruct((T_CTX, MLA_H * MLA_V), F32)],
        compiler_params=_cparams(("arbitrary",)),
        name="ctx_attn",
    )(naq, nak, nav, mq, mk, mv)


def _na_lat_kernel(q_ref, k0_ref, k1_ref, k2_ref, v0_ref, v1_ref, v2_ref, kc_ref, vc_ref, b_ref, o_ref):
    for hh in range(NA_H):
        q = q_ref[0, hh]
        kb = jnp.concatenate([k0_ref[0, hh], k1_ref[0, hh], k2_ref[0, hh]], axis=0).astype(BF16)
        vb = jnp.concatenate([v0_ref[0, hh], v1_ref[0, hh], v2_ref[0, hh]], axis=0).astype(BF16)
        s_win = _qk(q, kb) * NA_SCALE + b_ref[0, hh]
        s_ctx = _qk(q, kc_ref[0, 0, hh].astype(BF16)) * NA_SCALE
        o_ref[:, hh * NA_D:(hh + 1) * NA_D] = _softmax_pv([s_win, s_ctx], [vb, vc_ref[0, 0, hh].astype(BF16)])


def _band_start(j):
    return jnp.clip(j - 1, 0, LAT_TILES - 3)


def _na_latent(naq, nak, nav, cache_k, cache_v, bias, layer):
    def qmap(n, j):
        return (CTX_TILES + n * LAT_TILES + j, 0, 0, 0)

    def kmap(off):
        return lambda n, j: (CTX_TILES + n * LAT_TILES + _band_start(j) + off, 0, 0, 0)

    def bmap(n, j):
        return (jnp.where(j == 0, 0, jnp.where(j == LAT_TILES - 1, 2, 1)), 0, 0, 0)

    blk = (1, NA_H, TM, NA_D)
    return pl.pallas_call(
        _na_lat_kernel,
        grid=(N_LAT, LAT_TILES),
        in_specs=[pl.BlockSpec(blk, qmap)]
                 + [pl.BlockSpec(blk, kmap(o)) for o in range(3)]
                 + [pl.BlockSpec(blk, kmap(o)) for o in range(3)]
                 + [pl.BlockSpec((1, 1, NA_H, PAST, NA_D), lambda n, j: (n, layer, 0, 0, 0))] * 2
                 + [pl.BlockSpec((1, NA_H, TM, 3 * TM), bmap)],
        out_specs=pl.BlockSpec((TM, NA_H * NA_D), lambda n, j: (n * LAT_TILES + j, 0)),
        out_shape=jax.ShapeDtypeStruct((T_LAT, NA_H * NA_D), F32),
        compiler_params=_cparams(("arbitrary", "arbitrary")),
        name="na_latent",
    )(naq, nak, nak, nak, nav, nav, nav, cache_k, cache_v, bias)


def _na_bias_tables(rpb):
    rows = L_LAT // GRID_W
    tabs = []
    for j, s in ((0, 0), (1, 0), (LAT_TILES - 1, LAT_TILES - 3)):
        r = 4 * j + np.arange(4)
        rs = np.clip(r - WIN_H // 2, 0, rows - WIN_H)
        krow = 4 * s + np.arange(12)
        w = np.arange(GRID_W)
        cs = np.clip(w - WIN_W // 2, 0, GRID_W - WIN_W)
        cc = np.arange(GRID_W)
        row_ok = (krow[None, :] >= rs[:, None]) & (krow[None, :] < rs[:, None] + WIN_H)
        col_ok = (cc[None, :] >= cs[:, None]) & (cc[None, :] < cs[:, None] + WIN_W)
        dr = np.clip(krow[None, :] - r[:, None] + WIN_H - 1, 0, 2 * WIN_H - 2)
        dc = np.clip(cc[None, :] - w[:, None] + WIN_W - 1, 0, 2 * WIN_W - 2)
        b = rpb[:, dr[:, None, :, None], dc[None, :, None, :]]
        ok = row_ok[:, None, :, None] & col_ok[None, :, None, :]
        b = jnp.where(jnp.asarray(ok)[None], b, NEG)
        tabs.append(b.reshape(NA_H, TM, 3 * TM))
    return jnp.stack(tabs, axis=0).astype(F32)


def _mla_lat_kernel(q_ref, k_ref, v_ref, kc_ref, vc_ref, o_ref):
    for hh in range(MLA_H):
        q = q_ref[0, hh]
        k = k_ref[0, :, hh].reshape(L_LAT, MLA_PAD)
        v = v_ref[0, :, hh].reshape(L_LAT, MLA_V)
        s_lat = _qk(q, k) * MLA_SCALE
        s_ctx = _qk(q, kc_ref[0, 0, hh]) * MLA_SCALE
        o_ref[:, hh * MLA_V:(hh + 1) * MLA_V] = _softmax_pv([s_lat, s_ctx], [v, vc_ref[0, 0, hh]])


def _mla_latent(mq, mk, mv, kc, vc, layer):
    mk_lat = mk[CTX_TILES:].reshape(N_LAT, LAT_TILES, MLA_H, TM, MLA_PAD)
    mv_lat = mv[CTX_TILES:].reshape(N_LAT, LAT_TILES, MLA_H, TM, MLA_V)
    return pl.pallas_call(
        _mla_lat_kernel,
        grid=(N_LAT, LAT_TILES),
        in_specs=[pl.BlockSpec((1, MLA_H, TM, MLA_PAD), lambda n, j: (CTX_TILES + n * LAT_TILES + j, 0, 0, 0)),
                  pl.BlockSpec((1, LAT_TILES, MLA_H, TM, MLA_PAD), lambda n, j: (n, 0, 0, 0, 0)),
                  pl.BlockSpec((1, LAT_TILES, MLA_H, TM, MLA_V), lambda n, j: (n, 0, 0, 0, 0)),
                  pl.BlockSpec((1, 1, MLA_H, PAST, MLA_PAD), lambda n, j: (n, layer, 0, 0, 0)),
                  pl.BlockSpec((1, 1, MLA_H, PAST, MLA_V), lambda n, j: (n, layer, 0, 0, 0))],
        out_specs=pl.BlockSpec((TM, MLA_H * MLA_V), lambda n, j: (n * LAT_TILES + j, 0)),
        out_shape=jax.ShapeDtypeStruct((T_LAT, MLA_H * MLA_V), F32),
        compiler_params=_cparams(("arbitrary", "arbitrary")),
        name="mla_latent",
    )(mq, mk_lat, mv_lat, kc, vc)


def _post_kernel(y_ref, u_ref, d_ref, gw_ref, gb_ref, na_ref, mla_ref, wo_ref, x_ref, mod_ref, g2_ref,
                 rw_ref, rb_ref, x1_ref, h2_ref, te_ref, tg_ref, rk_ref, cnt_ref):
    ys = y_ref[0] + y_ref[1] + u_ref[...] * d_ref[...]
    g = jax.nn.gelu(ys)
    s5o = g * jax.nn.sigmoid(jnp.dot(g.astype(BF16), gw_ref[...], preferred_element_type=F32) + gb_ref[...])
    out = (jnp.dot(s5o.astype(BF16), wo_ref[0:S5_W, :], preferred_element_type=F32)
           + jnp.dot(na_ref[...].astype(BF16), wo_ref[S5_W:S5_W + NA_H * NA_D, :], preferred_element_type=F32)
           + jnp.dot(mla_ref[...].astype(BF16), wo_ref[S5_W + NA_H * NA_D:, :], preferred_element_type=F32))
    mod = mod_ref[0]
    x1 = x_ref[...] + mod[2:3, :] * out
    x1_ref[...] = x1
    h2 = _rms(x1, g2_ref[...]) * (1.0 + mod[4:5, :]) + mod[3:4, :]
    h2_ref[...] = h2
    logits = jnp.dot(h2, rw_ref[...], preferred_element_type=F32, precision=lax.Precision.HIGHEST) + rb_ref[...]
    lane_i = lax.broadcasted_iota(jnp.int32, (TM, 128), 1)
    lane = lane_i.astype(F32)
    cur = jnp.where(lane_i < N_EXP, logits, -jnp.inf)
    te = jnp.zeros((TM, 128), F32)
    tv = jnp.zeros((TM, 128), F32)
    hot = jnp.zeros((TM, 128), F32)
    idxs = []
    top = None
    for k in range(TOP_K):
        m = cur.max(axis=-1, keepdims=True)
        idx = jnp.where(cur == m, lane, 128.0).min(axis=-1, keepdims=True)
        sel = lane == idx
        top = m if top is None else top
        idxs.append(idx)
        te = jnp.where(lane_i == k, idx, te)
        tv = jnp.where(lane_i == k, jnp.exp(m - top), tv)
        hot = jnp.where(sel, 1.0, hot)
        cur = jnp.where(sel, -jnp.inf, cur)
    te_ref[...] = te.astype(jnp.int32)
    tg_ref[...] = tv / tv.sum(axis=-1, keepdims=True)
    row = lax.broadcasted_iota(jnp.int32, (TM, TM), 0)
    col = lax.broadcasted_iota(jnp.int32, (TM, TM), 1)
    tri = jnp.where(col < row, 1.0, 0.0).astype(BF16)
    before = jnp.dot(tri, hot.astype(BF16), preferred_element_type=F32)
    rk = jnp.zeros((TM, 128), F32)
    for k in range(TOP_K):
        rk = jnp.where(lane_i == k, jnp.where(lane == idxs[k], before, 0.0).sum(axis=-1, keepdims=True), rk)
    rk_ref[...] = rk.astype(jnp.int32)
    cnt_ref[0] = jnp.broadcast_to(hot.sum(axis=0, keepdims=True), (8, 128)).astype(jnp.int32)


def _post(y2, u, s5_d, glu_w, glu_b, na_out, mla_out, w_out, x, mod_l, g2, rw, rb):
    const2 = lambda i: (0, 0)
    tile2 = lambda i: (i, 0)
    return pl.pallas_call(
        _post_kernel,
        grid=(N_TILES,),
        in_specs=[pl.BlockSpec((2, TM, S5_W), lambda i: (0, i, 0)),
                  pl.BlockSpec((TM, S5_W), tile2),
                  pl.BlockSpec((1, S5_W), const2),
                  pl.BlockSpec((S5_W, S5_W), const2),
                  pl.BlockSpec((1, S5_W), const2),
                  pl.BlockSpec((TM, NA_H * NA_D), tile2),
                  pl.BlockSpec((TM, MLA_H * MLA_V), tile2),
                  pl.BlockSpec((D_MODEL, D_MODEL), const2),
                  pl.BlockSpec((TM, D_MODEL), tile2),
                  pl.BlockSpec((1, 6, D_MODEL), lambda i: (_mod_row(i), 0, 0)),
                  pl.BlockSpec((1, D_MODEL), const2),
                  pl.BlockSpec((D_MODEL, 128), const2),
                  pl.BlockSpec((1, 128), const2)],
        out_specs=[pl.BlockSpec((TM, D_MODEL), tile2),
                   pl.BlockSpec((TM, D_MODEL), tile2),
                   pl.BlockSpec((TM, 128), tile2),
                   pl.BlockSpec((TM, 128), tile2),
                   pl.BlockSpec((TM, 128), tile2),
                   pl.BlockSpec((1, 8, 128), lambda i: (i, 0, 0))],
        out_shape=[jax.ShapeDtypeStruct((T_ALL, D_MODEL), F32),
                   jax.ShapeDtypeStruct((T_ALL, D_MODEL), F32),
                   jax.ShapeDtypeStruct((T_ALL, 128), jnp.int32),
                   jax.ShapeDtypeStruct((T_ALL, 128), F32),
                   jax.ShapeDtypeStruct((T_ALL, 128), jnp.int32),
                   jax.ShapeDtypeStruct((N_TILES, 8, 128), jnp.int32)],
        compiler_params=_cparams(("arbitrary",)),
        name="post",
    )(y2, u, s5_d, glu_w, glu_b, na_out, mla_out, w_out, x, mod_l, g2, rw, rb)


def _dispatch_kernel(pos_ref, h_ref, xp_in_ref, xp_ref, sem):
    del xp_in_ref
    base = pl.program_id(0) * (TM * TOP_K)

    def issue(t, carry):
        for k in range(TOP_K):
            p = pos_ref[base + t * TOP_K + k]
            pltpu.make_async_copy(h_ref.at[pl.ds(t, 1), :], xp_ref.at[pl.ds(p, 1), :], sem).start()
        return carry

    lax.fori_loop(0, TM, issue, 0)
    for k in range(TOP_K):
        pltpu.make_async_copy(h_ref, xp_ref.at[pl.ds(0, TM), :], sem).wait()


def _dispatch(pos_flat, h2, xp_init):
    return pl.pallas_call(
        _dispatch_kernel,
        grid_spec=pltpu.PrefetchScalarGridSpec(
            num_scalar_prefetch=1, grid=(N_TILES,),
            in_specs=[pl.BlockSpec((TM, D_MODEL), lambda i, pos: (i, 0)),
                      pl.BlockSpec(memory_space=pl.ANY)],
            out_specs=pl.BlockSpec(memory_space=pl.ANY),
            scratch_shapes=[pltpu.SemaphoreType.DMA(())]),
        out_shape=jax.ShapeDtypeStruct((MOE_ROWS, D_MODEL), F32),
        input_output_aliases={2: 0},
        compiler_params=_cparams(("arbitrary",)),
        name="moe_dispatch",
    )(pos_flat, h2, xp_init)


def _moe_kernel(be_ref, first_ref, nused_ref, x_ref, wgu_ref, bgu_ref, wd_ref, bd_ref, o_ref, wgu_s, wd_s):
    b = pl.program_id(0)

    @pl.when(first_ref[b] == 1)
    def _():
        wgu_s[...] = wgu_ref[0].astype(BF16)
        wd_s[...] = wd_ref[0].astype(BF16)

    @pl.when(b < nused_ref[0])
    def _():
        gu = jnp.dot(x_ref[...].astype(BF16), wgu_s[...], preferred_element_type=F32) + bgu_ref[0]
        gate = jnp.minimum(gu[:, :D_FF], SWIGLU_LIMIT)
        up = jnp.clip(gu[:, D_FF:], -SWIGLU_LIMIT, SWIGLU_LIMIT)
        act = (up + 1.0) * (gate * jax.nn.sigmoid(SWIGLU_ALPHA * gate))
        o_ref[...] = jnp.dot(act.astype(BF16), wd_s[...], preferred_element_type=F32) + bd_ref[0]

    @pl.when(b >= nused_ref[0])
    def _():
        o_ref[...] = jnp.zeros_like(o_ref)


def _moe_experts(block_e, first, nused, xp, w_gu, b_gu, w_down, b_down):
    def xmap(b, be, fi, nu):
        return (jnp.minimum(b, nu[0] - 1), 0)

    def wmap(b, be, fi, nu):
        return (be[b], 0, 0)

    return pl.pallas_call(
        _moe_kernel,
        grid_spec=pltpu.PrefetchScalarGridSpec(
            num_scalar_prefetch=3, grid=(MOE_BLOCKS,),
            in_specs=[pl.BlockSpec((MOE_BM, D_MODEL), xmap),
                      pl.BlockSpec((1, D_MODEL, 2 * D_FF), wmap),
                      pl.BlockSpec((1, 1, 2 * D_FF), wmap),
                      pl.BlockSpec((1, D_FF, D_MODEL), wmap),
                      pl.BlockSpec((1, 1, D_MODEL), wmap)],
            out_specs=pl.BlockSpec((MOE_BM, D_MODEL), lambda b, be, fi, nu: (b, 0)),
            scratch_shapes=[pltpu.VMEM((D_MODEL, 2 * D_FF), BF16), pltpu.VMEM((D_FF, D_MODEL), BF16)]),
        out_shape=jax.ShapeDtypeStruct((MOE_ROWS, D_MODEL), F32),
        compiler_params=_cparams(("arbitrary",)),
        name="moe_experts",
    )(block_e, first, nused, xp, w_gu, b_gu.reshape(N_EXP, 1, 2 * D_FF), w_down, b_down.reshape(N_EXP, 1, D_MODEL))


def _combine_kernel(pos_ref, yp_ref, tg_ref, x1_ref, mod_ref, o_ref, buf, sem):
    base = pl.program_id(0) * (TM * TOP_K)

    def issue(t, carry):
        for k in range(TOP_K):
            p = pos_ref[base + t * TOP_K + k]
            pltpu.make_async_copy(yp_ref.at[pl.ds(p, 1), :], buf.at[k, pl.ds(t, 1), :], sem).start()
        return carry

    lax.fori_loop(0, TM, issue, 0)
    for k in range(TOP_K):
        pltpu.make_async_copy(yp_ref.at[pl.ds(0, TM), :], buf.at[k], sem).wait()
    tg = tg_ref[...]
    y = buf[0] * tg[:, 0:1]
    for k in range(1, TOP_K):
        y = y + buf[k] * tg[:, k:k + 1]
    o_ref[...] = x1_ref[...] + mod_ref[0][5:6, :] * y


def _combine(pos_flat, yp, tg, x1, mod_l):
    return pl.pallas_call(
        _combine_kernel,
        grid_spec=pltpu.PrefetchScalarGridSpec(
            num_scalar_prefetch=1, grid=(N_TILES,),
            in_specs=[pl.BlockSpec(memory_space=pl.ANY),
                      pl.BlockSpec((TM, 128), lambda i, pos: (i, 0)),
                      pl.BlockSpec((TM, D_MODEL), lambda i, pos: (i, 0)),
                      pl.BlockSpec((1, 6, D_MODEL), lambda i, pos: (_mod_row(i), 0, 0))],
            out_specs=pl.BlockSpec((TM, D_MODEL), lambda i, pos: (i, 0)),
            scratch_shapes=[pltpu.VMEM((TOP_K, TM, D_MODEL), F32), pltpu.SemaphoreType.DMA(())]),
        out_shape=jax.ShapeDtypeStruct((T_ALL, D_MODEL), F32),
        compiler_params=_cparams(("arbitrary",)),
        name="moe_combine",
    )(pos_flat, yp, tg, x1, mod_l)


def _moe_plan(te, rk, cnt):
    tile_cnt = cnt[:, 0, :N_EXP]
    counts = tile_cnt.sum(axis=0)
    padded = (counts + MOE_BM - 1) // MOE_BM * MOE_BM
    pad_end = jnp.cumsum(padded)
    pad_start = pad_end - padded
    tile_off = jnp.cumsum(tile_cnt, axis=0) - tile_cnt
    base = (pad_start[None, :] + tile_off).astype(jnp.int32)
    e = te[:, :TOP_K].reshape(N_TILES, TM, TOP_K)
    hot = e[..., None] == jnp.arange(N_EXP, dtype=jnp.int32)
    pos = jnp.sum(jnp.where(hot, base[:, None, None, :], 0), axis=-1) + rk[:, :TOP_K].reshape(N_TILES, TM, TOP_K)
    nused = (pad_end[-1] // MOE_BM).astype(jnp.int32)
    blk = jnp.arange(MOE_BLOCKS, dtype=jnp.int32)
    be = jnp.minimum(jnp.searchsorted(pad_end, blk * MOE_BM, side='right'), N_EXP - 1).astype(jnp.int32)
    be = jnp.where(blk < nused, be, be[nused - 1])
    first = jnp.concatenate([jnp.ones((1,), jnp.int32), (be[1:] != be[:-1]).astype(jnp.int32)])
    return pos.reshape(-1).astype(jnp.int32), be, first, nused.reshape(1)


def _final_kernel(x_ref, g_ref, o_ref):
    o_ref[...] = _rms(x_ref[...], g_ref[...])


def _final_norm(x, g):
    return pl.pallas_call(
        _final_kernel,
        grid=(N_TILES,),
        in_specs=[pl.BlockSpec((TM, D_MODEL), lambda i: (i, 0)), pl.BlockSpec((1, D_MODEL), lambda i: (0, 0))],
        out_specs=pl.BlockSpec((TM, D_MODEL), lambda i: (i, 0)),
        out_shape=jax.ShapeDtypeStruct((T_ALL, D_MODEL), F32),
        compiler_params=_cparams(("arbitrary",)),
        name="final_norm",
    )(x, g)


_ROPE_PERM = np.concatenate([np.arange(8, 16), np.arange(0, 8), np.arange(24, 32), np.arange(16, 24)])


def _rope_tables():
    half = MLA_ROPE // 2
    t = jnp.arange(L_LAT)
    row = (t // GRID_W).astype(F32)
    col = (t % GRID_W).astype(F32)
    inv = ROPE_BASE ** (-jnp.arange(0, half, 2, dtype=F32) / half)

    def part(pos):
        ang = pos[:, None] * inv[None, :]
        c, s = jnp.cos(ang), jnp.sin(ang)
        return jnp.concatenate([c, c], axis=-1), jnp.concatenate([-s, s], axis=-1)

    cr, sr = part(row)
    cc, sc = part(col)
    cos32 = jnp.concatenate([cr, cc], axis=-1)
    sin32 = jnp.concatenate([sr, sc], axis=-1)
    ck = jnp.concatenate([jnp.ones((TM, MLA_ROPE), F32), cos32], axis=0)
    sk = jnp.concatenate([jnp.zeros((TM, MLA_ROPE), F32), sin32], axis=0)
    padw = MLA_PAD - MLA_NOPE - MLA_ROPE
    cq = jnp.concatenate([jnp.ones((TM + L_LAT, MLA_NOPE), F32), ck, jnp.ones((TM + L_LAT, padw), F32)], axis=-1)
    sq = jnp.concatenate([jnp.zeros((TM + L_LAT, MLA_NOPE), F32), sk, jnp.zeros((TM + L_LAT, padw), F32)], axis=-1)
    return cq, sq, ck, sk


def _mla_weights(w_qb, w_kvb):
    dq = MLA_NOPE + MLA_ROPE
    wq = w_qb.reshape(MLA_QL, MLA_H, dq)
    zpad = jnp.zeros((MLA_QL, MLA_H, MLA_PAD - dq), F32)
    q_main = jnp.concatenate([wq, zpad], axis=-1)
    q_part = jnp.concatenate([jnp.zeros((MLA_QL, MLA_H, MLA_NOPE), F32), wq[:, :, MLA_NOPE + _ROPE_PERM], zpad], axis=-1)
    wq2 = jnp.concatenate([q_main.reshape(MLA_QL, -1), q_part.reshape(MLA_QL, -1)], axis=-1).astype(BF16)
    wkv = w_kvb.reshape(MLA_KVL, MLA_H, MLA_NOPE + MLA_V)
    k_top = jnp.concatenate([wkv[:, :, :MLA_NOPE], jnp.zeros((MLA_KVL, MLA_H, MLA_PAD - MLA_NOPE), F32)], axis=-1)
    place = jnp.concatenate([jnp.zeros((MLA_ROPE, MLA_NOPE), F32), jnp.eye(MLA_ROPE, dtype=F32),
                             jnp.zeros((MLA_ROPE, MLA_PAD - dq), F32)], axis=-1)
    k_bot = jnp.broadcast_to(place[:, None, :], (MLA_ROPE, MLA_H, MLA_PAD))
    wkk = jnp.concatenate([k_top, k_bot], axis=0).reshape(MLA_KVL + MLA_ROPE, MLA_H * MLA_PAD).astype(BF16)
    wv = wkv[:, :, MLA_NOPE:].reshape(MLA_KVL, MLA_H * MLA_V).astype(BF16)
    return wq2, wkk, wv


def kernel(x_prompt, x_sample, cache_na_k, cache_na_v, cache_mla_kv, state_s5_re, state_s5_im, c, c_ctx, w_mod, b_mod, norm1_g, norm2_g, w_in, w_out, s5_lambda_re, s5_lambda_im, s5_log_dt, s5_b_re, s5_b_im, s5_c_re, s5_c_im, s5_d, s5_glu_w, s5_glu_b, na_rpb, mla_q_norm_g, mla_w_qb, mla_kv_norm_g, mla_w_kvb, router_w, router_b, moe_w_gu, moe_b_gu, moe_w_down, moe_b_down, final_norm_g):
    x = jnp.concatenate([x_prompt.reshape(T_CTX, D_MODEL), x_sample.reshape(T_LAT, D_MODEL)], axis=0)
    cv = jnp.concatenate([c_ctx[None, :], c, jnp.zeros((MOD_ROWS - 1 - N_LAT, D_MODEL), F32)], axis=0)
    mod = _modulation(cv, w_mod, b_mod).reshape(DEPTH, MOD_ROWS, 6, D_MODEL)
    cq, sq, ck, sk = _rope_tables()
    mla_w = [_mla_weights(mla_w_qb[l], mla_w_kvb[l]) for l in range(DEPTH)]
    kc_all, vc_all = _cache_kv(cache_mla_kv, jnp.stack([m[1] for m in mla_w]), jnp.stack([m[2] for m in mla_w]))
    gp = S5_G * S5_P
    xp_zero = jnp.zeros((MOE_ROWS, D_MODEL), F32)

    na_k_list, na_v_list, mla_list, s5_re_list, s5_im_list = [], [], [], [], []
    for l in range(DEPTH):
        mod_l = mod[l]
        wq2, wkk, wv = mla_w[l]
        w_in_ext = jnp.concatenate([w_in[l], w_in[l][:, OFF_KVA + MLA_KVL + _ROPE_PERM]], axis=-1).astype(BF16)
        u, naq, nak, nav, mq, mk, mv, ckv = _project(
            x, mod_l, norm1_g[l][None], w_in_ext, mla_q_norm_g[l][None], wq2, mla_kv_norm_g[l][None], wkk, wv,
            cq, sq, ck, sk)
        na_k_list.append(nak[:CTX_TILES])
        na_v_list.append(nav[:CTX_TILES])
        mla_list.append(ckv[:T_CTX].reshape(N_CTX, L_CTX, MLA_KVL + MLA_ROPE))

        a, bb = _s5_discretize(s5_lambda_re[l], s5_lambda_im[l], s5_log_dt[l], s5_b_re[l], s5_b_im[l])
        a_lanes, wb, wc = _s5_matrices(a, bb, s5_c_re[l], s5_c_im[l])
        u_ctx = u[:T_CTX].reshape(N_CTX, L_CTX, S5_W).transpose(1, 0, 2)
        y_ctx, ht_ctx = _s5_scan(u_ctx, wb, wc, a_lanes, jnp.zeros((2, N_CTX, S5_STATE_W), F32))
        u_lat = u[T_CTX:].reshape(N_LAT, L_LAT, S5_W).transpose(1, 0, 2)
        u_lat = jnp.pad(u_lat, ((0, 0), (0, SCAN_ROWS - N_LAT), (0, 0)))
        h0 = jnp.concatenate([state_s5_re[:, l].reshape(N_LAT, 2, gp), state_s5_im[:, l].reshape(N_LAT, 2, gp)], axis=-1)
        h0 = jnp.pad(h0.transpose(1, 0, 2), ((0, 0), (0, SCAN_ROWS - N_LAT), (0, 0)))
        y_lat, _ = _s5_scan(u_lat, wb, wc, a_lanes, h0)
        y2 = jnp.concatenate([y_ctx.transpose(0, 2, 1, 3).reshape(2, T_CTX, S5_W),
                              y_lat[:, :, :N_LAT].transpose(0, 2, 1, 3).reshape(2, T_LAT, S5_W)], axis=1)
        st = ht_ctx.reshape(2, N_CTX, 2, S5_G, S5_P).transpose(1, 0, 2, 3, 4)
        s5_re_list.append(st[:, :, 0])
        s5_im_list.append(st[:, :, 1])

        na_ctx, mla_ctx = _ctx_attention(naq, nak, nav, mq, mk, mv)
        na_lat = _na_latent(naq, nak, nav, cache_na_k, cache_na_v, _na_bias_tables(na_rpb[l]), l)
        mla_lat = _mla_latent(mq, mk, mv, kc_all, vc_all, l)
        na_out = jnp.concatenate([na_ctx, na_lat], axis=0)
        mla_out = jnp.concatenate([mla_ctx, mla_lat], axis=0)

        rw = jnp.pad(router_w[l], ((0, 0), (0, 128 - N_EXP)))
        rb = jnp.pad(router_b[l], (0, 128 - N_EXP))[None]
        x1, h2, te, tg, rk, cnt = _post(y2, u, s5_d[l][None], s5_glu_w[l].astype(BF16), s5_glu_b[l][None],
                                        na_out, mla_out, w_out[l].astype(BF16), x, mod_l, norm2_g[l][None], rw, rb)
        pos, be, first, nused = _moe_plan(te, rk, cnt)
        xp = _dispatch(pos, h2, xp_zero)
        yp = _moe_experts(be, first, nused, xp, moe_w_gu[l], moe_b_gu[l], moe_w_down[l], moe_b_down[l])
        x = _combine(pos, yp, tg, x1, mod_l)

    y = _final_norm(x, final_norm_g[None])
    y_prompt = y[:T_CTX].reshape(N_CTX, L_CTX, D_MODEL)
    y_sample = y[T_CTX:].reshape(N_LAT, L_LAT, D_MODEL)
    return (y_prompt, y_sample, jnp.stack(na_k_list, axis=1), jnp.stack(na_v_list, axis=1),
            jnp.stack(mla_list, axis=1), jnp.stack(s5_re_list, axis=1), jnp.stack(s5_im_list, axis=1))
```

```python
import functools
import math

import numpy as np
import jax
import jax.numpy as jnp
from jax import lax
from jax.experimental import pallas as pl
from jax.experimental.pallas import tpu as pltpu

F32 = jnp.float32
BF16 = jnp.bfloat16

D_MODEL = 1024
N_CTX, L_CTX = 32, 256
N_LAT, L_LAT = 2, 2048
DEPTH = 4
PAST = 256
GRID_W = 64
S5_W, S5_G, S5_P, S5_C = 256, 16, 64, 16
NA_H, NA_D = 4, 64
WIN_H, WIN_W = 8, 16
MLA_H, MLA_NOPE, MLA_ROPE, MLA_V = 8, 64, 32, 64
MLA_QL, MLA_KVL = 256, 128
MLA_SCALE = (MLA_NOPE + MLA_ROPE) ** -0.5
NA_SCALE = NA_D ** -0.5
OFF_NA = S5_W
OFF_QA = OFF_NA + 3 * NA_H * NA_D
OFF_KVA = OFF_QA + MLA_QL
D_IN = OFF_KVA + MLA_KVL + MLA_ROPE
N_EXP, TOP_K, D_FF = 32, 4, 1024
SWIGLU_ALPHA, SWIGLU_LIMIT = 1.702, 7.0
ROPE_BASE = 10000.0
EPS = 1e-6

TM = 256
T_CTX = N_CTX * L_CTX
T_LAT = N_LAT * L_LAT
T_ALL = T_CTX + T_LAT
CTX_TILES = T_CTX // TM
LAT_TILES = L_LAT // TM
N_TILES = T_ALL // TM
MOD_ROWS = 8
MLA_PAD = 128
D_IN_EXT = D_IN + MLA_ROPE
SCAN_ROWS = 8
SCAN_CH = 64
S5_STATE_W = 2 * S5_G * S5_P
MOE_BM = 256
MOE_ROWS = T_ALL * TOP_K + N_EXP * MOE_BM
MOE_BLOCKS = MOE_ROWS // MOE_BM
NEG = -1e30
VMEM_LIMIT = 56 * 1024 * 1024


def _cparams(sem):
    return pltpu.CompilerParams(dimension_semantics=sem, vmem_limit_bytes=VMEM_LIMIT)


def _mod_row(i):
    return jnp.maximum(i - (CTX_TILES - LAT_TILES), 0) // LAT_TILES


def _rope_block(i):
    return jnp.where(i < CTX_TILES, 0, 1 + (i - CTX_TILES) % LAT_TILES)


def _rms(x, g):
    return x * lax.rsqrt(jnp.mean(x * x, axis=-1, keepdims=True) + EPS) * g


def _mod_kernel(cv_ref, w_ref, b_ref, o_ref):
    s = jax.nn.silu(cv_ref[...]).astype(BF16)
    o_ref[0] = jnp.dot(s, w_ref[0].astype(BF16), preferred_element_type=F32) + b_ref[0]


def _modulation(cv, w_mod, b_mod):
    tn = 1536
    return pl.pallas_call(
        _mod_kernel,
        grid=(DEPTH, 6 * D_MODEL // tn),
        in_specs=[pl.BlockSpec((MOD_ROWS, D_MODEL), lambda l, j: (0, 0)),
                  pl.BlockSpec((1, D_MODEL, tn), lambda l, j: (l, 0, j)),
                  pl.BlockSpec((1, 1, tn), lambda l, j: (l, 0, j))],
        out_specs=pl.BlockSpec((1, MOD_ROWS, tn), lambda l, j: (l, 0, j)),
        out_shape=jax.ShapeDtypeStruct((DEPTH, MOD_ROWS, 6 * D_MODEL), F32),
        compiler_params=_cparams(("arbitrary", "arbitrary")),
        name="modulation",
    )(cv, w_mod, b_mod.reshape(DEPTH, 1, 6 * D_MODEL))


def _proj_kernel(x_ref, mod_ref, g1_ref, win_ref, qg_ref, wq_ref, kvg_ref, wkk_ref, wv_ref,
                 cq_ref, sq_ref, ck_ref, sk_ref,
                 u_ref, naq_ref, nak_ref, nav_ref, mq_ref, mk_ref, mv_ref, ckv_ref):
    x = x_ref[...]
    mod = mod_ref[0]
    h = _rms(x, g1_ref[...]) * (1.0 + mod[1:2, :]) + mod[0:1, :]
    p = jnp.dot(h.astype(BF16), win_ref[...], preferred_element_type=F32)
    u_ref[...] = p[:, :OFF_NA]
    for hh in range(NA_H):
        naq_ref[0, hh] = p[:, OFF_NA + hh * NA_D: OFF_NA + (hh + 1) * NA_D].astype(BF16)
        nak_ref[0, hh] = p[:, OFF_NA + NA_H * NA_D + hh * NA_D: OFF_NA + NA_H * NA_D + (hh + 1) * NA_D]
        nav_ref[0, hh] = p[:, OFF_NA + 2 * NA_H * NA_D + hh * NA_D: OFF_NA + 2 * NA_H * NA_D + (hh + 1) * NA_D]
    qn = _rms(p[:, OFF_QA:OFF_KVA], qg_ref[...]).astype(BF16)
    qq = jnp.dot(qn, wq_ref[...], preferred_element_type=F32)
    cq = cq_ref[...]
    sq = sq_ref[...]
    for hh in range(MLA_H):
        qa = qq[:, hh * MLA_PAD:(hh + 1) * MLA_PAD]
        qb = qq[:, (MLA_H + hh) * MLA_PAD:(MLA_H + hh + 1) * MLA_PAD]
        mq_ref[0, hh] = (qa * cq + qb * sq).astype(BF16)
    ckv = _rms(p[:, OFF_KVA:OFF_KVA + MLA_KVL], kvg_ref[...])
    kr = p[:, OFF_KVA + MLA_KVL:D_IN]
    kr_partner = p[:, D_IN:D_IN_EXT]
    kr_rot = kr * ck_ref[...] + kr_partner * sk_ref[...]
    ckv_ref[...] = jnp.concatenate([ckv, kr], axis=-1)
    kin = jnp.concatenate([ckv, kr_rot], axis=-1).astype(BF16)
    kk = jnp.dot(kin, wkk_ref[...], preferred_element_type=F32)
    vv = jnp.dot(kin[:, :MLA_KVL], wv_ref[...], preferred_element_type=F32)
    for hh in range(MLA_H):
        mk_ref[0, hh] = kk[:, hh * MLA_PAD:(hh + 1) * MLA_PAD].astype(BF16)
        mv_ref[0, hh] = vv[:, hh * MLA_V:(hh + 1) * MLA_V].astype(BF16)


def _project(x, mod_l, g1, w_in_ext, qg, wq2, kvg, wkk, wv, cq, sq, ck, sk):
    const2 = lambda i: (0, 0)
    tile2 = lambda i: (i, 0)
    head4 = lambda i: (i, 0, 0, 0)
    rope2 = lambda i: (_rope_block(i), 0)
    return pl.pallas_call(
        _proj_kernel,
        grid=(N_TILES,),
        in_specs=[pl.BlockSpec((TM, D_MODEL), tile2),
                  pl.BlockSpec((1, 6, D_MODEL), lambda i: (_mod_row(i), 0, 0)),
                  pl.BlockSpec((1, D_MODEL), const2),
                  pl.BlockSpec((D_MODEL, D_IN_EXT), const2),
                  pl.BlockSpec((1, MLA_QL), const2),
                  pl.BlockSpec((MLA_QL, 2 * MLA_H * MLA_PAD), const2),
                  pl.BlockSpec((1, MLA_KVL), const2),
                  pl.BlockSpec((MLA_KVL + MLA_ROPE, MLA_H * MLA_PAD), const2),
                  pl.BlockSpec((MLA_KVL, MLA_H * MLA_V), const2),
                  pl.BlockSpec((TM, MLA_PAD), rope2),
                  pl.BlockSpec((TM, MLA_PAD), rope2),
                  pl.BlockSpec((TM, MLA_ROPE), rope2),
                  pl.BlockSpec((TM, MLA_ROPE), rope2)],
        out_specs=[pl.BlockSpec((TM, S5_W), tile2),
                   pl.BlockSpec((1, NA_H, TM, NA_D), head4),
                   pl.BlockSpec((1, NA_H, TM, NA_D), head4),
                   pl.BlockSpec((1, NA_H, TM, NA_D), head4),
                   pl.BlockSpec((1, MLA_H, TM, MLA_PAD), head4),
                   pl.BlockSpec((1, MLA_H, TM, MLA_PAD), head4),
                   pl.BlockSpec((1, MLA_H, TM, MLA_V), head4),
                   pl.BlockSpec((TM, MLA_KVL + MLA_ROPE), tile2)],
        out_shape=[jax.ShapeDtypeStruct((T_ALL, S5_W), F32),
                   jax.ShapeDtypeStruct((N_TILES, NA_H, TM, NA_D), BF16),
                   jax.ShapeDtypeStruct((N_TILES, NA_H, TM, NA_D), F32),
                   jax.ShapeDtypeStruct((N_TILES, NA_H, TM, NA_D), F32),
                   jax.ShapeDtypeStruct((N_TILES, MLA_H, TM, MLA_PAD), BF16),
                   jax.ShapeDtypeStruct((N_TILES, MLA_H, TM, MLA_PAD), BF16),
                   jax.ShapeDtypeStruct((N_TILES, MLA_H, TM, MLA_V), BF16),
                   jax.ShapeDtypeStruct((T_ALL, MLA_KVL + MLA_ROPE), F32)],
        compiler_params=_cparams(("arbitrary",)),
        name="proj",
    )(x, mod_l, g1, w_in_ext, qg, wq2, kvg, wkk, wv, cq, sq, ck, sk)


def _cache_kv_kernel(c_ref, wkk_ref, wv_ref, k_ref, v_ref):
    cin = c_ref[0, 0].astype(BF16)
    kk = jnp.dot(cin, wkk_ref[0], preferred_element_type=F32)
    vv = jnp.dot(cin[:, :MLA_KVL], wv_ref[0], preferred_element_type=F32)
    for hh in range(MLA_H):
        k_ref[0, 0, hh] = kk[:, hh * MLA_PAD:(hh + 1) * MLA_PAD].astype(BF16)
        v_ref[0, 0, hh] = vv[:, hh * MLA_V:(hh + 1) * MLA_V].astype(BF16)


def _cache_kv(cache_mla_kv, wkk_all, wv_all):
    return pl.pallas_call(
        _cache_kv_kernel,
        grid=(N_LAT, DEPTH),
        in_specs=[pl.BlockSpec((1, 1, PAST, MLA_KVL + MLA_ROPE), lambda n, l: (n, l, 0, 0)),
                  pl.BlockSpec((1, MLA_KVL + MLA_ROPE, MLA_H * MLA_PAD), lambda n, l: (l, 0, 0)),
                  pl.BlockSpec((1, MLA_KVL, MLA_H * MLA_V), lambda n, l: (l, 0, 0))],
        out_specs=[pl.BlockSpec((1, 1, MLA_H, PAST, MLA_PAD), lambda n, l: (n, l, 0, 0, 0)),
                   pl.BlockSpec((1, 1, MLA_H, PAST, MLA_V), lambda n, l: (n, l, 0, 0, 0))],
        out_shape=[jax.ShapeDtypeStruct((N_LAT, DEPTH, MLA_H, PAST, MLA_PAD), BF16),
                   jax.ShapeDtypeStruct((N_LAT, DEPTH, MLA_H, PAST, MLA_V), BF16)],
        compiler_params=_cparams(("arbitrary", "arbitrary")),
        name="cache_kv",
    )(cache_mla_kv, wkk_all, wv_all)


def _s5_disc_kernel(lr_ref, li_ref, ldt_ref, bre_ref, bim_ref, a_ref, bb_ref):
    lr = lr_ref[...]
    li = li_ref[...]
    dt = jnp.exp(ldt_ref[...])
    mag = jnp.exp(lr * dt)
    a_re = mag * jnp.cos(li * dt)
    a_im = mag * jnp.sin(li * dt)
    den = lr * lr + li * li
    nr = a_re - 1.0
    k_re = (nr * lr + a_im * li) / den
    k_im = (a_im * lr - nr * li) / den
    a_ref[0] = a_re
    a_ref[1] = a_im
    bre = bre_ref[...]
    bim = bim_ref[...]
    bb_ref[0] = k_re * bre - k_im * bim
    bb_ref[1] = k_re * bim + k_im * bre


def _s5_discretize(lam_re, lam_im, log_dt, b_re, b_im):
    g2 = 2 * S5_G
    a, bb = pl.pallas_call(
        _s5_disc_kernel,
        out_shape=[jax.ShapeDtypeStruct((2, g2, 1, S5_P), F32),
                   jax.ShapeDtypeStruct((2, g2, S5_C, S5_P), F32)],
        name="s5_disc",
    )(lam_re.reshape(g2, 1, S5_P), lam_im.reshape(g2, 1, S5_P), log_dt.reshape(g2, 1, 1),
      b_re.reshape(g2, S5_P, S5_C).transpose(0, 2, 1), b_im.reshape(g2, S5_P, S5_C).transpose(0, 2, 1))
    return a, bb


def _s5_matrices(a, bb, c_re, c_im):
    gp = S5_G * S5_P
    a_lanes = a.reshape(2, 2, S5_G, S5_P).transpose(1, 0, 2, 3).reshape(2, 2, gp)
    bbar = bb.reshape(2, 2, S5_G, S5_C, S5_P)
    eye = jnp.eye(S5_G, dtype=F32)
    wb = jnp.einsum('xdgcp,gh->dgcxhp', bbar, eye).reshape(2, S5_W, 2 * gp)
    cc = jnp.stack([c_re, -c_im], axis=1)
    wc = jnp.einsum('dxgcp,gh->dxhpgc', cc, eye).reshape(2, 2 * gp, S5_W)
    return a_lanes, wb.astype(BF16), wc.astype(BF16)


def _scan_kernel(u_ref, wb_ref, wc_ref, a_ref, h0_ref, y_ref, ht_ref, x_s, h_s, st_s):
    d = pl.program_id(0)
    c = pl.program_id(2)
    nc = pl.num_programs(2)
    gp = S5_G * S5_P

    @pl.when(c == 0)
    def _():
        st_s[...] = h0_ref[0]

    u = u_ref[...].reshape(SCAN_CH * SCAN_ROWS, S5_W).astype(BF16)
    x_s[...] = jnp.dot(u, wb_ref[0], preferred_element_type=F32)
    a_re = a_ref[0, 0:1, :]
    a_im = a_ref[0, 1:2, :]
    unroll = 4

    def outer(io, carry):
        h_re, h_im = carry
        for ii in range(unroll):
            i = io * unroll + ii
            t = jnp.where(d == 0, i, SCAN_CH - 1 - i)
            r = pl.multiple_of(t * SCAN_ROWS, SCAN_ROWS)
            x_re = x_s[pl.ds(r, SCAN_ROWS), pl.ds(0, gp)]
            x_im = x_s[pl.ds(r, SCAN_ROWS), pl.ds(gp, gp)]
            n_re = a_re * h_re - a_im * h_im + x_re
            n_im = a_re * h_im + a_im * h_re + x_im
            h_s[pl.ds(r, SCAN_ROWS), pl.ds(0, gp)] = n_re
            h_s[pl.ds(r, SCAN_ROWS), pl.ds(gp, gp)] = n_im
            h_re, h_im = n_re, n_im
        return h_re, h_im

    h_re, h_im = lax.fori_loop(0, SCAN_CH // unroll, outer, (st_s[:, pl.ds(0, gp)], st_s[:, pl.ds(gp, gp)]))
    st_s[:, pl.ds(0, gp)] = h_re
    st_s[:, pl.ds(gp, gp)] = h_im
    y = jnp.dot(h_s[...].astype(BF16), wc_ref[0], preferred_element_type=F32)
    y_ref[0] = y.reshape(SCAN_CH, SCAN_ROWS, S5_W)

    @pl.when(c == nc - 1)
    def _():
        ht_ref[0] = st_s[...]


def _s5_scan(u_tm, wb, wc, a_lanes, h0):
    L, n, _ = u_tm.shape
    nc = L // SCAN_CH
    ng = n // SCAN_ROWS

    def ceff(d, c):
        return c + d * (nc - 1 - 2 * c)

    return pl.pallas_call(
        _scan_kernel,
        grid=(2, ng, nc),
        in_specs=[pl.BlockSpec((SCAN_CH, SCAN_ROWS, S5_W), lambda d, g, c: (ceff(d, c), g, 0)),
                  pl.BlockSpec((1, S5_W, S5_STATE_W), lambda d, g, c: (d, 0, 0)),
                  pl.BlockSpec((1, S5_STATE_W, S5_W), lambda d, g, c: (d, 0, 0)),
                  pl.BlockSpec((1, 2, S5_G * S5_P), lambda d, g, c: (d, 0, 0)),
                  pl.BlockSpec((1, SCAN_ROWS, S5_STATE_W), lambda d, g, c: (d, g, 0))],
        out_specs=[pl.BlockSpec((1, SCAN_CH, SCAN_ROWS, S5_W), lambda d, g, c: (d, ceff(d, c), g, 0)),
                   pl.BlockSpec((1, SCAN_ROWS, S5_STATE_W), lambda d, g, c: (d, g, 0))],
        out_shape=[jax.ShapeDtypeStruct((2, L, n, S5_W), F32),
                   jax.ShapeDtypeStruct((2, n, S5_STATE_W), F32)],
        scratch_shapes=[pltpu.VMEM((SCAN_CH * SCAN_ROWS, S5_STATE_W), F32),
                        pltpu.VMEM((SCAN_CH * SCAN_ROWS, S5_STATE_W), F32),
                        pltpu.VMEM((SCAN_ROWS, S5_STATE_W), F32)],
        compiler_params=_cparams(("arbitrary", "arbitrary", "arbitrary")),
        name="s5_scan",
    )(u_tm, wb, wc, a_lanes, h0)


def _qk(q, k):
    return lax.dot_general(q, k, (((1,), (1,)), ((), ())), preferred_element_type=F32)


def _softmax_pv(s_list, v_list):
    m = s_list[0].max(axis=-1, keepdims=True)
    for s in s_list[1:]:
        m = jnp.maximum(m, s.max(axis=-1, keepdims=True))
    den = None
    acc = None
    for s, v in zip(s_list, v_list):
        e = jnp.exp(s - m)
        ds = e.sum(axis=-1, keepdims=True)
        pv = jnp.dot(e.astype(BF16), v, preferred_element_type=F32)
        den = ds if den is None else den + ds
        acc = pv if acc is None else acc + pv
    return acc / den


def _ctx_attn_kernel(nq_ref, nk_ref, nv_ref, mq_ref, mk_ref, mv_ref, ckv_ref, fk_in, fv_in, fm_in,
                     na_ref, mla_ref, fk_ref, fv_ref, fm_ref):
    del fk_in, fv_in, fm_in
    fk_ref[0, 0] = nk_ref[0]
    fv_ref[0, 0] = nv_ref[0]
    fm_ref[0, 0] = ckv_ref[...]
    for hh in range(NA_H):
        s = _qk(nq_ref[0, hh], nk_ref[0, hh].astype(BF16)) * NA_SCALE
        na_ref[:, hh * NA_D:(hh + 1) * NA_D] = _softmax_pv([s], [nv_ref[0, hh].astype(BF16)])
    for hh in range(MLA_H):
        s = _qk(mq_ref[0, hh], mk_ref[0, hh]) * MLA_SCALE
        mla_ref[:, hh * MLA_V:(hh + 1) * MLA_V] = _softmax_pv([s], [mv_ref[0, hh]])


def _ctx_attention(naq, nak, nav, mq, mk, mv, ckv, fin_k, fin_v, fin_m, layer):
    head4 = lambda i: (i, 0, 0, 0)
    tile2 = lambda i: (i, 0)
    anyspec = pl.BlockSpec(memory_space=pl.ANY)
    return pl.pallas_call(
        _ctx_attn_kernel,
        grid=(CTX_TILES,),
        in_specs=[pl.BlockSpec((1, NA_H, TM, NA_D), head4)] * 3
                 + [pl.BlockSpec((1, MLA_H, TM, MLA_PAD), head4)] * 2
                 + [pl.BlockSpec((1, MLA_H, TM, MLA_V), head4),
                    pl.BlockSpec((TM, MLA_KVL + MLA_ROPE), tile2), anyspec, anyspec, anyspec],
        out_specs=[pl.BlockSpec((TM, NA_H * NA_D), tile2), pl.BlockSpec((TM, MLA_H * MLA_V), tile2),
                   pl.BlockSpec((1, 1, NA_H, TM, NA_D), lambda i: (i, layer, 0, 0, 0)),
                   pl.BlockSpec((1, 1, NA_H, TM, NA_D), lambda i: (i, layer, 0, 0, 0)),
                   pl.BlockSpec((1, 1, TM, MLA_KVL + MLA_ROPE), lambda i: (i, layer, 0, 0))],
        out_shape=[jax.ShapeDtypeStruct((T_CTX, NA_H * NA_D), F32),
                   jax.ShapeDtypeStruct((T_CTX, MLA_H * MLA_V), F32),
                   jax.ShapeDtypeStruct(fin_k.shape, F32),
                   jax.ShapeDtypeStruct(fin_v.shape, F32),
                   jax.ShapeDtypeStruct(fin_m.shape, F32)],
        input_output_aliases={7: 2, 8: 3, 9: 4},
        compiler_params=_cparams(("arbitrary",)),
        name="ctx_attn",
    )(naq, nak, nav, mq, mk, mv, ckv, fin_k, fin_v, fin_m)


def _na_lat_kernel(q_ref, k0_ref, k1_ref, k2_ref, v0_ref, v1_ref, v2_ref, kc_ref, vc_ref, b_ref, o_ref):
    for hh in range(NA_H):
        q = q_ref[0, hh]
        kb = jnp.concatenate([k0_ref[0, hh], k1_ref[0, hh], k2_ref[0, hh]], axis=0).astype(BF16)
        vb = jnp.concatenate([v0_ref[0, hh], v1_ref[0, hh], v2_ref[0, hh]], axis=0).astype(BF16)
        s_win = _qk(q, kb) * NA_SCALE + b_ref[0, 0, hh]
        s_ctx = _qk(q, kc_ref[0, 0, hh].astype(BF16)) * NA_SCALE
        o_ref[:, hh * NA_D:(hh + 1) * NA_D] = _softmax_pv([s_win, s_ctx], [vb, vc_ref[0, 0, hh].astype(BF16)])


def _band_start(j):
    return jnp.clip(j - 1, 0, LAT_TILES - 3)


def _na_latent(naq, nak, nav, cache_k, cache_v, bias, layer):
    def qmap(n, j):
        return (CTX_TILES + n * LAT_TILES + j, 0, 0, 0)

    def kmap(off):
        return lambda n, j: (CTX_TILES + n * LAT_TILES + _band_start(j) + off, 0, 0, 0)

    def bmap(n, j):
        return (layer, jnp.where(j == 0, 0, jnp.where(j == LAT_TILES - 1, 2, 1)), 0, 0, 0)

    blk = (1, NA_H, TM, NA_D)
    return pl.pallas_call(
        _na_lat_kernel,
        grid=(N_LAT, LAT_TILES),
        in_specs=[pl.BlockSpec(blk, qmap)]
                 + [pl.BlockSpec(blk, kmap(o)) for o in range(3)]
                 + [pl.BlockSpec(blk, kmap(o)) for o in range(3)]
                 + [pl.BlockSpec((1, 1, NA_H, PAST, NA_D), lambda n, j: (n, layer, 0, 0, 0))] * 2
                 + [pl.BlockSpec((1, 1, NA_H, TM, 3 * TM), bmap)],
        out_specs=pl.BlockSpec((TM, NA_H * NA_D), lambda n, j: (n * LAT_TILES + j, 0)),
        out_shape=jax.ShapeDtypeStruct((T_LAT, NA_H * NA_D), F32),
        compiler_params=_cparams(("arbitrary", "arbitrary")),
        name="na_latent",
    )(naq, nak, nak, nak, nav, nav, nav, cache_k, cache_v, bias)


def _na_bias_tables(rpb_all):
    rows = L_LAT // GRID_W
    nr, ncol = 2 * WIN_H - 1, 2 * WIN_W - 1
    w = np.arange(GRID_W)
    cs = np.clip(w - WIN_W // 2, 0, GRID_W - WIN_W)
    cc = np.arange(GRID_W)
    col_ok = (cc[None, :] >= cs[:, None]) & (cc[None, :] < cs[:, None] + WIN_W)
    dc = cc[None, :] - w[:, None] + WIN_W - 1
    col_sel = (dc[:, :, None] == np.arange(ncol)) & col_ok[:, :, None]
    row_sel, oks = [], []
    for j, s in ((0, 0), (1, 0), (LAT_TILES - 1, LAT_TILES - 3)):
        r = 4 * j + np.arange(4)
        rs = np.clip(r - WIN_H // 2, 0, rows - WIN_H)
        krow = 4 * s + np.arange(12)
        row_ok = (krow[None, :] >= rs[:, None]) & (krow[None, :] < rs[:, None] + WIN_H)
        dr = krow[None, :] - r[:, None] + WIN_H - 1
        row_sel.append((dr[:, :, None] == np.arange(nr)) & row_ok[:, :, None])
        oks.append(row_ok[:, None, :, None] & col_ok[None, :, None, :])
    row_sel = jnp.asarray(np.stack(row_sel), F32)
    col_sel = jnp.asarray(col_sel, F32)
    mask = jnp.asarray(np.where(np.stack(oks), 0.0, NEG).reshape(3, TM, 3 * TM), F32)
    b = jnp.einsum('prka,lhab,wcb->lphrwkc', row_sel, rpb_all, col_sel, precision=lax.Precision.HIGHEST)
    return b.reshape(DEPTH, 3, NA_H, TM, 3 * TM) + mask[None, :, None]


def _mla_lat_kernel(q_ref, k_ref, v_ref, kc_ref, vc_ref, o_ref):
    for hh in range(MLA_H):
        q = q_ref[0, hh]
        k = k_ref[:, hh].reshape(L_LAT, MLA_PAD)
        v = v_ref[:, hh].reshape(L_LAT, MLA_V)
        s_lat = _qk(q, k) * MLA_SCALE
        s_ctx = _qk(q, kc_ref[0, 0, hh]) * MLA_SCALE
        o_ref[:, hh * MLA_V:(hh + 1) * MLA_V] = _softmax_pv([s_lat, s_ctx], [v, vc_ref[0, 0, hh]])


def _mla_latent(mq, mk, mv, kc, vc, layer):
    seq_blk = CTX_TILES // LAT_TILES
    return pl.pallas_call(
        _mla_lat_kernel,
        grid=(N_LAT, LAT_TILES),
        in_specs=[pl.BlockSpec((1, MLA_H, TM, MLA_PAD), lambda n, j: (CTX_TILES + n * LAT_TILES + j, 0, 0, 0)),
                  pl.BlockSpec((LAT_TILES, MLA_H, TM, MLA_PAD), lambda n, j: (seq_blk + n, 0, 0, 0)),
                  pl.BlockSpec((LAT_TILES, MLA_H, TM, MLA_V), lambda n, j: (seq_blk + n, 0, 0, 0)),
                  pl.BlockSpec((1, 1, MLA_H, PAST, MLA_PAD), lambda n, j: (n, layer, 0, 0, 0)),
                  pl.BlockSpec((1, 1, MLA_H, PAST, MLA_V), lambda n, j: (n, layer, 0, 0, 0))],
        out_specs=pl.BlockSpec((TM, MLA_H * MLA_V), lambda n, j: (n * LAT_TILES + j, 0)),
        out_shape=jax.ShapeDtypeStruct((T_LAT, MLA_H * MLA_V), F32),
        compiler_params=_cparams(("arbitrary", "arbitrary")),
        name="mla_latent",
    )(mq, mk, mv, kc, vc)


def _post_kernel(y_ref, u_ref, d_ref, gw_ref, gb_ref, nac_ref, nal_ref, mlac_ref, mlal_ref, wo_ref, x_ref, mod_ref,
                 g2_ref, rw_ref, rb_ref, x1_ref, h2_ref, te_ref, tg_ref, rk_ref, cnt_ref):
    ys = y_ref[0] + y_ref[1] + u_ref[...] * d_ref[...]
    g = jax.nn.gelu(ys)
    s5o = g * jax.nn.sigmoid(jnp.dot(g.astype(BF16), gw_ref[...], preferred_element_type=F32) + gb_ref[...])
    is_ctx = pl.program_id(0) < CTX_TILES
    na = jnp.where(is_ctx, nac_ref[...], nal_ref[...])
    mla = jnp.where(is_ctx, mlac_ref[...], mlal_ref[...])
    out = (jnp.dot(s5o.astype(BF16), wo_ref[0:S5_W, :], preferred_element_type=F32)
           + jnp.dot(na.astype(BF16), wo_ref[S5_W:S5_W + NA_H * NA_D, :], preferred_element_type=F32)
           + jnp.dot(mla.astype(BF16), wo_ref[S5_W + NA_H * NA_D:, :], preferred_element_type=F32))
    mod = mod_ref[0]
    x1 = x_ref[...] + mod[2:3, :] * out
    x1_ref[...] = x1
    h2 = _rms(x1, g2_ref[...]) * (1.0 + mod[4:5, :]) + mod[3:4, :]
    h2_ref[...] = h2
    logits = jnp.dot(h2, rw_ref[...], preferred_element_type=F32, precision=lax.Precision.HIGHEST) + rb_ref[...]
    lane_i = lax.broadcasted_iota(jnp.int32, (TM, 128), 1)
    lane = lane_i.astype(F32)
    cur = jnp.where(lane_i < N_EXP, logits, -jnp.inf)
    te = jnp.zeros((TM, 128), F32)
    tv = jnp.zeros((TM, 128), F32)
    hot = jnp.zeros((TM, 128), F32)
    idxs = []
    top = None
    for k in range(TOP_K):
        m = cur.max(axis=-1, keepdims=True)
        idx = jnp.where(cur == m, lane, 128.0).min(axis=-1, keepdims=True)
        sel = lane == idx
        top = m if top is None else top
        idxs.append(idx)
        te = jnp.where(lane_i == k, idx, te)
        tv = jnp.where(lane_i == k, jnp.exp(m - top), tv)
        hot = jnp.where(sel, 1.0, hot)
        cur = jnp.where(sel, -jnp.inf, cur)
    te_ref[...] = te.astype(jnp.int32)
    tg_ref[...] = tv / tv.sum(axis=-1, keepdims=True)
    row = lax.broadcasted_iota(jnp.int32, (TM, TM), 0)
    col = lax.broadcasted_iota(jnp.int32, (TM, TM), 1)
    tri = jnp.where(col < row, 1.0, 0.0).astype(BF16)
    before = jnp.dot(tri, hot.astype(BF16), preferred_element_type=F32)
    rk = jnp.zeros((TM, 128), F32)
    for k in range(TOP_K):
        rk = jnp.where(lane_i == k, jnp.where(lane == idxs[k], before, 0.0).sum(axis=-1, keepdims=True), rk)
    rk_ref[...] = rk.astype(jnp.int32)
    cnt_ref[0] = jnp.broadcast_to(hot.sum(axis=0, keepdims=True), (8, 128)).astype(jnp.int32)


def _post(y2, u, s5_d, glu_w, glu_b, na_ctx, na_lat, mla_ctx, mla_lat, w_out, x, mod_l, g2, rw, rb):
    const2 = lambda i: (0, 0)
    tile2 = lambda i: (i, 0)
    ctx2 = lambda i: (jnp.minimum(i, CTX_TILES - 1), 0)
    lat2 = lambda i: (jnp.maximum(i - CTX_TILES, 0), 0)
    return pl.pallas_call(
        _post_kernel,
        grid=(N_TILES,),
        in_specs=[pl.BlockSpec((2, TM, S5_W), lambda i: (0, i, 0)),
                  pl.BlockSpec((TM, S5_W), tile2),
                  pl.BlockSpec((1, S5_W), const2),
                  pl.BlockSpec((S5_W, S5_W), const2),
                  pl.BlockSpec((1, S5_W), const2),
                  pl.BlockSpec((TM, NA_H * NA_D), ctx2),
                  pl.BlockSpec((TM, NA_H * NA_D), lat2),
                  pl.BlockSpec((TM, MLA_H * MLA_V), ctx2),
                  pl.BlockSpec((TM, MLA_H * MLA_V), lat2),
                  pl.BlockSpec((D_MODEL, D_MODEL), const2),
                  pl.BlockSpec((TM, D_MODEL), tile2),
                  pl.BlockSpec((1, 6, D_MODEL), lambda i: (_mod_row(i), 0, 0)),
                  pl.BlockSpec((1, D_MODEL), const2),
                  pl.BlockSpec((D_MODEL, 128), const2),
                  pl.BlockSpec((1, 128), const2)],
        out_specs=[pl.BlockSpec((TM, D_MODEL), tile2),
                   pl.BlockSpec((TM, D_MODEL), tile2),
                   pl.BlockSpec((TM, 128), tile2),
                   pl.BlockSpec((TM, 128), tile2),
                   pl.BlockSpec((TM, 128), tile2),
                   pl.BlockSpec((1, 8, 128), lambda i: (i, 0, 0))],
        out_shape=[jax.ShapeDtypeStruct((T_ALL, D_MODEL), F32),
                   jax.ShapeDtypeStruct((T_ALL, D_MODEL), F32),
                   jax.ShapeDtypeStruct((T_ALL, 128), jnp.int32),
                   jax.ShapeDtypeStruct((T_ALL, 128), F32),
                   jax.ShapeDtypeStruct((T_ALL, 128), jnp.int32),
                   jax.ShapeDtypeStruct((N_TILES, 8, 128), jnp.int32)],
        compiler_params=_cparams(("arbitrary",)),
        name="post",
    )(y2, u, s5_d, glu_w, glu_b, na_ctx, na_lat, mla_ctx, mla_lat, w_out, x, mod_l, g2, rw, rb)


def _dispatch_kernel(pos_ref, h_ref, xp_in_ref, xp_ref, sem):
    del xp_in_ref
    base = pl.program_id(0) * (TM * TOP_K)

    def issue(t, carry):
        for k in range(TOP_K):
            p = pos_ref[base + t * TOP_K + k]
            pltpu.make_async_copy(h_ref.at[pl.ds(t, 1), :], xp_ref.at[pl.ds(p, 1), :], sem).start()
        return carry

    lax.fori_loop(0, TM, issue, 0)
    for k in range(TOP_K):
        pltpu.make_async_copy(h_ref, xp_ref.at[pl.ds(0, TM), :], sem).wait()


def _dispatch(pos_flat, h2, xp_init):
    return pl.pallas_call(
        _dispatch_kernel,
        grid_spec=pltpu.PrefetchScalarGridSpec(
            num_scalar_prefetch=1, grid=(N_TILES,),
            in_specs=[pl.BlockSpec((TM, D_MODEL), lambda i, pos: (i, 0)),
                      pl.BlockSpec(memory_space=pl.ANY)],
            out_specs=pl.BlockSpec(memory_space=pl.ANY),
            scratch_shapes=[pltpu.SemaphoreType.DMA(())]),
        out_shape=jax.ShapeDtypeStruct((MOE_ROWS, D_MODEL), F32),
        input_output_aliases={2: 0},
        compiler_params=_cparams(("arbitrary",)),
        name="moe_dispatch",
    )(pos_flat, h2, xp_init)


def _moe_kernel(be_ref, first_ref, nused_ref, x_ref, wgu_ref, bgu_ref, wd_ref, bd_ref, o_ref, wgu_s, wd_s):
    b = pl.program_id(0)

    @pl.when(first_ref[b] == 1)
    def _():
        wgu_s[...] = wgu_ref[0, 0].astype(BF16)
        wd_s[...] = wd_ref[0, 0].astype(BF16)

    @pl.when(b < nused_ref[0])
    def _():
        gu = jnp.dot(x_ref[...].astype(BF16), wgu_s[...], preferred_element_type=F32) + bgu_ref[0, 0]
        gate = jnp.minimum(gu[:, :D_FF], SWIGLU_LIMIT)
        up = jnp.clip(gu[:, D_FF:], -SWIGLU_LIMIT, SWIGLU_LIMIT)
        act = (up + 1.0) * (gate * jax.nn.sigmoid(SWIGLU_ALPHA * gate))
        o_ref[...] = jnp.dot(act.astype(BF16), wd_s[...], preferred_element_type=F32) + bd_ref[0, 0]

    @pl.when(b >= nused_ref[0])
    def _():
        o_ref[...] = jnp.zeros_like(o_ref)


def _moe_experts(block_e, first, nused, xp, w_gu, b_gu, w_down, b_down, layer):
    def xmap(b, be, fi, nu):
        return (jnp.minimum(b, nu[0] - 1), 0)

    def wmap(b, be, fi, nu):
        return (layer, be[b], 0, 0)

    return pl.pallas_call(
        _moe_kernel,
        grid_spec=pltpu.PrefetchScalarGridSpec(
            num_scalar_prefetch=3, grid=(MOE_BLOCKS,),
            in_specs=[pl.BlockSpec((MOE_BM, D_MODEL), xmap),
                      pl.BlockSpec((1, 1, D_MODEL, 2 * D_FF), wmap),
                      pl.BlockSpec((1, 1, 1, 2 * D_FF), wmap),
                      pl.BlockSpec((1, 1, D_FF, D_MODEL), wmap),
                      pl.BlockSpec((1, 1, 1, D_MODEL), wmap)],
            out_specs=pl.BlockSpec((MOE_BM, D_MODEL), lambda b, be, fi, nu: (b, 0)),
            scratch_shapes=[pltpu.VMEM((D_MODEL, 2 * D_FF), BF16), pltpu.VMEM((D_FF, D_MODEL), BF16)]),
        out_shape=jax.ShapeDtypeStruct((MOE_ROWS, D_MODEL), F32),
        compiler_params=_cparams(("arbitrary",)),
        name="moe_experts",
    )(block_e, first, nused, xp, w_gu, b_gu.reshape(DEPTH, N_EXP, 1, 2 * D_FF), w_down,
      b_down.reshape(DEPTH, N_EXP, 1, D_MODEL))


def _combine_kernel(pos_ref, yp_ref, tg_ref, x1_ref, mod_ref, o_ref, buf, sem):
    base = pl.program_id(0) * (TM * TOP_K)

    def issue(t, carry):
        for k in range(TOP_K):
            p = pos_ref[base + t * TOP_K + k]
            pltpu.make_async_copy(yp_ref.at[pl.ds(p, 1), :], buf.at[k, pl.ds(t, 1), :], sem).start()
        return carry

    lax.fori_loop(0, TM, issue, 0)
    for k in range(TOP_K):
        pltpu.make_async_copy(yp_ref.at[pl.ds(0, TM), :], buf.at[k], sem).wait()
    tg = tg_ref[...]
    y = buf[0] * tg[:, 0:1]
    for k in range(1, TOP_K):
        y = y + buf[k] * tg[:, k:k + 1]
    o_ref[...] = x1_ref[...] + mod_ref[0][5:6, :] * y


def _combine(pos_flat, yp, tg, x1, mod_l):
    return pl.pallas_call(
        _combine_kernel,
        grid_spec=pltpu.PrefetchScalarGridSpec(
            num_scalar_prefetch=1, grid=(N_TILES,),
            in_specs=[pl.BlockSpec(memory_space=pl.ANY),
                      pl.BlockSpec((TM, 128), lambda i, pos: (i, 0)),
                      pl.BlockSpec((TM, D_MODEL), lambda i, pos: (i, 0)),
                      pl.BlockSpec((1, 6, D_MODEL), lambda i, pos: (_mod_row(i), 0, 0))],
            out_specs=pl.BlockSpec((TM, D_MODEL), lambda i, pos: (i, 0)),
            scratch_shapes=[pltpu.VMEM((TOP_K, TM, D_MODEL), F32), pltpu.SemaphoreType.DMA(())]),
        out_shape=jax.ShapeDtypeStruct((T_ALL, D_MODEL), F32),
        compiler_params=_cparams(("arbitrary",)),
        name="moe_combine",
    )(pos_flat, yp, tg, x1, mod_l)


def _moe_plan(te, rk, cnt):
    tile_cnt = cnt[:, 0, :N_EXP]
    counts = tile_cnt.sum(axis=0)
    padded = (counts + MOE_BM - 1) // MOE_BM * MOE_BM
    pad_end = jnp.cumsum(padded)
    pad_start = pad_end - padded
    tile_off = jnp.cumsum(tile_cnt, axis=0) - tile_cnt
    base = (pad_start[None, :] + tile_off).astype(jnp.int32)
    e = te[:, :TOP_K].reshape(N_TILES, TM, TOP_K)
    hot = e[..., None] == jnp.arange(N_EXP, dtype=jnp.int32)
    pos = jnp.sum(jnp.where(hot, base[:, None, None, :], 0), axis=-1) + rk[:, :TOP_K].reshape(N_TILES, TM, TOP_K)
    nused = (pad_end[-1] // MOE_BM).astype(jnp.int32)
    blk = jnp.arange(MOE_BLOCKS, dtype=jnp.int32)
    be = jnp.minimum(jnp.sum((pad_end[None, :] <= (blk * MOE_BM)[:, None]).astype(jnp.int32), axis=1), N_EXP - 1)
    be = jnp.where(blk < nused, be, be[nused - 1])
    first = jnp.concatenate([jnp.ones((1,), jnp.int32), (be[1:] != be[:-1]).astype(jnp.int32)])
    return pos.reshape(-1).astype(jnp.int32), be, first, nused.reshape(1)


def _final_kernel(x_ref, g_ref, o_ref):
    o_ref[...] = _rms(x_ref[...], g_ref[...])


def _final_norm(x, g):
    return pl.pallas_call(
        _final_kernel,
        grid=(N_TILES,),
        in_specs=[pl.BlockSpec((TM, D_MODEL), lambda i: (i, 0)), pl.BlockSpec((1, D_MODEL), lambda i: (0, 0))],
        out_specs=pl.BlockSpec((TM, D_MODEL), lambda i: (i, 0)),
        out_shape=jax.ShapeDtypeStruct((T_ALL, D_MODEL), F32),
        compiler_params=_cparams(("arbitrary",)),
        name="final_norm",
    )(x, g)


_ROPE_PERM = np.concatenate([np.arange(8, 16), np.arange(0, 8), np.arange(24, 32), np.arange(16, 24)])


def _rope_tables():
    half = MLA_ROPE // 2
    t = jnp.arange(L_LAT)
    row = (t // GRID_W).astype(F32)
    col = (t % GRID_W).astype(F32)
    inv = ROPE_BASE ** (-jnp.arange(0, half, 2, dtype=F32) / half)

    def part(pos):
        ang = pos[:, None] * inv[None, :]
        c, s = jnp.cos(ang), jnp.sin(ang)
        return jnp.concatenate([c, c], axis=-1), jnp.concatenate([-s, s], axis=-1)

    cr, sr = part(row)
    cc, sc = part(col)
    cos32 = jnp.concatenate([cr, cc], axis=-1)
    sin32 = jnp.concatenate([sr, sc], axis=-1)
    ck = jnp.concatenate([jnp.ones((TM, MLA_ROPE), F32), cos32], axis=0)
    sk = jnp.concatenate([jnp.zeros((TM, MLA_ROPE), F32), sin32], axis=0)
    padw = MLA_PAD - MLA_NOPE - MLA_ROPE
    cq = jnp.concatenate([jnp.ones((TM + L_LAT, MLA_NOPE), F32), ck, jnp.ones((TM + L_LAT, padw), F32)], axis=-1)
    sq = jnp.concatenate([jnp.zeros((TM + L_LAT, MLA_NOPE), F32), sk, jnp.zeros((TM + L_LAT, padw), F32)], axis=-1)
    return cq, sq, ck, sk


def _mla_weights(w_qb, w_kvb):
    dq = MLA_NOPE + MLA_ROPE
    wq = w_qb.reshape(MLA_QL, MLA_H, dq)
    zpad = jnp.zeros((MLA_QL, MLA_H, MLA_PAD - dq), F32)
    q_main = jnp.concatenate([wq, zpad], axis=-1)
    q_part = jnp.concatenate([jnp.zeros((MLA_QL, MLA_H, MLA_NOPE), F32), wq[:, :, MLA_NOPE + _ROPE_PERM], zpad], axis=-1)
    wq2 = jnp.concatenate([q_main.reshape(MLA_QL, -1), q_part.reshape(MLA_QL, -1)], axis=-1).astype(BF16)
    wkv = w_kvb.reshape(MLA_KVL, MLA_H, MLA_NOPE + MLA_V)
    k_top = jnp.concatenate([wkv[:, :, :MLA_NOPE], jnp.zeros((MLA_KVL, MLA_H, MLA_PAD - MLA_NOPE), F32)], axis=-1)
    place = jnp.concatenate([jnp.zeros((MLA_ROPE, MLA_NOPE), F32), jnp.eye(MLA_ROPE, dtype=F32),
                             jnp.zeros((MLA_ROPE, MLA_PAD - dq), F32)], axis=-1)
    k_bot = jnp.broadcast_to(place[:, None, :], (MLA_ROPE, MLA_H, MLA_PAD))
    wkk = jnp.concatenate([k_top, k_bot], axis=0).reshape(MLA_KVL + MLA_ROPE, MLA_H * MLA_PAD).astype(BF16)
    wv = wkv[:, :, MLA_NOPE:].reshape(MLA_KVL, MLA_H * MLA_V).astype(BF16)
    return wq2, wkk, wv


def kernel(x_prompt, x_sample, cache_na_k, cache_na_v, cache_mla_kv, state_s5_re, state_s5_im, c, c_ctx, w_mod, b_mod, norm1_g, norm2_g, w_in, w_out, s5_lambda_re, s5_lambda_im, s5_log_dt, s5_b_re, s5_b_im, s5_c_re, s5_c_im, s5_d, s5_glu_w, s5_glu_b, na_rpb, mla_q_norm_g, mla_w_qb, mla_kv_norm_g, mla_w_kvb, router_w, router_b, moe_w_gu, moe_b_gu, moe_w_down, moe_b_down, final_norm_g):
    x = jnp.concatenate([x_prompt.reshape(T_CTX, D_MODEL), x_sample.reshape(T_LAT, D_MODEL)], axis=0)
    cv = jnp.concatenate([c_ctx[None, :], c, jnp.zeros((MOD_ROWS - 1 - N_LAT, D_MODEL), F32)], axis=0)
    mod = _modulation(cv, w_mod, b_mod).reshape(DEPTH, MOD_ROWS, 6, D_MODEL)
    cq, sq, ck, sk = _rope_tables()
    mla_w = [_mla_weights(mla_w_qb[l], mla_w_kvb[l]) for l in range(DEPTH)]
    kc_all, vc_all = _cache_kv(cache_mla_kv, jnp.stack([m[1] for m in mla_w]), jnp.stack([m[2] for m in mla_w]))
    gp = S5_G * S5_P
    na_bias = _na_bias_tables(na_rpb)
    fin_k = jnp.zeros((N_CTX, DEPTH, NA_H, L_CTX, NA_D), F32)
    fin_v = jnp.zeros((N_CTX, DEPTH, NA_H, L_CTX, NA_D), F32)
    fin_m = jnp.zeros((N_CTX, DEPTH, L_CTX, MLA_KVL + MLA_ROPE), F32)

    s5_re_list, s5_im_list = [], []
    for l in range(DEPTH):
        mod_l = mod[l]
        wq2, wkk, wv = mla_w[l]
        w_in_ext = jnp.concatenate([w_in[l], w_in[l][:, OFF_KVA + MLA_KVL + _ROPE_PERM]], axis=-1).astype(BF16)
        u, naq, nak, nav, mq, mk, mv, ckv = _project(
            x, mod_l, norm1_g[l][None], w_in_ext, mla_q_norm_g[l][None], wq2, mla_kv_norm_g[l][None], wkk, wv,
            cq, sq, ck, sk)

        a, bb = _s5_discretize(s5_lambda_re[l], s5_lambda_im[l], s5_log_dt[l], s5_b_re[l], s5_b_im[l])
        a_lanes, wb, wc = _s5_matrices(a, bb, s5_c_re[l], s5_c_im[l])
        u_ctx = u[:T_CTX].reshape(N_CTX, L_CTX, S5_W).transpose(1, 0, 2)
        y_ctx, ht_ctx = _s5_scan(u_ctx, wb, wc, a_lanes, jnp.zeros((2, N_CTX, S5_STATE_W), F32))
        u_lat = u[T_CTX:].reshape(N_LAT, L_LAT, S5_W).transpose(1, 0, 2)
        u_lat = jnp.pad(u_lat, ((0, 0), (0, SCAN_ROWS - N_LAT), (0, 0)))
        h0 = jnp.concatenate([state_s5_re[:, l].reshape(N_LAT, 2, gp), state_s5_im[:, l].reshape(N_LAT, 2, gp)], axis=-1)
        h0 = jnp.pad(h0.transpose(1, 0, 2), ((0, 0), (0, SCAN_ROWS - N_LAT), (0, 0)))
        y_lat, _ = _s5_scan(u_lat, wb, wc, a_lanes, h0)
        y2 = jnp.concatenate([y_ctx.transpose(0, 2, 1, 3).reshape(2, T_CTX, S5_W),
                              y_lat[:, :, :N_LAT].transpose(0, 2, 1, 3).reshape(2, T_LAT, S5_W)], axis=1)
        st = ht_ctx.reshape(2, N_CTX, 2, S5_G, S5_P).transpose(1, 0, 2, 3, 4)
        s5_re_list.append(st[:, :, 0])
        s5_im_list.append(st[:, :, 1])

        na_ctx, mla_ctx, fin_k, fin_v, fin_m = _ctx_attention(naq, nak, nav, mq, mk, mv, ckv, fin_k, fin_v, fin_m, l)
        na_lat = _na_latent(naq, nak, nav, cache_na_k, cache_na_v, na_bias, l)
        mla_lat = _mla_latent(mq, mk, mv, kc_all, vc_all, l)

        rw = jnp.pad(router_w[l], ((0, 0), (0, 128 - N_EXP)))
        rb = jnp.pad(router_b[l], (0, 128 - N_EXP))[None]
        x1, h2, te, tg, rk, cnt = _post(y2, u, s5_d[l][None], s5_glu_w[l].astype(BF16), s5_glu_b[l][None],
                                        na_ctx, na_lat, mla_ctx, mla_lat, w_out[l].astype(BF16), x, mod_l,
                                        norm2_g[l][None], rw, rb)
        pos, be, first, nused = _moe_plan(te, rk, cnt)
        xp = _dispatch(pos, h2, jnp.zeros((MOE_ROWS, D_MODEL), F32))
        yp = _moe_experts(be, first, nused, xp, moe_w_gu, moe_b_gu, moe_w_down, moe_b_down, l)
        x = _combine(pos, yp, tg, x1, mod_l)

    y = _final_norm(x, final_norm_g[None])
    y_prompt = y[:T_CTX].reshape(N_CTX, L_CTX, D_MODEL)
    y_sample = y[T_CTX:].reshape(N_LAT, L_LAT, D_MODEL)
    return (y_prompt, y_sample, fin_k, fin_v, fin_m, jnp.stack(s5_re_list, axis=1), jnp.stack(s5_im_list, axis=1))
```

```python
import functools
import math

import numpy as np
import jax
import jax.numpy as jnp
from jax import lax
from jax.experimental import pallas as pl
from jax.experimental.pallas import tpu as pltpu

F32 = jnp.float32
BF16 = jnp.bfloat16

D_MODEL = 1024
N_CTX, L_CTX = 32, 256
N_LAT, L_LAT = 2, 2048
DEPTH = 4
PAST = 256
GRID_W = 64
S5_W, S5_G, S5_P, S5_C = 256, 16, 64, 16
NA_H, NA_D = 4, 64
WIN_H, WIN_W = 8, 16
MLA_H, MLA_NOPE, MLA_ROPE, MLA_V = 8, 64, 32, 64
MLA_QL, MLA_KVL = 256, 128
MLA_SCALE = (MLA_NOPE + MLA_ROPE) ** -0.5
NA_SCALE = NA_D ** -0.5
OFF_NA = S5_W
OFF_QA = OFF_NA + 3 * NA_H * NA_D
OFF_KVA = OFF_QA + MLA_QL
D_IN = OFF_KVA + MLA_KVL + MLA_ROPE
N_EXP, TOP_K, D_FF = 32, 4, 1024
SWIGLU_ALPHA, SWIGLU_LIMIT = 1.702, 7.0
ROPE_BASE = 10000.0
EPS = 1e-6

TM = 256
T_CTX = N_CTX * L_CTX
T_LAT = N_LAT * L_LAT
T_ALL = T_CTX + T_LAT
CTX_TILES = T_CTX // TM
LAT_TILES = L_LAT // TM
N_TILES = T_ALL // TM
MOD_ROWS = 8
MLA_PAD = 128
D_IN_EXT = D_IN + MLA_ROPE
SCAN_ROWS = 8
SCAN_CH = 64
S5_STATE_W = 2 * S5_G * S5_P
MOE_BM = 256
MOE_SLOTS = T_ALL * TOP_K
MOE_ROWS = MOE_SLOTS + N_EXP * MOE_BM
MOE_BLOCKS = MOE_ROWS // MOE_BM
ASG_ROWS = 57 * 1024
NEG = -1e30
VMEM_LIMIT = 56 * 1024 * 1024


def _cparams(sem):
    return pltpu.CompilerParams(dimension_semantics=sem, vmem_limit_bytes=VMEM_LIMIT)


def _mod_row(i):
    return jnp.maximum(i - (CTX_TILES - LAT_TILES), 0) // LAT_TILES


def _rope_block(i):
    return jnp.where(i < CTX_TILES, 0, 1 + (i - CTX_TILES) % LAT_TILES)


def _rms(x, g):
    return x * lax.rsqrt(jnp.mean(x * x, axis=-1, keepdims=True) + EPS) * g


def _mod_kernel(cv_ref, w_ref, b_ref, o_ref):
    s = jax.nn.silu(cv_ref[...]).astype(BF16)
    o_ref[0] = jnp.dot(s, w_ref[0].astype(BF16), preferred_element_type=F32) + b_ref[0]


def _modulation(cv, w_mod, b_mod):
    tn = 1536
    return pl.pallas_call(
        _mod_kernel,
        grid=(DEPTH, 6 * D_MODEL // tn),
        in_specs=[pl.BlockSpec((MOD_ROWS, D_MODEL), lambda l, j: (0, 0)),
                  pl.BlockSpec((1, D_MODEL, tn), lambda l, j: (l, 0, j)),
                  pl.BlockSpec((1, 1, tn), lambda l, j: (l, 0, j))],
        out_specs=pl.BlockSpec((1, MOD_ROWS, tn), lambda l, j: (l, 0, j)),
        out_shape=jax.ShapeDtypeStruct((DEPTH, MOD_ROWS, 6 * D_MODEL), F32),
        compiler_params=_cparams(("arbitrary", "arbitrary")),
        name="modulation",
    )(cv, w_mod, b_mod.reshape(DEPTH, 1, 6 * D_MODEL))


def _proj_kernel(x_ref, mod_ref, g1_ref, win_ref, qg_ref, wq_ref, kvg_ref, wkk_ref, wv_ref,
                 cq_ref, sq_ref, ck_ref, sk_ref,
                 u_ref, naq_ref, nak_ref, nav_ref, mq_ref, mk_ref, mv_ref, ckv_ref):
    x = x_ref[...]
    mod = mod_ref[0]
    h = _rms(x, g1_ref[...]) * (1.0 + mod[1:2, :]) + mod[0:1, :]
    p = jnp.dot(h.astype(BF16), win_ref[...], preferred_element_type=F32)
    u_ref[...] = p[:, :OFF_NA]
    for hh in range(NA_H):
        naq_ref[0, hh] = p[:, OFF_NA + hh * NA_D: OFF_NA + (hh + 1) * NA_D].astype(BF16)
        nak_ref[0, hh] = p[:, OFF_NA + NA_H * NA_D + hh * NA_D: OFF_NA + NA_H * NA_D + (hh + 1) * NA_D]
        nav_ref[0, hh] = p[:, OFF_NA + 2 * NA_H * NA_D + hh * NA_D: OFF_NA + 2 * NA_H * NA_D + (hh + 1) * NA_D]
    qn = _rms(p[:, OFF_QA:OFF_KVA], qg_ref[...]).astype(BF16)
    qq = jnp.dot(qn, wq_ref[...], preferred_element_type=F32)
    cq = cq_ref[...]
    sq = sq_ref[...]
    for hh in range(MLA_H):
        qa = qq[:, hh * MLA_PAD:(hh + 1) * MLA_PAD]
        qb = qq[:, (MLA_H + hh) * MLA_PAD:(MLA_H + hh + 1) * MLA_PAD]
        mq_ref[0, hh] = (qa * cq + qb * sq).astype(BF16)
    ckv = _rms(p[:, OFF_KVA:OFF_KVA + MLA_KVL], kvg_ref[...])
    kr = p[:, OFF_KVA + MLA_KVL:D_IN]
    kr_partner = p[:, D_IN:D_IN_EXT]
    kr_rot = kr * ck_ref[...] + kr_partner * sk_ref[...]
    ckv_ref[...] = jnp.concatenate([ckv, kr], axis=-1)
    kin = jnp.concatenate([ckv, kr_rot], axis=-1).astype(BF16)
    kk = jnp.dot(kin, wkk_ref[...], preferred_element_type=F32)
    vv = jnp.dot(kin[:, :MLA_KVL], wv_ref[...], preferred_element_type=F32)
    for hh in range(MLA_H):
        mk_ref[0, hh] = kk[:, hh * MLA_PAD:(hh + 1) * MLA_PAD].astype(BF16)
        mv_ref[0, hh] = vv[:, hh * MLA_V:(hh + 1) * MLA_V].astype(BF16)


def _project(x, mod_l, g1, w_in_ext, qg, wq2, kvg, wkk, wv, cq, sq, ck, sk):
    const2 = lambda i: (0, 0)
    tile2 = lambda i: (i, 0)
    head4 = lambda i: (i, 0, 0, 0)
    rope2 = lambda i: (_rope_block(i), 0)
    return pl.pallas_call(
        _proj_kernel,
        grid=(N_TILES,),
        in_specs=[pl.BlockSpec((TM, D_MODEL), tile2),
                  pl.BlockSpec((1, 6, D_MODEL), lambda i: (_mod_row(i), 0, 0)),
                  pl.BlockSpec((1, D_MODEL), const2),
                  pl.BlockSpec((D_MODEL, D_IN_EXT), const2),
                  pl.BlockSpec((1, MLA_QL), const2),
                  pl.BlockSpec((MLA_QL, 2 * MLA_H * MLA_PAD), const2),
                  pl.BlockSpec((1, MLA_KVL), const2),
                  pl.BlockSpec((MLA_KVL + MLA_ROPE, MLA_H * MLA_PAD), const2),
                  pl.BlockSpec((MLA_KVL, MLA_H * MLA_V), const2),
                  pl.BlockSpec((TM, MLA_PAD), rope2),
                  pl.BlockSpec((TM, MLA_PAD), rope2),
                  pl.BlockSpec((TM, MLA_ROPE), rope2),
                  pl.BlockSpec((TM, MLA_ROPE), rope2)],
        out_specs=[pl.BlockSpec((TM, S5_W), tile2),
                   pl.BlockSpec((1, NA_H, TM, NA_D), head4),
                   pl.BlockSpec((1, NA_H, TM, NA_D), head4),
                   pl.BlockSpec((1, NA_H, TM, NA_D), head4),
                   pl.BlockSpec((1, MLA_H, TM, MLA_PAD), head4),
                   pl.BlockSpec((1, MLA_H, TM, MLA_PAD), head4),
                   pl.BlockSpec((1, MLA_H, TM, MLA_V), head4),
                   pl.BlockSpec((TM, MLA_KVL + MLA_ROPE), tile2)],
        out_shape=[jax.ShapeDtypeStruct((T_ALL, S5_W), F32),
                   jax.ShapeDtypeStruct((N_TILES, NA_H, TM, NA_D), BF16),
                   jax.ShapeDtypeStruct((N_TILES, NA_H, TM, NA_D), F32),
                   jax.ShapeDtypeStruct((N_TILES, NA_H, TM, NA_D), F32),
                   jax.ShapeDtypeStruct((N_TILES, MLA_H, TM, MLA_PAD), BF16),
                   jax.ShapeDtypeStruct((N_TILES, MLA_H, TM, MLA_PAD), BF16),
                   jax.ShapeDtypeStruct((N_TILES, MLA_H, TM, MLA_V), BF16),
                   jax.ShapeDtypeStruct((T_ALL, MLA_KVL + MLA_ROPE), F32)],
        compiler_params=_cparams(("arbitrary",)),
        name="proj",
    )(x, mod_l, g1, w_in_ext, qg, wq2, kvg, wkk, wv, cq, sq, ck, sk)


def _cache_kv_kernel(c_ref, wkk_ref, wv_ref, k_ref, v_ref):
    cin = c_ref[0, 0].astype(BF16)
    kk = jnp.dot(cin, wkk_ref[0], preferred_element_type=F32)
    vv = jnp.dot(cin[:, :MLA_KVL], wv_ref[0], preferred_element_type=F32)
    for hh in range(MLA_H):
        k_ref[0, 0, hh] = kk[:, hh * MLA_PAD:(hh + 1) * MLA_PAD].astype(BF16)
        v_ref[0, 0, hh] = vv[:, hh * MLA_V:(hh + 1) * MLA_V].astype(BF16)


def _cache_kv(cache_mla_kv, wkk_all, wv_all):
    return pl.pallas_call(
        _cache_kv_kernel,
        grid=(N_LAT, DEPTH),
        in_specs=[pl.BlockSpec((1, 1, PAST, MLA_KVL + MLA_ROPE), lambda n, l: (n, l, 0, 0)),
                  pl.BlockSpec((1, MLA_KVL + MLA_ROPE, MLA_H * MLA_PAD), lambda n, l: (l, 0, 0)),
                  pl.BlockSpec((1, MLA_KVL, MLA_H * MLA_V), lambda n, l: (l, 0, 0))],
        out_specs=[pl.BlockSpec((1, 1, MLA_H, PAST, MLA_PAD), lambda n, l: (n, l, 0, 0, 0)),
                   pl.BlockSpec((1, 1, MLA_H, PAST, MLA_V), lambda n, l: (n, l, 0, 0, 0))],
        out_shape=[jax.ShapeDtypeStruct((N_LAT, DEPTH, MLA_H, PAST, MLA_PAD), BF16),
                   jax.ShapeDtypeStruct((N_LAT, DEPTH, MLA_H, PAST, MLA_V), BF16)],
        compiler_params=_cparams(("arbitrary", "arbitrary")),
        name="cache_kv",
    )(cache_mla_kv, wkk_all, wv_all)


def _s5_disc_kernel(lr_ref, li_ref, ldt_ref, bre_ref, bim_ref, a_ref, bb_ref):
    lr = lr_ref[...]
    li = li_ref[...]
    dt = jnp.exp(ldt_ref[...])
    mag = jnp.exp(lr * dt)
    a_re = mag * jnp.cos(li * dt)
    a_im = mag * jnp.sin(li * dt)
    den = lr * lr + li * li
    nr = a_re - 1.0
    k_re = (nr * lr + a_im * li) / den
    k_im = (a_im * lr - nr * li) / den
    a_ref[0] = a_re
    a_ref[1] = a_im
    bre = bre_ref[...]
    bim = bim_ref[...]
    bb_ref[0] = k_re * bre - k_im * bim
    bb_ref[1] = k_re * bim + k_im * bre


def _s5_discretize(lam_re, lam_im, log_dt, b_re, b_im):
    g2 = 2 * S5_G
    a, bb = pl.pallas_call(
        _s5_disc_kernel,
        out_shape=[jax.ShapeDtypeStruct((2, g2, 1, S5_P), F32),
                   jax.ShapeDtypeStruct((2, g2, S5_C, S5_P), F32)],
        name="s5_disc",
    )(lam_re.reshape(g2, 1, S5_P), lam_im.reshape(g2, 1, S5_P), log_dt.reshape(g2, 1, 1),
      b_re.reshape(g2, S5_P, S5_C).transpose(0, 2, 1), b_im.reshape(g2, S5_P, S5_C).transpose(0, 2, 1))
    return a, bb


def _s5_matrices(a, bb, c_re, c_im):
    gp = S5_G * S5_P
    a_lanes = a.reshape(2, 2, S5_G, S5_P).transpose(1, 0, 2, 3).reshape(2, 2, gp)
    bbar = bb.reshape(2, 2, S5_G, S5_C, S5_P)
    eye = jnp.eye(S5_G, dtype=F32)
    wb = jnp.einsum('xdgcp,gh->dgcxhp', bbar, eye).reshape(2, S5_W, 2 * gp)
    cc = jnp.stack([c_re, -c_im], axis=1)
    wc = jnp.einsum('dxgcp,gh->dxhpgc', cc, eye).reshape(2, 2 * gp, S5_W)
    return a_lanes, wb.astype(BF16), wc.astype(BF16)


def _scan_kernel(u_ref, wb_ref, wc_ref, a_ref, h0_ref, y_ref, ht_ref, x_s, h_s, st_s):
    d = pl.program_id(0)
    c = pl.program_id(2)
    nc = pl.num_programs(2)
    gp = S5_G * S5_P

    @pl.when(c == 0)
    def _():
        st_s[...] = h0_ref[0]

    u = u_ref[...].reshape(SCAN_CH * SCAN_ROWS, S5_W).astype(BF16)
    x_s[...] = jnp.dot(u, wb_ref[0], preferred_element_type=F32)
    a_re = a_ref[0, 0:1, :]
    a_im = a_ref[0, 1:2, :]
    unroll = 4

    def outer(io, carry):
        h_re, h_im = carry
        for ii in range(unroll):
            i = io * unroll + ii
            t = jnp.where(d == 0, i, SCAN_CH - 1 - i)
            r = pl.multiple_of(t * SCAN_ROWS, SCAN_ROWS)
            x_re = x_s[pl.ds(r, SCAN_ROWS), pl.ds(0, gp)]
            x_im = x_s[pl.ds(r, SCAN_ROWS), pl.ds(gp, gp)]
            n_re = a_re * h_re - a_im * h_im + x_re
            n_im = a_re * h_im + a_im * h_re + x_im
            h_s[pl.ds(r, SCAN_ROWS), pl.ds(0, gp)] = n_re
            h_s[pl.ds(r, SCAN_ROWS), pl.ds(gp, gp)] = n_im
            h_re, h_im = n_re, n_im
        return h_re, h_im

    h_re, h_im = lax.fori_loop(0, SCAN_CH // unroll, outer, (st_s[:, pl.ds(0, gp)], st_s[:, pl.ds(gp, gp)]))
    st_s[:, pl.ds(0, gp)] = h_re
    st_s[:, pl.ds(gp, gp)] = h_im
    y = jnp.dot(h_s[...].astype(BF16), wc_ref[0], preferred_element_type=F32)
    y_ref[0] = y.reshape(SCAN_CH, SCAN_ROWS, S5_W)

    @pl.when(c == nc - 1)
    def _():
        ht_ref[0] = st_s[...]


def _s5_scan(u_tm, wb, wc, a_lanes, h0):
    L, n, _ = u_tm.shape
    nc = L // SCAN_CH
    ng = n // SCAN_ROWS

    def ceff(d, c):
        return c + d * (nc - 1 - 2 * c)

    return pl.pallas_call(
        _scan_kernel,
        grid=(2, ng, nc),
        in_specs=[pl.BlockSpec((SCAN_CH, SCAN_ROWS, S5_W), lambda d, g, c: (ceff(d, c), g, 0)),
                  pl.BlockSpec((1, S5_W, S5_STATE_W), lambda d, g, c: (d, 0, 0)),
                  pl.BlockSpec((1, S5_STATE_W, S5_W), lambda d, g, c: (d, 0, 0)),
                  pl.BlockSpec((1, 2, S5_G * S5_P), lambda d, g, c: (d, 0, 0)),
                  pl.BlockSpec((1, SCAN_ROWS, S5_STATE_W), lambda d, g, c: (d, g, 0))],
        out_specs=[pl.BlockSpec((1, SCAN_CH, SCAN_ROWS, S5_W), lambda d, g, c: (d, ceff(d, c), g, 0)),
                   pl.BlockSpec((1, SCAN_ROWS, S5_STATE_W), lambda d, g, c: (d, g, 0))],
        out_shape=[jax.ShapeDtypeStruct((2, L, n, S5_W), F32),
                   jax.ShapeDtypeStruct((2, n, S5_STATE_W), F32)],
        scratch_shapes=[pltpu.VMEM((SCAN_CH * SCAN_ROWS, S5_STATE_W), F32),
                        pltpu.VMEM((SCAN_CH * SCAN_ROWS, S5_STATE_W), F32),
                        pltpu.VMEM((SCAN_ROWS, S5_STATE_W), F32)],
        compiler_params=_cparams(("arbitrary", "arbitrary", "arbitrary")),
        name="s5_scan",
    )(u_tm, wb, wc, a_lanes, h0)


def _qk(q, k):
    return lax.dot_general(q, k, (((1,), (1,)), ((), ())), preferred_element_type=F32)


def _softmax_pv(s_list, v_list):
    m = s_list[0].max(axis=-1, keepdims=True)
    for s in s_list[1:]:
        m = jnp.maximum(m, s.max(axis=-1, keepdims=True))
    den = None
    acc = None
    for s, v in zip(s_list, v_list):
        e = jnp.exp(s - m)
        ds = e.sum(axis=-1, keepdims=True)
        pv = jnp.dot(e.astype(BF16), v, preferred_element_type=F32)
        den = ds if den is None else den + ds
        acc = pv if acc is None else acc + pv
    return acc / den


def _ctx_attn_kernel(nq_ref, nk_ref, nv_ref, mq_ref, mk_ref, mv_ref, ckv_ref, fk_in, fv_in, fm_in,
                     na_ref, mla_ref, fk_ref, fv_ref, fm_ref):
    del fk_in, fv_in, fm_in
    fk_ref[0, 0] = nk_ref[0]
    fv_ref[0, 0] = nv_ref[0]
    fm_ref[0, 0] = ckv_ref[...]
    for hh in range(NA_H):
        s = _qk(nq_ref[0, hh], nk_ref[0, hh].astype(BF16)) * NA_SCALE
        na_ref[:, hh * NA_D:(hh + 1) * NA_D] = _softmax_pv([s], [nv_ref[0, hh].astype(BF16)])
    for hh in range(MLA_H):
        s = _qk(mq_ref[0, hh], mk_ref[0, hh]) * MLA_SCALE
        mla_ref[:, hh * MLA_V:(hh + 1) * MLA_V] = _softmax_pv([s], [mv_ref[0, hh]])


def _ctx_attention(naq, nak, nav, mq, mk, mv, ckv, fin_k, fin_v, fin_m, layer):
    head4 = lambda i: (i, 0, 0, 0)
    tile2 = lambda i: (i, 0)
    anyspec = pl.BlockSpec(memory_space=pl.ANY)
    return pl.pallas_call(
        _ctx_attn_kernel,
        grid=(CTX_TILES,),
        in_specs=[pl.BlockSpec((1, NA_H, TM, NA_D), head4)] * 3
                 + [pl.BlockSpec((1, MLA_H, TM, MLA_PAD), head4)] * 2
                 + [pl.BlockSpec((1, MLA_H, TM, MLA_V), head4),
                    pl.BlockSpec((TM, MLA_KVL + MLA_ROPE), tile2), anyspec, anyspec, anyspec],
        out_specs=[pl.BlockSpec((TM, NA_H * NA_D), tile2), pl.BlockSpec((TM, MLA_H * MLA_V), tile2),
                   pl.BlockSpec((1, 1, NA_H, TM, NA_D), lambda i: (i, layer, 0, 0, 0)),
                   pl.BlockSpec((1, 1, NA_H, TM, NA_D), lambda i: (i, layer, 0, 0, 0)),
                   pl.BlockSpec((1, 1, TM, MLA_KVL + MLA_ROPE), lambda i: (i, layer, 0, 0))],
        out_shape=[jax.ShapeDtypeStruct((T_CTX, NA_H * NA_D), F32),
                   jax.ShapeDtypeStruct((T_CTX, MLA_H * MLA_V), F32),
                   jax.ShapeDtypeStruct(fin_k.shape, F32),
                   jax.ShapeDtypeStruct(fin_v.shape, F32),
                   jax.ShapeDtypeStruct(fin_m.shape, F32)],
        input_output_aliases={7: 2, 8: 3, 9: 4},
        compiler_params=_cparams(("arbitrary",)),
        name="ctx_attn",
    )(naq, nak, nav, mq, mk, mv, ckv, fin_k, fin_v, fin_m)


def _na_lat_kernel(q_ref, k0_ref, k1_ref, k2_ref, v0_ref, v1_ref, v2_ref, kc_ref, vc_ref, b_ref, o_ref):
    for hh in range(NA_H):
        q = q_ref[0, hh]
        kb = jnp.concatenate([k0_ref[0, hh], k1_ref[0, hh], k2_ref[0, hh]], axis=0).astype(BF16)
        vb = jnp.concatenate([v0_ref[0, hh], v1_ref[0, hh], v2_ref[0, hh]], axis=0).astype(BF16)
        s_win = _qk(q, kb) * NA_SCALE + b_ref[0, 0, hh]
        s_ctx = _qk(q, kc_ref[0, 0, hh].astype(BF16)) * NA_SCALE
        o_ref[:, hh * NA_D:(hh + 1) * NA_D] = _softmax_pv([s_win, s_ctx], [vb, vc_ref[0, 0, hh].astype(BF16)])


def _band_start(j):
    return jnp.clip(j - 1, 0, LAT_TILES - 3)


def _na_latent(naq, nak, nav, cache_k, cache_v, bias, layer):
    def qmap(n, j):
        return (CTX_TILES + n * LAT_TILES + j, 0, 0, 0)

    def kmap(off):
        return lambda n, j: (CTX_TILES + n * LAT_TILES + _band_start(j) + off, 0, 0, 0)

    def bmap(n, j):
        return (layer, jnp.where(j == 0, 0, jnp.where(j == LAT_TILES - 1, 2, 1)), 0, 0, 0)

    blk = (1, NA_H, TM, NA_D)
    return pl.pallas_call(
        _na_lat_kernel,
        grid=(N_LAT, LAT_TILES),
        in_specs=[pl.BlockSpec(blk, qmap)]
                 + [pl.BlockSpec(blk, kmap(o)) for o in range(3)]
                 + [pl.BlockSpec(blk, kmap(o)) for o in range(3)]
                 + [pl.BlockSpec((1, 1, NA_H, PAST, NA_D), lambda n, j: (n, layer, 0, 0, 0))] * 2
                 + [pl.BlockSpec((1, 1, NA_H, TM, 3 * TM), bmap)],
        out_specs=pl.BlockSpec((TM, NA_H * NA_D), lambda n, j: (n * LAT_TILES + j, 0)),
        out_shape=jax.ShapeDtypeStruct((T_LAT, NA_H * NA_D), F32),
        compiler_params=_cparams(("arbitrary", "arbitrary")),
        name="na_latent",
    )(naq, nak, nak, nak, nav, nav, nav, cache_k, cache_v, bias)


def _na_bias_tables(rpb_all):
    rows = L_LAT // GRID_W
    nr, ncol = 2 * WIN_H - 1, 2 * WIN_W - 1
    w = np.arange(GRID_W)
    cs = np.clip(w - WIN_W // 2, 0, GRID_W - WIN_W)
    cc = np.arange(GRID_W)
    col_ok = (cc[None, :] >= cs[:, None]) & (cc[None, :] < cs[:, None] + WIN_W)
    dc = cc[None, :] - w[:, None] + WIN_W - 1
    col_sel = (dc[:, :, None] == np.arange(ncol)) & col_ok[:, :, None]
    row_sel, oks = [], []
    for j, s in ((0, 0), (1, 0), (LAT_TILES - 1, LAT_TILES - 3)):
        r = 4 * j + np.arange(4)
        rs = np.clip(r - WIN_H // 2, 0, rows - WIN_H)
        krow = 4 * s + np.arange(12)
        row_ok = (krow[None, :] >= rs[:, None]) & (krow[None, :] < rs[:, None] + WIN_H)
        dr = krow[None, :] - r[:, None] + WIN_H - 1
        row_sel.append((dr[:, :, None] == np.arange(nr)) & row_ok[:, :, None])
        oks.append(row_ok[:, None, :, None] & col_ok[None, :, None, :])
    row_sel = jnp.asarray(np.stack(row_sel), F32)
    col_sel = jnp.asarray(col_sel, F32)
    mask = jnp.asarray(np.where(np.stack(oks), 0.0, NEG).reshape(3, TM, 3 * TM), F32)
    b = jnp.einsum('prka,lhab,wcb->lphrwkc', row_sel, rpb_all, col_sel, precision=lax.Precision.HIGHEST)
    return b.reshape(DEPTH, 3, NA_H, TM, 3 * TM) + mask[None, :, None]


def _mla_lat_kernel(q_ref, k_ref, v_ref, kc_ref, vc_ref, o_ref):
    for hh in range(MLA_H):
        q = q_ref[0, hh]
        k = k_ref[:, hh].reshape(L_LAT, MLA_PAD)
        v = v_ref[:, hh].reshape(L_LAT, MLA_V)
        s_lat = _qk(q, k) * MLA_SCALE
        s_ctx = _qk(q, kc_ref[0, 0, hh]) * MLA_SCALE
        o_ref[:, hh * MLA_V:(hh + 1) * MLA_V] = _softmax_pv([s_lat, s_ctx], [v, vc_ref[0, 0, hh]])


def _mla_latent(mq, mk, mv, kc, vc, layer):
    seq_blk = CTX_TILES // LAT_TILES
    return pl.pallas_call(
        _mla_lat_kernel,
        grid=(N_LAT, LAT_TILES),
        in_specs=[pl.BlockSpec((1, MLA_H, TM, MLA_PAD), lambda n, j: (CTX_TILES + n * LAT_TILES + j, 0, 0, 0)),
                  pl.BlockSpec((LAT_TILES, MLA_H, TM, MLA_PAD), lambda n, j: (seq_blk + n, 0, 0, 0)),
                  pl.BlockSpec((LAT_TILES, MLA_H, TM, MLA_V), lambda n, j: (seq_blk + n, 0, 0, 0)),
                  pl.BlockSpec((1, 1, MLA_H, PAST, MLA_PAD), lambda n, j: (n, layer, 0, 0, 0)),
                  pl.BlockSpec((1, 1, MLA_H, PAST, MLA_V), lambda n, j: (n, layer, 0, 0, 0))],
        out_specs=pl.BlockSpec((TM, MLA_H * MLA_V), lambda n, j: (n * LAT_TILES + j, 0)),
        out_shape=jax.ShapeDtypeStruct((T_LAT, MLA_H * MLA_V), F32),
        compiler_params=_cparams(("arbitrary", "arbitrary")),
        name="mla_latent",
    )(mq, mk, mv, kc, vc)


def _post_kernel(y_ref, u_ref, d_ref, gw_ref, gb_ref, nac_ref, nal_ref, mlac_ref, mlal_ref, wo_ref, x_ref, mod_ref,
                 g2_ref, rw_ref, rb_ref, x1_ref, h2_ref, te_ref, tg_ref, rk_ref, cnt_ref):
    ys = y_ref[0] + y_ref[1] + u_ref[...] * d_ref[...]
    g = jax.nn.gelu(ys)
    s5o = g * jax.nn.sigmoid(jnp.dot(g.astype(BF16), gw_ref[...], preferred_element_type=F32) + gb_ref[...])
    is_ctx = pl.program_id(0) < CTX_TILES
    na = jnp.where(is_ctx, nac_ref[...], nal_ref[...])
    mla = jnp.where(is_ctx, mlac_ref[...], mlal_ref[...])
    out = (jnp.dot(s5o.astype(BF16), wo_ref[0:S5_W, :], preferred_element_type=F32)
           + jnp.dot(na.astype(BF16), wo_ref[S5_W:S5_W + NA_H * NA_D, :], preferred_element_type=F32)
           + jnp.dot(mla.astype(BF16), wo_ref[S5_W + NA_H * NA_D:, :], preferred_element_type=F32))
    mod = mod_ref[0]
    x1 = x_ref[...] + mod[2:3, :] * out
    x1_ref[...] = x1
    h2 = _rms(x1, g2_ref[...]) * (1.0 + mod[4:5, :]) + mod[3:4, :]
    h2_ref[...] = h2
    logits = jnp.dot(h2, rw_ref[...], preferred_element_type=F32, precision=lax.Precision.HIGHEST) + rb_ref[...]
    lane_i = lax.broadcasted_iota(jnp.int32, (TM, 128), 1)
    lane = lane_i.astype(F32)
    cur = jnp.where(lane_i < N_EXP, logits, -jnp.inf)
    te = jnp.zeros((TM, 128), F32)
    tv = jnp.zeros((TM, 128), F32)
    hot = jnp.zeros((TM, 128), F32)
    idxs = []
    top = None
    for k in range(TOP_K):
        m = cur.max(axis=-1, keepdims=True)
        idx = jnp.where(cur == m, lane, 128.0).min(axis=-1, keepdims=True)
        sel = lane == idx
        top = m if top is None else top
        idxs.append(idx)
        te = jnp.where(lane_i == k, idx, te)
        tv = jnp.where(lane_i == k, jnp.exp(m - top), tv)
        hot = jnp.where(sel, 1.0, hot)
        cur = jnp.where(sel, -jnp.inf, cur)
    te_ref[...] = te.astype(jnp.int32)
    tg_ref[...] = tv / tv.sum(axis=-1, keepdims=True)
    row = lax.broadcasted_iota(jnp.int32, (TM, TM), 0)
    col = lax.broadcasted_iota(jnp.int32, (TM, TM), 1)
    tri = jnp.where(col < row, 1.0, 0.0).astype(BF16)
    before = jnp.dot(tri, hot.astype(BF16), preferred_element_type=F32)
    rk = jnp.zeros((TM, 128), F32)
    for k in range(TOP_K):
        rk = jnp.where(lane_i == k, jnp.where(lane == idxs[k], before, 0.0).sum(axis=-1, keepdims=True), rk)
    rk_ref[...] = rk.astype(jnp.int32)
    cnt_ref[0] = jnp.broadcast_to(hot.sum(axis=0, keepdims=True), (8, 128)).astype(jnp.int32)


def _post(y2, u, s5_d, glu_w, glu_b, na_ctx, na_lat, mla_ctx, mla_lat, w_out, x, mod_l, g2, rw, rb):
    const2 = lambda i: (0, 0)
    tile2 = lambda i: (i, 0)
    ctx2 = lambda i: (jnp.minimum(i, CTX_TILES - 1), 0)
    lat2 = lambda i: (jnp.maximum(i - CTX_TILES, 0), 0)
    return pl.pallas_call(
        _post_kernel,
        grid=(N_TILES,),
        in_specs=[pl.BlockSpec((2, TM, S5_W), lambda i: (0, i, 0)),
                  pl.BlockSpec((TM, S5_W), tile2),
                  pl.BlockSpec((1, S5_W), const2),
                  pl.BlockSpec((S5_W, S5_W), const2),
                  pl.BlockSpec((1, S5_W), const2),
                  pl.BlockSpec((TM, NA_H * NA_D), ctx2),
                  pl.BlockSpec((TM, NA_H * NA_D), lat2),
                  pl.BlockSpec((TM, MLA_H * MLA_V), ctx2),
                  pl.BlockSpec((TM, MLA_H * MLA_V), lat2),
                  pl.BlockSpec((D_MODEL, D_MODEL), const2),
                  pl.BlockSpec((TM, D_MODEL), tile2),
                  pl.BlockSpec((1, 6, D_MODEL), lambda i: (_mod_row(i), 0, 0)),
                  pl.BlockSpec((1, D_MODEL), const2),
                  pl.BlockSpec((D_MODEL, 128), const2),
                  pl.BlockSpec((1, 128), const2)],
        out_specs=[pl.BlockSpec((TM, D_MODEL), tile2),
                   pl.BlockSpec((TM, D_MODEL), tile2),
                   pl.BlockSpec((TM, 128), tile2),
                   pl.BlockSpec((TM, 128), tile2),
                   pl.BlockSpec((TM, 128), tile2),
                   pl.BlockSpec((1, 8, 128), lambda i: (i, 0, 0))],
        out_shape=[jax.ShapeDtypeStruct((T_ALL, D_MODEL), F32),
                   jax.ShapeDtypeStruct((T_ALL, D_MODEL), F32),
                   jax.ShapeDtypeStruct((T_ALL, 128), jnp.int32),
                   jax.ShapeDtypeStruct((T_ALL, 128), F32),
                   jax.ShapeDtypeStruct((T_ALL, 128), jnp.int32),
                   jax.ShapeDtypeStruct((N_TILES, 8, 128), jnp.int32)],
        compiler_params=_cparams(("arbitrary",)),
        name="post",
    )(y2, u, s5_d, glu_w, glu_b, na_ctx, na_lat, mla_ctx, mla_lat, w_out, x, mod_l, g2, rw, rb)


def _moe_fused_kernel(be_ref, first_ref, nused_ref, pos_ref,
                      dflt_hbm, h_hbm, wgu_ref, bgu_ref, wd_ref, bd_ref, ys_hbm,
                      asg_s, xs0, xs1, yo0, yo1, wgu_s, wd_s, act_s, gsem, ssem, tsem):
    b = pl.program_id(0)
    nused = nused_ref[0]
    xs = (xs0, xs1)
    yo = (yo0, yo1)

    def wait_gather(p):
        pltpu.make_async_copy(h_hbm.at[pl.ds(0, MOE_BM), :], xs[p], gsem.at[p]).wait()

    def wait_scatter(p):
        pltpu.make_async_copy(yo[p], ys_hbm.at[pl.ds(0, MOE_BM), :], ssem.at[p]).wait()

    def issue_gather(blk, p):
        r0 = (blk + 1) * MOE_BM
        for j in range(MOE_BM):
            tok = asg_s[r0 + j] >> 16
            pltpu.make_async_copy(h_hbm.at[pl.ds(tok, 1), :], xs[p].at[pl.ds(j, 1), :], gsem.at[p]).start()

    def issue_scatter(blk, p, lo=0, hi=MOE_BM):
        r0 = (blk + 1) * MOE_BM
        for j in range(lo, hi):
            d = asg_s[r0 + j] & 0xFFFF
            pltpu.make_async_copy(yo[p].at[pl.ds(j, 1), :], ys_hbm.at[pl.ds(d, 1), :], ssem.at[p]).start()

    @pl.when(b == 0)
    def _():
        cp = pltpu.make_async_copy(dflt_hbm, asg_s, tsem)
        cp.start()
        cp.wait()

        def inv(i, carry):
            a0 = i * 8
            v0 = a0 | ((((a0 >> 10) << 8) + (a0 & (TM - 1))) << 16)
            for uu in range(8):
                asg_s[pos_ref[a0 + uu]] = v0 + uu * ((1 << 16) + 1)
            return carry
        lax.fori_loop(0, MOE_SLOTS // 8, inv, 0)
        yo1[...] = jnp.zeros_like(yo1)
        zc = pltpu.make_async_copy(yo1, ys_hbm.at[pl.ds(MOE_SLOTS, MOE_BM), :], tsem)
        zc.start()
        zc.wait()
        issue_gather(0, 0)

    def step(p):
        q = 1 - p
        wait_gather(p)
        if p == 0:
            @pl.when(b >= 1)
            def _():
                wait_scatter(p)
        else:
            wait_scatter(p)

        @pl.when(first_ref[b] == 1)
        def _():
            wgu_s[...] = wgu_ref[0, 0].astype(BF16)
            wd_s[...] = wd_ref[0, 0].astype(BF16)

        issue_gather(jnp.minimum(b + 1, nused - 1), q)
        issue_scatter(b - 1, q, 0, MOE_BM // 2)
        gu = jnp.dot(xs[p][...].astype(BF16), wgu_s[...], preferred_element_type=F32) + bgu_ref[0, 0]
        gate = jnp.minimum(gu[:, :D_FF], SWIGLU_LIMIT)
        up = jnp.clip(gu[:, D_FF:], -SWIGLU_LIMIT, SWIGLU_LIMIT)
        act_s[...] = ((up + 1.0) * (gate * jax.nn.sigmoid(SWIGLU_ALPHA * gate))).astype(BF16)

        @pl.when(b >= 0)
        def _():
            issue_scatter(b - 1, q, MOE_BM // 2, MOE_BM)
            yo[p][...] = jnp.dot(act_s[...], wd_s[...], preferred_element_type=F32) + bd_ref[0, 0]

    def drain(p):
        q = 1 - p
        wait_gather(p)
        wait_scatter(p)
        issue_scatter(nused - 1, q)
        wait_scatter(q)

    for p in range(2):
        @pl.when((b < nused) & (b % 2 == p))
        def _():
            step(p)

        @pl.when((b == nused) & (b % 2 == p))
        def _():
            drain(p)


def _moe_fused(block_e, first, nused, pos_km, h2, w_gu, b_gu, w_down, b_down, layer):
    def wmap(b, be, fi, nu, pos):
        return (layer, be[b], 0, 0)

    r = jnp.arange(ASG_ROWS, dtype=jnp.int32)
    dflt = MOE_SLOTS + (((r >> 8) + 1) & 1) * MOE_BM + (r & (MOE_BM - 1))

    return pl.pallas_call(
        _moe_fused_kernel,
        grid_spec=pltpu.PrefetchScalarGridSpec(
            num_scalar_prefetch=4, grid=(MOE_BLOCKS + 1,),
            in_specs=[pl.BlockSpec(memory_space=pl.ANY), pl.BlockSpec(memory_space=pl.ANY),
                      pl.BlockSpec((1, 1, D_MODEL, 2 * D_FF), wmap),
                      pl.BlockSpec((1, 1, 1, 2 * D_FF), wmap),
                      pl.BlockSpec((1, 1, D_FF, D_MODEL), wmap),
                      pl.BlockSpec((1, 1, 1, D_MODEL), wmap)],
            out_specs=pl.BlockSpec(memory_space=pl.ANY),
            scratch_shapes=[pltpu.SMEM((ASG_ROWS,), jnp.int32),
                            pltpu.VMEM((MOE_BM, D_MODEL), F32), pltpu.VMEM((MOE_BM, D_MODEL), F32),
                            pltpu.VMEM((MOE_BM, D_MODEL), F32), pltpu.VMEM((MOE_BM, D_MODEL), F32),
                            pltpu.VMEM((D_MODEL, 2 * D_FF), BF16), pltpu.VMEM((D_FF, D_MODEL), BF16),
                            pltpu.VMEM((MOE_BM, D_FF), BF16),
                            pltpu.SemaphoreType.DMA((2,)), pltpu.SemaphoreType.DMA((2,)),
                            pltpu.SemaphoreType.DMA(())]),
        out_shape=jax.ShapeDtypeStruct((MOE_SLOTS + 2 * MOE_BM, D_MODEL), F32),
        compiler_params=_cparams(("arbitrary",)),
        name="moe_fused",
    )(block_e, first, nused, pos_km, dflt, h2, w_gu, b_gu.reshape(DEPTH, N_EXP, 1, 2 * D_FF), w_down,
      b_down.reshape(DEPTH, N_EXP, 1, D_MODEL))


def _combine_kernel(ys_ref, tg_ref, x1_ref, mod_ref, o_ref):
    tg = tg_ref[...]
    y = ys_ref[0:TM, :] * tg[:, 0:1]
    for k in range(1, TOP_K):
        y = y + ys_ref[k * TM:(k + 1) * TM, :] * tg[:, k:k + 1]
    o_ref[...] = x1_ref[...] + mod_ref[0][5:6, :] * y


def _combine(ys, tg, x1, mod_l):
    return pl.pallas_call(
        _combine_kernel,
        grid=(N_TILES,),
        in_specs=[pl.BlockSpec((TOP_K * TM, D_MODEL), lambda i: (i, 0)),
                  pl.BlockSpec((TM, 128), lambda i: (i, 0)),
                  pl.BlockSpec((TM, D_MODEL), lambda i: (i, 0)),
                  pl.BlockSpec((1, 6, D_MODEL), lambda i: (_mod_row(i), 0, 0))],
        out_specs=pl.BlockSpec((TM, D_MODEL), lambda i: (i, 0)),
        out_shape=jax.ShapeDtypeStruct((T_ALL, D_MODEL), F32),
        compiler_params=_cparams(("arbitrary",)),
        name="moe_combine",
    )(ys, tg, x1, mod_l)


def _moe_plan(te, rk, cnt):
    tile_cnt = cnt[:, 0, :N_EXP]
    counts = tile_cnt.sum(axis=0)
    padded = (counts + MOE_BM - 1) // MOE_BM * MOE_BM
    pad_end = jnp.cumsum(padded)
    pad_start = pad_end - padded
    tile_off = jnp.cumsum(tile_cnt, axis=0) - tile_cnt
    base = (pad_start[None, :] + tile_off).astype(jnp.int32)
    e = te[:, :TOP_K].reshape(N_TILES, TM, TOP_K)
    hot = e[..., None] == jnp.arange(N_EXP, dtype=jnp.int32)
    pos = jnp.sum(jnp.where(hot, base[:, None, None, :], 0), axis=-1) + rk[:, :TOP_K].reshape(N_TILES, TM, TOP_K)
    pos_km = (pos.transpose(0, 2, 1).reshape(-1) + MOE_BM).astype(jnp.int32)
    nused = (pad_end[-1] // MOE_BM).astype(jnp.int32)
    blk = jnp.arange(MOE_BLOCKS + 1, dtype=jnp.int32)
    be = jnp.minimum(jnp.sum((pad_end[None, :] <= (blk * MOE_BM)[:, None]).astype(jnp.int32), axis=1), N_EXP - 1)
    be = jnp.where(blk < nused, be, be[nused - 1])
    first = jnp.concatenate([jnp.ones((1,), jnp.int32), (be[1:] != be[:-1]).astype(jnp.int32)])
    return pos_km, be, first, nused.reshape(1)


def _final_kernel(x_ref, g_ref, o_ref):
    o_ref[...] = _rms(x_ref[...], g_ref[...])


def _final_norm(x, g):
    return pl.pallas_call(
        _final_kernel,
        grid=(N_TILES,),
        in_specs=[pl.BlockSpec((TM, D_MODEL), lambda i: (i, 0)), pl.BlockSpec((1, D_MODEL), lambda i: (0, 0))],
        out_specs=pl.BlockSpec((TM, D_MODEL), lambda i: (i, 0)),
        out_shape=jax.ShapeDtypeStruct((T_ALL, D_MODEL), F32),
        compiler_params=_cparams(("arbitrary",)),
        name="final_norm",
    )(x, g)


_ROPE_PERM = np.concatenate([np.arange(8, 16), np.arange(0, 8), np.arange(24, 32), np.arange(16, 24)])


def _rope_tables():
    half = MLA_ROPE // 2
    t = jnp.arange(L_LAT)
    row = (t // GRID_W).astype(F32)
    col = (t % GRID_W).astype(F32)
    inv = ROPE_BASE ** (-jnp.arange(0, half, 2, dtype=F32) / half)

    def part(pos):
        ang = pos[:, None] * inv[None, :]
        c, s = jnp.cos(ang), jnp.sin(ang)
        return jnp.concatenate([c, c], axis=-1), jnp.concatenate([-s, s], axis=-1)

    cr, sr = part(row)
    cc, sc = part(col)
    cos32 = jnp.concatenate([cr, cc], axis=-1)
    sin32 = jnp.concatenate([sr, sc], axis=-1)
    ck = jnp.concatenate([jnp.ones((TM, MLA_ROPE), F32), cos32], axis=0)
    sk = jnp.concatenate([jnp.zeros((TM, MLA_ROPE), F32), sin32], axis=0)
    padw = MLA_PAD - MLA_NOPE - MLA_ROPE
    cq = jnp.concatenate([jnp.ones((TM + L_LAT, MLA_NOPE), F32), ck, jnp.ones((TM + L_LAT, padw), F32)], axis=-1)
    sq = jnp.concatenate([jnp.zeros((TM + L_LAT, MLA_NOPE), F32), sk, jnp.zeros((TM + L_LAT, padw), F32)], axis=-1)
    return cq, sq, ck, sk


def _mla_weights(w_qb, w_kvb):
    dq = MLA_NOPE + MLA_ROPE
    wq = w_qb.reshape(MLA_QL, MLA_H, dq)
    zpad = jnp.zeros((MLA_QL, MLA_H, MLA_PAD - dq), F32)
    q_main = jnp.concatenate([wq, zpad], axis=-1)
    q_part = jnp.concatenate([jnp.zeros((MLA_QL, MLA_H, MLA_NOPE), F32), wq[:, :, MLA_NOPE + _ROPE_PERM], zpad], axis=-1)
    wq2 = jnp.concatenate([q_main.reshape(MLA_QL, -1), q_part.reshape(MLA_QL, -1)], axis=-1).astype(BF16)
    wkv = w_kvb.reshape(MLA_KVL, MLA_H, MLA_NOPE + MLA_V)
    k_top = jnp.concatenate([wkv[:, :, :MLA_NOPE], jnp.zeros((MLA_KVL, MLA_H, MLA_PAD - MLA_NOPE), F32)], axis=-1)
    place = jnp.concatenate([jnp.zeros((MLA_ROPE, MLA_NOPE), F32), jnp.eye(MLA_ROPE, dtype=F32),
                             jnp.zeros((MLA_ROPE, MLA_PAD - dq), F32)], axis=-1)
    k_bot = jnp.broadcast_to(place[:, None, :], (MLA_ROPE, MLA_H, MLA_PAD))
    wkk = jnp.concatenate([k_top, k_bot], axis=0).reshape(MLA_KVL + MLA_ROPE, MLA_H * MLA_PAD).astype(BF16)
    wv = wkv[:, :, MLA_NOPE:].reshape(MLA_KVL, MLA_H * MLA_V).astype(BF16)
    return wq2, wkk, wv


def kernel(x_prompt, x_sample, cache_na_k, cache_na_v, cache_mla_kv, state_s5_re, state_s5_im, c, c_ctx, w_mod, b_mod, norm1_g, norm2_g, w_in, w_out, s5_lambda_re, s5_lambda_im, s5_log_dt, s5_b_re, s5_b_im, s5_c_re, s5_c_im, s5_d, s5_glu_w, s5_glu_b, na_rpb, mla_q_norm_g, mla_w_qb, mla_kv_norm_g, mla_w_kvb, router_w, router_b, moe_w_gu, moe_b_gu, moe_w_down, moe_b_down, final_norm_g):
    x = jnp.concatenate([x_prompt.reshape(T_CTX, D_MODEL), x_sample.reshape(T_LAT, D_MODEL)], axis=0)
    cv = jnp.concatenate([c_ctx[None, :], c, jnp.zeros((MOD_ROWS - 1 - N_LAT, D_MODEL), F32)], axis=0)
    mod = _modulation(cv, w_mod, b_mod).reshape(DEPTH, MOD_ROWS, 6, D_MODEL)
    cq, sq, ck, sk = _rope_tables()
    mla_w = [_mla_weights(mla_w_qb[l], mla_w_kvb[l]) for l in range(DEPTH)]
    kc_all, vc_all = _cache_kv(cache_mla_kv, jnp.stack([m[1] for m in mla_w]), jnp.stack([m[2] for m in mla_w]))
    gp = S5_G * S5_P
    na_bias = _na_bias_tables(na_rpb)
    fin_k = jnp.zeros((N_CTX, DEPTH, NA_H, L_CTX, NA_D), F32)
    fin_v = jnp.zeros((N_CTX, DEPTH, NA_H, L_CTX, NA_D), F32)
    fin_m = jnp.zeros((N_CTX, DEPTH, L_CTX, MLA_KVL + MLA_ROPE), F32)

    s5_re_list, s5_im_list = [], []
    for l in range(DEPTH):
        mod_l = mod[l]
        wq2, wkk, wv = mla_w[l]
        w_in_ext = jnp.concatenate([w_in[l], w_in[l][:, OFF_KVA + MLA_KVL + _ROPE_PERM]], axis=-1).astype(BF16)
        u, naq, nak, nav, mq, mk, mv, ckv = _project(
            x, mod_l, norm1_g[l][None], w_in_ext, mla_q_norm_g[l][None], wq2, mla_kv_norm_g[l][None], wkk, wv,
            cq, sq, ck, sk)

        a, bb = _s5_discretize(s5_lambda_re[l], s5_lambda_im[l], s5_log_dt[l], s5_b_re[l], s5_b_im[l])
        a_lanes, wb, wc = _s5_matrices(a, bb, s5_c_re[l], s5_c_im[l])
        u_ctx = u[:T_CTX].reshape(N_CTX, L_CTX, S5_W).transpose(1, 0, 2)
        y_ctx, ht_ctx = _s5_scan(u_ctx, wb, wc, a_lanes, jnp.zeros((2, N_CTX, S5_STATE_W), F32))
        u_lat = u[T_CTX:].reshape(N_LAT, L_LAT, S5_W).transpose(1, 0, 2)
        u_lat = jnp.pad(u_lat, ((0, 0), (0, SCAN_ROWS - N_LAT), (0, 0)))
        h0 = jnp.concatenate([state_s5_re[:, l].reshape(N_LAT, 2, gp), state_s5_im[:, l].reshape(N_LAT, 2, gp)], axis=-1)
        h0 = jnp.pad(h0.transpose(1, 0, 2), ((0, 0), (0, SCAN_ROWS - N_LAT), (0, 0)))
        y_lat, _ = _s5_scan(u_lat, wb, wc, a_lanes, h0)
        y2 = jnp.concatenate([y_ctx.transpose(0, 2, 1, 3).reshape(2, T_CTX, S5_W),
                              y_lat[:, :, :N_LAT].transpose(0, 2, 1, 3).reshape(2, T_LAT, S5_W)], axis=1)
        st = ht_ctx.reshape(2, N_CTX, 2, S5_G, S5_P).transpose(1, 0, 2, 3, 4)
        s5_re_list.append(st[:, :, 0])
        s5_im_list.append(st[:, :, 1])

        na_ctx, mla_ctx, fin_k, fin_v, fin_m = _ctx_attention(naq, nak, nav, mq, mk, mv, ckv, fin_k, fin_v, fin_m, l)
        na_lat = _na_latent(naq, nak, nav, cache_na_k, cache_na_v, na_bias, l)
        mla_lat = _mla_latent(mq, mk, mv, kc_all, vc_all, l)

        rw = jnp.pad(router_w[l], ((0, 0), (0, 128 - N_EXP)))
        rb = jnp.pad(router_b[l], (0, 128 - N_EXP))[None]
        x1, h2, te, tg, rk, cnt = _post(y2, u, s5_d[l][None], s5_glu_w[l].astype(BF16), s5_glu_b[l][None],
                                        na_ctx, na_lat, mla_ctx, mla_lat, w_out[l].astype(BF16), x, mod_l,
                                        norm2_g[l][None], rw, rb)
        pos_km, be, first, nused = _moe_plan(te, rk, cnt)
        ys = _moe_fused(be, first, nused, pos_km, h2, moe_w_gu, moe_b_gu, moe_w_down, moe_b_down, l)
        x = _combine(ys, tg, x1, mod_l)

    y = _final_norm(x, final_norm_g[None])
    y_prompt = y[:T_CTX].reshape(N_CTX, L_CTX, D_MODEL)
    y_sample = y[T_CTX:].reshape(N_LAT, L_LAT, D_MODEL)
    return (y_prompt, y_sample, fin_k, fin_v, fin_m, jnp.stack(s5_re_list, axis=1), jnp.stack(s5_im_list, axis=1))
```

```python
import functools
import math

import numpy as np
import jax
import jax.numpy as jnp
from jax import lax
from jax.experimental import pallas as pl
from jax.experimental.pallas import tpu as pltpu

F32 = jnp.float32
BF16 = jnp.bfloat16

D_MODEL = 1024
N_CTX, L_CTX = 32, 256
N_LAT, L_LAT = 2, 2048
DEPTH = 4
PAST = 256
GRID_W = 64
S5_W, S5_G, S5_P, S5_C = 256, 16, 64, 16
NA_H, NA_D = 4, 64
WIN_H, WIN_W = 8, 16
MLA_H, MLA_NOPE, MLA_ROPE, MLA_V = 8, 64, 32, 64
MLA_QL, MLA_KVL = 256, 128
MLA_SCALE = (MLA_NOPE + MLA_ROPE) ** -0.5
NA_SCALE = NA_D ** -0.5
OFF_NA = S5_W
OFF_QA = OFF_NA + 3 * NA_H * NA_D
OFF_KVA = OFF_QA + MLA_QL
D_IN = OFF_KVA + MLA_KVL + MLA_ROPE
N_EXP, TOP_K, D_FF = 32, 4, 1024
SWIGLU_ALPHA, SWIGLU_LIMIT = 1.702, 7.0
ROPE_BASE = 10000.0
EPS = 1e-6

TM = 256
T_CTX = N_CTX * L_CTX
T_LAT = N_LAT * L_LAT
T_ALL = T_CTX + T_LAT
CTX_TILES = T_CTX // TM
LAT_TILES = L_LAT // TM
N_TILES = T_ALL // TM
MOD_ROWS = 8
MLA_PAD = 128
D_IN_EXT = D_IN + MLA_ROPE
SCAN_ROWS = 8
SCAN_CH = 64
S5_STATE_W = 2 * S5_G * S5_P
MOE_BM = 256
MOE_SLOTS = T_ALL * TOP_K
MOE_ROWS = MOE_SLOTS + N_EXP * MOE_BM
MOE_BLOCKS = MOE_ROWS // MOE_BM
ASG_ROWS = 57 * 1024
NEG = -1e30
VMEM_LIMIT = 56 * 1024 * 1024


def _cparams(sem):
    return pltpu.CompilerParams(dimension_semantics=sem, vmem_limit_bytes=VMEM_LIMIT)


def _mod_row(i):
    return jnp.maximum(i - (CTX_TILES - LAT_TILES), 0) // LAT_TILES


def _rope_block(i):
    return jnp.where(i < CTX_TILES, 0, 1 + (i - CTX_TILES) % LAT_TILES)


def _rms(x, g):
    return x * lax.rsqrt(jnp.mean(x * x, axis=-1, keepdims=True) + EPS) * g


def _mod_kernel(cv_ref, w_ref, b_ref, o_ref):
    s = jax.nn.silu(cv_ref[...]).astype(BF16)
    o_ref[0] = jnp.dot(s, w_ref[0].astype(BF16), preferred_element_type=F32) + b_ref[0]


def _modulation(cv, w_mod, b_mod):
    tn = 1536
    return pl.pallas_call(
        _mod_kernel,
        grid=(DEPTH, 6 * D_MODEL // tn),
        in_specs=[pl.BlockSpec((MOD_ROWS, D_MODEL), lambda l, j: (0, 0)),
                  pl.BlockSpec((1, D_MODEL, tn), lambda l, j: (l, 0, j)),
                  pl.BlockSpec((1, 1, tn), lambda l, j: (l, 0, j))],
        out_specs=pl.BlockSpec((1, MOD_ROWS, tn), lambda l, j: (l, 0, j)),
        out_shape=jax.ShapeDtypeStruct((DEPTH, MOD_ROWS, 6 * D_MODEL), F32),
        compiler_params=_cparams(("arbitrary", "arbitrary")),
        name="modulation",
    )(cv, w_mod, b_mod.reshape(DEPTH, 1, 6 * D_MODEL))


def _proj_kernel(x_ref, mod_ref, g1_ref, win_ref, qg_ref, wq_ref, kvg_ref, wkk_ref, wv_ref,
                 cq_ref, sq_ref, ck_ref, sk_ref,
                 u_ref, naq_ref, nak_ref, nav_ref, mq_ref, mk_ref, mv_ref, ckv_ref):
    x = x_ref[...]
    mod = mod_ref[0]
    h = _rms(x, g1_ref[...]) * (1.0 + mod[1:2, :]) + mod[0:1, :]
    p = jnp.dot(h.astype(BF16), win_ref[...], preferred_element_type=F32)
    u_ref[...] = p[:, :OFF_NA]
    for hh in range(NA_H):
        naq_ref[0, hh] = p[:, OFF_NA + hh * NA_D: OFF_NA + (hh + 1) * NA_D].astype(BF16)
        nak_ref[0, hh] = p[:, OFF_NA + NA_H * NA_D + hh * NA_D: OFF_NA + NA_H * NA_D + (hh + 1) * NA_D]
        nav_ref[0, hh] = p[:, OFF_NA + 2 * NA_H * NA_D + hh * NA_D: OFF_NA + 2 * NA_H * NA_D + (hh + 1) * NA_D]
    qn = _rms(p[:, OFF_QA:OFF_KVA], qg_ref[...]).astype(BF16)
    qq = jnp.dot(qn, wq_ref[...], preferred_element_type=F32)
    cq = cq_ref[...]
    sq = sq_ref[...]
    for hh in range(MLA_H):
        qa = qq[:, hh * MLA_PAD:(hh + 1) * MLA_PAD]
        qb = qq[:, (MLA_H + hh) * MLA_PAD:(MLA_H + hh + 1) * MLA_PAD]
        mq_ref[0, hh] = (qa * cq + qb * sq).astype(BF16)
    ckv = _rms(p[:, OFF_KVA:OFF_KVA + MLA_KVL], kvg_ref[...])
    kr = p[:, OFF_KVA + MLA_KVL:D_IN]
    kr_partner = p[:, D_IN:D_IN_EXT]
    kr_rot = kr * ck_ref[...] + kr_partner * sk_ref[...]
    ckv_ref[...] = jnp.concatenate([ckv, kr], axis=-1)
    kin = jnp.concatenate([ckv, kr_rot], axis=-1).astype(BF16)
    kk = jnp.dot(kin, wkk_ref[...], preferred_element_type=F32)
    vv = jnp.dot(kin[:, :MLA_KVL], wv_ref[...], preferred_element_type=F32)
    for hh in range(MLA_H):
        mk_ref[0, hh] = kk[:, hh * MLA_PAD:(hh + 1) * MLA_PAD].astype(BF16)
        mv_ref[0, hh] = vv[:, hh * MLA_V:(hh + 1) * MLA_V].astype(BF16)


def _project(x, mod_l, g1, w_in_ext, qg, wq2, kvg, wkk, wv, cq, sq, ck, sk):
    const2 = lambda i: (0, 0)
    tile2 = lambda i: (i, 0)
    head4 = lambda i: (i, 0, 0, 0)
    rope2 = lambda i: (_rope_block(i), 0)
    return pl.pallas_call(
        _proj_kernel,
        grid=(N_TILES,),
        in_specs=[pl.BlockSpec((TM, D_MODEL), tile2),
                  pl.BlockSpec((1, 6, D_MODEL), lambda i: (_mod_row(i), 0, 0)),
                  pl.BlockSpec((1, D_MODEL), const2),
                  pl.BlockSpec((D_MODEL, D_IN_EXT), const2),
                  pl.BlockSpec((1, MLA_QL), const2),
                  pl.BlockSpec((MLA_QL, 2 * MLA_H * MLA_PAD), const2),
                  pl.BlockSpec((1, MLA_KVL), const2),
                  pl.BlockSpec((MLA_KVL + MLA_ROPE, MLA_H * MLA_PAD), const2),
                  pl.BlockSpec((MLA_KVL, MLA_H * MLA_V), const2),
                  pl.BlockSpec((TM, MLA_PAD), rope2),
                  pl.BlockSpec((TM, MLA_PAD), rope2),
                  pl.BlockSpec((TM, MLA_ROPE), rope2),
                  pl.BlockSpec((TM, MLA_ROPE), rope2)],
        out_specs=[pl.BlockSpec((TM, S5_W), tile2),
                   pl.BlockSpec((1, NA_H, TM, NA_D), head4),
                   pl.BlockSpec((1, NA_H, TM, NA_D), head4),
                   pl.BlockSpec((1, NA_H, TM, NA_D), head4),
                   pl.BlockSpec((1, MLA_H, TM, MLA_PAD), head4),
                   pl.BlockSpec((1, MLA_H, TM, MLA_PAD), head4),
                   pl.BlockSpec((1, MLA_H, TM, MLA_V), head4),
                   pl.BlockSpec((TM, MLA_KVL + MLA_ROPE), tile2)],
        out_shape=[jax.ShapeDtypeStruct((T_ALL, S5_W), F32),
                   jax.ShapeDtypeStruct((N_TILES, NA_H, TM, NA_D), BF16),
                   jax.ShapeDtypeStruct((N_TILES, NA_H, TM, NA_D), F32),
                   jax.ShapeDtypeStruct((N_TILES, NA_H, TM, NA_D), F32),
                   jax.ShapeDtypeStruct((N_TILES, MLA_H, TM, MLA_PAD), BF16),
                   jax.ShapeDtypeStruct((N_TILES, MLA_H, TM, MLA_PAD), BF16),
                   jax.ShapeDtypeStruct((N_TILES, MLA_H, TM, MLA_V), BF16),
                   jax.ShapeDtypeStruct((T_ALL, MLA_KVL + MLA_ROPE), F32)],
        compiler_params=_cparams(("arbitrary",)),
        name="proj",
    )(x, mod_l, g1, w_in_ext, qg, wq2, kvg, wkk, wv, cq, sq, ck, sk)


def _cache_kv_kernel(c_ref, wkk_ref, wv_ref, k_ref, v_ref):
    cin = c_ref[0, 0].astype(BF16)
    kk = jnp.dot(cin, wkk_ref[0], preferred_element_type=F32)
    vv = jnp.dot(cin[:, :MLA_KVL], wv_ref[0], preferred_element_type=F32)
    for hh in range(MLA_H):
        k_ref[0, 0, hh] = kk[:, hh * MLA_PAD:(hh + 1) * MLA_PAD].astype(BF16)
        v_ref[0, 0, hh] = vv[:, hh * MLA_V:(hh + 1) * MLA_V].astype(BF16)


def _cache_kv(cache_mla_kv, wkk_all, wv_all):
    return pl.pallas_call(
        _cache_kv_kernel,
        grid=(N_LAT, DEPTH),
        in_specs=[pl.BlockSpec((1, 1, PAST, MLA_KVL + MLA_ROPE), lambda n, l: (n, l, 0, 0)),
                  pl.BlockSpec((1, MLA_KVL + MLA_ROPE, MLA_H * MLA_PAD), lambda n, l: (l, 0, 0)),
                  pl.BlockSpec((1, MLA_KVL, MLA_H * MLA_V), lambda n, l: (l, 0, 0))],
        out_specs=[pl.BlockSpec((1, 1, MLA_H, PAST, MLA_PAD), lambda n, l: (n, l, 0, 0, 0)),
                   pl.BlockSpec((1, 1, MLA_H, PAST, MLA_V), lambda n, l: (n, l, 0, 0, 0))],
        out_shape=[jax.ShapeDtypeStruct((N_LAT, DEPTH, MLA_H, PAST, MLA_PAD), BF16),
                   jax.ShapeDtypeStruct((N_LAT, DEPTH, MLA_H, PAST, MLA_V), BF16)],
        compiler_params=_cparams(("arbitrary", "arbitrary")),
        name="cache_kv",
    )(cache_mla_kv, wkk_all, wv_all)


def _s5_disc_kernel(lr_ref, li_ref, ldt_ref, bre_ref, bim_ref, a_ref, bb_ref):
    lr = lr_ref[...]
    li = li_ref[...]
    dt = jnp.exp(ldt_ref[...])
    mag = jnp.exp(lr * dt)
    a_re = mag * jnp.cos(li * dt)
    a_im = mag * jnp.sin(li * dt)
    den = lr * lr + li * li
    nr = a_re - 1.0
    k_re = (nr * lr + a_im * li) / den
    k_im = (a_im * lr - nr * li) / den
    a_ref[0] = a_re
    a_ref[1] = a_im
    bre = bre_ref[...]
    bim = bim_ref[...]
    bb_ref[0] = k_re * bre - k_im * bim
    bb_ref[1] = k_re * bim + k_im * bre


def _s5_discretize(lam_re, lam_im, log_dt, b_re, b_im):
    g2 = 2 * S5_G
    a, bb = pl.pallas_call(
        _s5_disc_kernel,
        out_shape=[jax.ShapeDtypeStruct((2, g2, 1, S5_P), F32),
                   jax.ShapeDtypeStruct((2, g2, S5_C, S5_P), F32)],
        name="s5_disc",
    )(lam_re.reshape(g2, 1, S5_P), lam_im.reshape(g2, 1, S5_P), log_dt.reshape(g2, 1, 1),
      b_re.reshape(g2, S5_P, S5_C).transpose(0, 2, 1), b_im.reshape(g2, S5_P, S5_C).transpose(0, 2, 1))
    return a, bb


def _s5_matrices(a, bb, c_re, c_im):
    gp = S5_G * S5_P
    a_lanes = a.reshape(2, 2, S5_G, S5_P).transpose(1, 0, 2, 3).reshape(2, 2, gp)
    bbar = bb.reshape(2, 2, S5_G, S5_C, S5_P)
    eye = jnp.eye(S5_G, dtype=F32)
    wb = jnp.einsum('xdgcp,gh->dgcxhp', bbar, eye).reshape(2, S5_W, 2 * gp)
    cc = jnp.stack([c_re, -c_im], axis=1)
    wc = jnp.einsum('dxgcp,gh->dxhpgc', cc, eye).reshape(2, 2 * gp, S5_W)
    return a_lanes, wb.astype(BF16), wc.astype(BF16)


def _scan_kernel(u_ref, wb_ref, wc_ref, a_ref, h0_ref, y_ref, ht_ref, x_s, h_s, st_s):
    d = pl.program_id(0)
    c = pl.program_id(2)
    nc = pl.num_programs(2)
    gp = S5_G * S5_P

    @pl.when(c == 0)
    def _():
        st_s[...] = h0_ref[0]

    u = u_ref[...].reshape(SCAN_CH * SCAN_ROWS, S5_W).astype(BF16)
    x_s[...] = jnp.dot(u, wb_ref[0], preferred_element_type=F32)
    a_re = a_ref[0, 0:1, :]
    a_im = a_ref[0, 1:2, :]
    unroll = 4

    def outer(io, carry):
        h_re, h_im = carry
        for ii in range(unroll):
            i = io * unroll + ii
            t = jnp.where(d == 0, i, SCAN_CH - 1 - i)
            r = pl.multiple_of(t * SCAN_ROWS, SCAN_ROWS)
            x_re = x_s[pl.ds(r, SCAN_ROWS), pl.ds(0, gp)]
            x_im = x_s[pl.ds(r, SCAN_ROWS), pl.ds(gp, gp)]
            n_re = a_re * h_re - a_im * h_im + x_re
            n_im = a_re * h_im + a_im * h_re + x_im
            h_s[pl.ds(r, SCAN_ROWS), pl.ds(0, gp)] = n_re
            h_s[pl.ds(r, SCAN_ROWS), pl.ds(gp, gp)] = n_im
            h_re, h_im = n_re, n_im
        return h_re, h_im

    h_re, h_im = lax.fori_loop(0, SCAN_CH // unroll, outer, (st_s[:, pl.ds(0, gp)], st_s[:, pl.ds(gp, gp)]))
    st_s[:, pl.ds(0, gp)] = h_re
    st_s[:, pl.ds(gp, gp)] = h_im
    y = jnp.dot(h_s[...].astype(BF16), wc_ref[0], preferred_element_type=F32)
    y_ref[0] = y.reshape(SCAN_CH, SCAN_ROWS, S5_W)

    @pl.when(c == nc - 1)
    def _():
        ht_ref[0] = st_s[...]


def _s5_scan(u_tm, wb, wc, a_lanes, h0):
    L, n, _ = u_tm.shape
    nc = L // SCAN_CH
    ng = n // SCAN_ROWS

    def ceff(d, c):
        return c + d * (nc - 1 - 2 * c)

    return pl.pallas_call(
        _scan_kernel,
        grid=(2, ng, nc),
        in_specs=[pl.BlockSpec((SCAN_CH, SCAN_ROWS, S5_W), lambda d, g, c: (ceff(d, c), g, 0)),
                  pl.BlockSpec((1, S5_W, S5_STATE_W), lambda d, g, c: (d, 0, 0)),
                  pl.BlockSpec((1, S5_STATE_W, S5_W), lambda d, g, c: (d, 0, 0)),
                  pl.BlockSpec((1, 2, S5_G * S5_P), lambda d, g, c: (d, 0, 0)),
                  pl.BlockSpec((1, SCAN_ROWS, S5_STATE_W), lambda d, g, c: (d, g, 0))],
        out_specs=[pl.BlockSpec((1, SCAN_CH, SCAN_ROWS, S5_W), lambda d, g, c: (d, ceff(d, c), g, 0)),
                   pl.BlockSpec((1, SCAN_ROWS, S5_STATE_W), lambda d, g, c: (d, g, 0))],
        out_shape=[jax.ShapeDtypeStruct((2, L, n, S5_W), F32),
                   jax.ShapeDtypeStruct((2, n, S5_STATE_W), F32)],
        scratch_shapes=[pltpu.VMEM((SCAN_CH * SCAN_ROWS, S5_STATE_W), F32),
                        pltpu.VMEM((SCAN_CH * SCAN_ROWS, S5_STATE_W), F32),
                        pltpu.VMEM((SCAN_ROWS, S5_STATE_W), F32)],
        compiler_params=_cparams(("arbitrary", "arbitrary", "arbitrary")),
        name="s5_scan",
    )(u_tm, wb, wc, a_lanes, h0)


def _qk(q, k):
    return lax.dot_general(q, k, (((1,), (1,)), ((), ())), preferred_element_type=F32)


def _softmax_pv(s_list, v_list):
    m = s_list[0].max(axis=-1, keepdims=True)
    for s in s_list[1:]:
        m = jnp.maximum(m, s.max(axis=-1, keepdims=True))
    den = None
    acc = None
    for s, v in zip(s_list, v_list):
        e = jnp.exp(s - m)
        ds = e.sum(axis=-1, keepdims=True)
        pv = jnp.dot(e.astype(BF16), v, preferred_element_type=F32)
        den = ds if den is None else den + ds
        acc = pv if acc is None else acc + pv
    return acc / den


def _ctx_attn_kernel(nq_ref, nk_ref, nv_ref, mq_ref, mk_ref, mv_ref, ckv_ref, fk_in, fv_in, fm_in,
                     na_ref, mla_ref, fk_ref, fv_ref, fm_ref):
    del fk_in, fv_in, fm_in
    fk_ref[0, 0] = nk_ref[0]
    fv_ref[0, 0] = nv_ref[0]
    fm_ref[0, 0] = ckv_ref[...]
    for hh in range(NA_H):
        s = _qk(nq_ref[0, hh], nk_ref[0, hh].astype(BF16)) * NA_SCALE
        na_ref[:, hh * NA_D:(hh + 1) * NA_D] = _softmax_pv([s], [nv_ref[0, hh].astype(BF16)])
    for hh in range(MLA_H):
        s = _qk(mq_ref[0, hh], mk_ref[0, hh]) * MLA_SCALE
        mla_ref[:, hh * MLA_V:(hh + 1) * MLA_V] = _softmax_pv([s], [mv_ref[0, hh]])


def _ctx_attention(naq, nak, nav, mq, mk, mv, ckv, fin_k, fin_v, fin_m, layer):
    head4 = lambda i: (i, 0, 0, 0)
    tile2 = lambda i: (i, 0)
    anyspec = pl.BlockSpec(memory_space=pl.ANY)
    return pl.pallas_call(
        _ctx_attn_kernel,
        grid=(CTX_TILES,),
        in_specs=[pl.BlockSpec((1, NA_H, TM, NA_D), head4)] * 3
                 + [pl.BlockSpec((1, MLA_H, TM, MLA_PAD), head4)] * 2
                 + [pl.BlockSpec((1, MLA_H, TM, MLA_V), head4),
                    pl.BlockSpec((TM, MLA_KVL + MLA_ROPE), tile2), anyspec, anyspec, anyspec],
        out_specs=[pl.BlockSpec((TM, NA_H * NA_D), tile2), pl.BlockSpec((TM, MLA_H * MLA_V), tile2),
                   pl.BlockSpec((1, 1, NA_H, TM, NA_D), lambda i: (i, layer, 0, 0, 0)),
                   pl.BlockSpec((1, 1, NA_H, TM, NA_D), lambda i: (i, layer, 0, 0, 0)),
                   pl.BlockSpec((1, 1, TM, MLA_KVL + MLA_ROPE), lambda i: (i, layer, 0, 0))],
        out_shape=[jax.ShapeDtypeStruct((T_CTX, NA_H * NA_D), F32),
                   jax.ShapeDtypeStruct((T_CTX, MLA_H * MLA_V), F32),
                   jax.ShapeDtypeStruct(fin_k.shape, F32),
                   jax.ShapeDtypeStruct(fin_v.shape, F32),
                   jax.ShapeDtypeStruct(fin_m.shape, F32)],
        input_output_aliases={7: 2, 8: 3, 9: 4},
        compiler_params=_cparams(("arbitrary",)),
        name="ctx_attn",
    )(naq, nak, nav, mq, mk, mv, ckv, fin_k, fin_v, fin_m)


def _na_lat_kernel(q_ref, k0_ref, k1_ref, k2_ref, v0_ref, v1_ref, v2_ref, kc_ref, vc_ref, b_ref, o_ref):
    for hh in range(NA_H):
        q = q_ref[0, hh]
        kb = jnp.concatenate([k0_ref[0, hh], k1_ref[0, hh], k2_ref[0, hh]], axis=0).astype(BF16)
        vb = jnp.concatenate([v0_ref[0, hh], v1_ref[0, hh], v2_ref[0, hh]], axis=0).astype(BF16)
        s_win = _qk(q, kb) * NA_SCALE + b_ref[0, 0, hh]
        s_ctx = _qk(q, kc_ref[0, 0, hh].astype(BF16)) * NA_SCALE
        o_ref[:, hh * NA_D:(hh + 1) * NA_D] = _softmax_pv([s_win, s_ctx], [vb, vc_ref[0, 0, hh].astype(BF16)])


def _band_start(j):
    return jnp.clip(j - 1, 0, LAT_TILES - 3)


def _na_latent(naq, nak, nav, cache_k, cache_v, bias, layer):
    def qmap(n, j):
        return (CTX_TILES + n * LAT_TILES + j, 0, 0, 0)

    def kmap(off):
        return lambda n, j: (CTX_TILES + n * LAT_TILES + _band_start(j) + off, 0, 0, 0)

    def bmap(n, j):
        return (layer, jnp.where(j == 0, 0, jnp.where(j == LAT_TILES - 1, 2, 1)), 0, 0, 0)

    blk = (1, NA_H, TM, NA_D)
    return pl.pallas_call(
        _na_lat_kernel,
        grid=(N_LAT, LAT_TILES),
        in_specs=[pl.BlockSpec(blk, qmap)]
                 + [pl.BlockSpec(blk, kmap(o)) for o in range(3)]
                 + [pl.BlockSpec(blk, kmap(o)) for o in range(3)]
                 + [pl.BlockSpec((1, 1, NA_H, PAST, NA_D), lambda n, j: (n, layer, 0, 0, 0))] * 2
                 + [pl.BlockSpec((1, 1, NA_H, TM, 3 * TM), bmap)],
        out_specs=pl.BlockSpec((TM, NA_H * NA_D), lambda n, j: (n * LAT_TILES + j, 0)),
        out_shape=jax.ShapeDtypeStruct((T_LAT, NA_H * NA_D), F32),
        compiler_params=_cparams(("arbitrary", "arbitrary")),
        name="na_latent",
    )(naq, nak, nak, nak, nav, nav, nav, cache_k, cache_v, bias)


def _na_bias_tables(rpb_all):
    rows = L_LAT // GRID_W
    nr, ncol = 2 * WIN_H - 1, 2 * WIN_W - 1
    w = np.arange(GRID_W)
    cs = np.clip(w - WIN_W // 2, 0, GRID_W - WIN_W)
    cc = np.arange(GRID_W)
    col_ok = (cc[None, :] >= cs[:, None]) & (cc[None, :] < cs[:, None] + WIN_W)
    dc = cc[None, :] - w[:, None] + WIN_W - 1
    col_sel = (dc[:, :, None] == np.arange(ncol)) & col_ok[:, :, None]
    row_sel, oks = [], []
    for j, s in ((0, 0), (1, 0), (LAT_TILES - 1, LAT_TILES - 3)):
        r = 4 * j + np.arange(4)
        rs = np.clip(r - WIN_H // 2, 0, rows - WIN_H)
        krow = 4 * s + np.arange(12)
        row_ok = (krow[None, :] >= rs[:, None]) & (krow[None, :] < rs[:, None] + WIN_H)
        dr = krow[None, :] - r[:, None] + WIN_H - 1
        row_sel.append((dr[:, :, None] == np.arange(nr)) & row_ok[:, :, None])
        oks.append(row_ok[:, None, :, None] & col_ok[None, :, None, :])
    row_sel = jnp.asarray(np.stack(row_sel), F32)
    col_sel = jnp.asarray(col_sel, F32)
    mask = jnp.asarray(np.where(np.stack(oks), 0.0, NEG).reshape(3, TM, 3 * TM), F32)
    b = jnp.einsum('prka,lhab,wcb->lphrwkc', row_sel, rpb_all, col_sel, precision=lax.Precision.HIGHEST)
    return b.reshape(DEPTH, 3, NA_H, TM, 3 * TM) + mask[None, :, None]


def _mla_lat_kernel(q_ref, k_ref, v_ref, kc_ref, vc_ref, o_ref):
    for hh in range(MLA_H):
        q = q_ref[0, hh]
        k = k_ref[:, hh].reshape(L_LAT, MLA_PAD)
        v = v_ref[:, hh].reshape(L_LAT, MLA_V)
        s_lat = _qk(q, k) * MLA_SCALE
        s_ctx = _qk(q, kc_ref[0, 0, hh]) * MLA_SCALE
        o_ref[:, hh * MLA_V:(hh + 1) * MLA_V] = _softmax_pv([s_lat, s_ctx], [v, vc_ref[0, 0, hh]])


def _mla_latent(mq, mk, mv, kc, vc, layer):
    seq_blk = CTX_TILES // LAT_TILES
    return pl.pallas_call(
        _mla_lat_kernel,
        grid=(N_LAT, LAT_TILES),
        in_specs=[pl.BlockSpec((1, MLA_H, TM, MLA_PAD), lambda n, j: (CTX_TILES + n * LAT_TILES + j, 0, 0, 0)),
                  pl.BlockSpec((LAT_TILES, MLA_H, TM, MLA_PAD), lambda n, j: (seq_blk + n, 0, 0, 0)),
                  pl.BlockSpec((LAT_TILES, MLA_H, TM, MLA_V), lambda n, j: (seq_blk + n, 0, 0, 0)),
                  pl.BlockSpec((1, 1, MLA_H, PAST, MLA_PAD), lambda n, j: (n, layer, 0, 0, 0)),
                  pl.BlockSpec((1, 1, MLA_H, PAST, MLA_V), lambda n, j: (n, layer, 0, 0, 0))],
        out_specs=pl.BlockSpec((TM, MLA_H * MLA_V), lambda n, j: (n * LAT_TILES + j, 0)),
        out_shape=jax.ShapeDtypeStruct((T_LAT, MLA_H * MLA_V), F32),
        compiler_params=_cparams(("arbitrary", "arbitrary")),
        name="mla_latent",
    )(mq, mk, mv, kc, vc)


def _post_kernel(y_ref, u_ref, d_ref, gw_ref, gb_ref, nac_ref, nal_ref, mlac_ref, mlal_ref, wo_ref, x_ref, mod_ref,
                 g2_ref, rw_ref, rb_ref, x1_ref, h2_ref, te_ref, tg_ref, rk_ref, cnt_ref):
    ys = y_ref[0] + y_ref[1] + u_ref[...] * d_ref[...]
    g = jax.nn.gelu(ys)
    s5o = g * jax.nn.sigmoid(jnp.dot(g.astype(BF16), gw_ref[...], preferred_element_type=F32) + gb_ref[...])
    is_ctx = pl.program_id(0) < CTX_TILES
    na = jnp.where(is_ctx, nac_ref[...], nal_ref[...])
    mla = jnp.where(is_ctx, mlac_ref[...], mlal_ref[...])
    out = (jnp.dot(s5o.astype(BF16), wo_ref[0:S5_W, :], preferred_element_type=F32)
           + jnp.dot(na.astype(BF16), wo_ref[S5_W:S5_W + NA_H * NA_D, :], preferred_element_type=F32)
           + jnp.dot(mla.astype(BF16), wo_ref[S5_W + NA_H * NA_D:, :], preferred_element_type=F32))
    mod = mod_ref[0]
    x1 = x_ref[...] + mod[2:3, :] * out
    x1_ref[...] = x1
    h2 = _rms(x1, g2_ref[...]) * (1.0 + mod[4:5, :]) + mod[3:4, :]
    h2_ref[...] = h2
    h_hi = h2.astype(BF16)
    h_lo = (h2 - h_hi.astype(F32)).astype(BF16)
    logits = (jnp.dot(h_hi, rw_ref[0], preferred_element_type=F32)
              + jnp.dot(h_hi, rw_ref[1], preferred_element_type=F32)
              + jnp.dot(h_lo, rw_ref[0], preferred_element_type=F32)) + rb_ref[...]
    lane_i = lax.broadcasted_iota(jnp.int32, (TM, 128), 1)
    lane = lane_i.astype(F32)
    cur = jnp.where(lane_i < N_EXP, logits, -jnp.inf)
    te = jnp.zeros((TM, 128), F32)
    tv = jnp.zeros((TM, 128), F32)
    hot = jnp.zeros((TM, 128), F32)
    idxs = []
    top = None
    for k in range(TOP_K):
        m = cur.max(axis=-1, keepdims=True)
        idx = jnp.where(cur == m, lane, 128.0).min(axis=-1, keepdims=True)
        sel = lane == idx
        top = m if top is None else top
        idxs.append(idx)
        te = jnp.where(lane_i == k, idx, te)
        tv = jnp.where(lane_i == k, jnp.exp(m - top), tv)
        hot = jnp.where(sel, 1.0, hot)
        cur = jnp.where(sel, -jnp.inf, cur)
    te_ref[...] = te.astype(jnp.int32)
    tg_ref[...] = tv / tv.sum(axis=-1, keepdims=True)
    row = lax.broadcasted_iota(jnp.int32, (TM, TM), 0)
    col = lax.broadcasted_iota(jnp.int32, (TM, TM), 1)
    tri = jnp.where(col < row, 1.0, 0.0).astype(BF16)
    before = jnp.dot(tri, hot.astype(BF16), preferred_element_type=F32)
    rk = jnp.zeros((TM, 128), F32)
    for k in range(TOP_K):
        rk = jnp.where(lane_i == k, jnp.where(lane == idxs[k], before, 0.0).sum(axis=-1, keepdims=True), rk)
    rk_ref[...] = rk.astype(jnp.int32)
    cnt_ref[0] = jnp.broadcast_to(hot.sum(axis=0, keepdims=True), (8, 128)).astype(jnp.int32)


def _post(y2, u, s5_d, glu_w, glu_b, na_ctx, na_lat, mla_ctx, mla_lat, w_out, x, mod_l, g2, rw, rb):
    const2 = lambda i: (0, 0)
    tile2 = lambda i: (i, 0)
    ctx2 = lambda i: (jnp.minimum(i, CTX_TILES - 1), 0)
    lat2 = lambda i: (jnp.maximum(i - CTX_TILES, 0), 0)
    return pl.pallas_call(
        _post_kernel,
        grid=(N_TILES,),
        in_specs=[pl.BlockSpec((2, TM, S5_W), lambda i: (0, i, 0)),
                  pl.BlockSpec((TM, S5_W), tile2),
                  pl.BlockSpec((1, S5_W), const2),
                  pl.BlockSpec((S5_W, S5_W), const2),
                  pl.BlockSpec((1, S5_W), const2),
                  pl.BlockSpec((TM, NA_H * NA_D), ctx2),
                  pl.BlockSpec((TM, NA_H * NA_D), lat2),
                  pl.BlockSpec((TM, MLA_H * MLA_V), ctx2),
                  pl.BlockSpec((TM, MLA_H * MLA_V), lat2),
                  pl.BlockSpec((D_MODEL, D_MODEL), const2),
                  pl.BlockSpec((TM, D_MODEL), tile2),
                  pl.BlockSpec((1, 6, D_MODEL), lambda i: (_mod_row(i), 0, 0)),
                  pl.BlockSpec((1, D_MODEL), const2),
                  pl.BlockSpec((2, D_MODEL, 128), lambda i: (0, 0, 0)),
                  pl.BlockSpec((1, 128), const2)],
        out_specs=[pl.BlockSpec((TM, D_MODEL), tile2),
                   pl.BlockSpec((TM, D_MODEL), tile2),
                   pl.BlockSpec((TM, 128), tile2),
                   pl.BlockSpec((TM, 128), tile2),
                   pl.BlockSpec((TM, 128), tile2),
                   pl.BlockSpec((1, 8, 128), lambda i: (i, 0, 0))],
        out_shape=[jax.ShapeDtypeStruct((T_ALL, D_MODEL), F32),
                   jax.ShapeDtypeStruct((T_ALL, D_MODEL), F32),
                   jax.ShapeDtypeStruct((T_ALL, 128), jnp.int32),
                   jax.ShapeDtypeStruct((T_ALL, 128), F32),
                   jax.ShapeDtypeStruct((T_ALL, 128), jnp.int32),
                   jax.ShapeDtypeStruct((N_TILES, 8, 128), jnp.int32)],
        compiler_params=_cparams(("arbitrary",)),
        name="post",
    )(y2, u, s5_d, glu_w, glu_b, na_ctx, na_lat, mla_ctx, mla_lat, w_out, x, mod_l, g2, rw, rb)


def _moe_fused_kernel(be_ref, first_ref, nused_ref, pos_ref,
                      dflt_hbm, h_hbm, wgu_ref, bgu_ref, wd_ref, bd_ref, ys_hbm,
                      asg_s, xs0, xs1, yo0, yo1, wgu_s, wd_s, act_s, gsem, ssem, tsem):
    b = pl.program_id(0)
    nused = nused_ref[0]
    xs = (xs0, xs1)
    yo = (yo0, yo1)

    def wait_gather(p):
        pltpu.make_async_copy(h_hbm.at[pl.ds(0, MOE_BM), :], xs[p], gsem.at[p]).wait()

    def wait_scatter(p):
        pltpu.make_async_copy(yo[p], ys_hbm.at[pl.ds(0, MOE_BM), :], ssem.at[p]).wait()

    def issue_gather(blk, p):
        r0 = (blk + 1) * MOE_BM
        for j in range(MOE_BM):
            tok = asg_s[r0 + j] >> 16
            pltpu.make_async_copy(h_hbm.at[pl.ds(tok, 1), :], xs[p].at[pl.ds(j, 1), :],
                                  gsem.at[p]).start(priority=j % 2)

    def issue_scatter(blk, p, lo=0, hi=MOE_BM):
        r0 = (blk + 1) * MOE_BM
        for j in range(lo, hi):
            d = asg_s[r0 + j] & 0xFFFF
            pltpu.make_async_copy(yo[p].at[pl.ds(j, 1), :], ys_hbm.at[pl.ds(d, 1), :],
                                  ssem.at[p]).start(priority=j % 2)

    @pl.when(b == 0)
    def _():
        cp = pltpu.make_async_copy(dflt_hbm, asg_s, tsem)
        cp.start()
        cp.wait()

        def inv(i, carry):
            a0 = i * 8
            v0 = a0 | ((((a0 >> 10) << 8) + (a0 & (TM - 1))) << 16)
            for uu in range(8):
                asg_s[pos_ref[a0 + uu]] = v0 + uu * ((1 << 16) + 1)
            return carry
        lax.fori_loop(0, MOE_SLOTS // 8, inv, 0)
        yo1[...] = jnp.zeros_like(yo1)
        zc = pltpu.make_async_copy(yo1, ys_hbm.at[pl.ds(MOE_SLOTS, MOE_BM), :], tsem)
        zc.start()
        zc.wait()
        issue_gather(0, 0)

    def step(p):
        q = 1 - p
        wait_gather(p)
        if p == 0:
            @pl.when(b >= 1)
            def _():
                wait_scatter(p)
        else:
            wait_scatter(p)

        @pl.when(first_ref[b] == 1)
        def _():
            wgu_s[...] = wgu_ref[0, 0].astype(BF16)
            wd_s[...] = wd_ref[0, 0].astype(BF16)

        issue_gather(jnp.minimum(b + 1, nused - 1), q)
        issue_scatter(b - 1, q, 0, MOE_BM // 2)
        gu = jnp.dot(xs[p][...].astype(BF16), wgu_s[...], preferred_element_type=F32) + bgu_ref[0, 0]
        gate = jnp.minimum(gu[:, :D_FF], SWIGLU_LIMIT)
        up = jnp.clip(gu[:, D_FF:], -SWIGLU_LIMIT, SWIGLU_LIMIT)
        act_s[...] = ((up + 1.0) * (gate * jax.nn.sigmoid(SWIGLU_ALPHA * gate))).astype(BF16)

        @pl.when(b >= 0)
        def _():
            issue_scatter(b - 1, q, MOE_BM // 2, MOE_BM)
            yo[p][...] = jnp.dot(act_s[...], wd_s[...], preferred_element_type=F32) + bd_ref[0, 0]

    def drain(p):
        q = 1 - p
        wait_gather(p)
        wait_scatter(p)
        issue_scatter(nused - 1, q)
        wait_scatter(q)

    for p in range(2):
        @pl.when((b < nused) & (b % 2 == p))
        def _():
            step(p)

        @pl.when((b == nused) & (b % 2 == p))
        def _():
            drain(p)


def _moe_fused(block_e, first, nused, pos_km, h2, w_gu, b_gu, w_down, b_down, layer):
    def wmap(b, be, fi, nu, pos):
        return (layer, be[b], 0, 0)

    r = jnp.arange(ASG_ROWS, dtype=jnp.int32)
    dflt = MOE_SLOTS + (((r >> 8) + 1) & 1) * MOE_BM + (r & (MOE_BM - 1))

    return pl.pallas_call(
        _moe_fused_kernel,
        grid_spec=pltpu.PrefetchScalarGridSpec(
            num_scalar_prefetch=4, grid=(MOE_BLOCKS + 1,),
            in_specs=[pl.BlockSpec(memory_space=pl.ANY), pl.BlockSpec(memory_space=pl.ANY),
                      pl.BlockSpec((1, 1, D_MODEL, 2 * D_FF), wmap),
                      pl.BlockSpec((1, 1, 1, 2 * D_FF), wmap),
                      pl.BlockSpec((1, 1, D_FF, D_MODEL), wmap),
                      pl.BlockSpec((1, 1, 1, D_MODEL), wmap)],
            out_specs=pl.BlockSpec(memory_space=pl.ANY),
            scratch_shapes=[pltpu.SMEM((ASG_ROWS,), jnp.int32),
                            pltpu.VMEM((MOE_BM, D_MODEL), F32), pltpu.VMEM((MOE_BM, D_MODEL), F32),
                            pltpu.VMEM((MOE_BM, D_MODEL), F32), pltpu.VMEM((MOE_BM, D_MODEL), F32),
                            pltpu.VMEM((D_MODEL, 2 * D_FF), BF16), pltpu.VMEM((D_FF, D_MODEL), BF16),
                            pltpu.VMEM((MOE_BM, D_FF), BF16),
                            pltpu.SemaphoreType.DMA((2,)), pltpu.SemaphoreType.DMA((2,)),
                            pltpu.SemaphoreType.DMA(())]),
        out_shape=jax.ShapeDtypeStruct((MOE_SLOTS + 2 * MOE_BM, D_MODEL), F32),
        compiler_params=_cparams(("arbitrary",)),
        name="moe_fused",
    )(block_e, first, nused, pos_km, dflt, h2, w_gu, b_gu.reshape(DEPTH, N_EXP, 1, 2 * D_FF), w_down,
      b_down.reshape(DEPTH, N_EXP, 1, D_MODEL))


def _combine_kernel(ys_ref, tg_ref, x1_ref, mod_ref, o_ref):
    tg = tg_ref[...]
    y = ys_ref[0:TM, :] * tg[:, 0:1]
    for k in range(1, TOP_K):
        y = y + ys_ref[k * TM:(k + 1) * TM, :] * tg[:, k:k + 1]
    o_ref[...] = x1_ref[...] + mod_ref[0][5:6, :] * y


def _combine(ys, tg, x1, mod_l):
    return pl.pallas_call(
        _combine_kernel,
        grid=(N_TILES,),
        in_specs=[pl.BlockSpec((TOP_K * TM, D_MODEL), lambda i: (i, 0)),
                  pl.BlockSpec((TM, 128), lambda i: (i, 0)),
                  pl.BlockSpec((TM, D_MODEL), lambda i: (i, 0)),
                  pl.BlockSpec((1, 6, D_MODEL), lambda i: (_mod_row(i), 0, 0))],
        out_specs=pl.BlockSpec((TM, D_MODEL), lambda i: (i, 0)),
        out_shape=jax.ShapeDtypeStruct((T_ALL, D_MODEL), F32),
        compiler_params=_cparams(("arbitrary",)),
        name="moe_combine",
    )(ys, tg, x1, mod_l)


def _moe_plan(te, rk, cnt):
    tile_cnt = cnt[:, 0, :N_EXP]
    counts = tile_cnt.sum(axis=0)
    padded = (counts + MOE_BM - 1) // MOE_BM * MOE_BM
    pad_end = jnp.cumsum(padded)
    pad_start = pad_end - padded
    tile_off = jnp.cumsum(tile_cnt, axis=0) - tile_cnt
    base = (pad_start[None, :] + tile_off).astype(jnp.int32)
    e = te[:, :TOP_K].reshape(N_TILES, TM, TOP_K)
    hot = e[..., None] == jnp.arange(N_EXP, dtype=jnp.int32)
    pos = jnp.sum(jnp.where(hot, base[:, None, None, :], 0), axis=-1) + rk[:, :TOP_K].reshape(N_TILES, TM, TOP_K)
    pos_km = (pos.transpose(0, 2, 1).reshape(-1) + MOE_BM).astype(jnp.int32)
    nused = (pad_end[-1] // MOE_BM).astype(jnp.int32)
    blk = jnp.arange(MOE_BLOCKS + 1, dtype=jnp.int32)
    be = jnp.minimum(jnp.sum((pad_end[None, :] <= (blk * MOE_BM)[:, None]).astype(jnp.int32), axis=1), N_EXP - 1)
    be = jnp.where(blk < nused, be, be[nused - 1])
    first = jnp.concatenate([jnp.ones((1,), jnp.int32), (be[1:] != be[:-1]).astype(jnp.int32)])
    return pos_km, be, first, nused.reshape(1)


def _final_kernel(x_ref, g_ref, o_ref):
    o_ref[...] = _rms(x_ref[...], g_ref[...])


def _final_norm(x, g):
    return pl.pallas_call(
        _final_kernel,
        grid=(N_TILES,),
        in_specs=[pl.BlockSpec((TM, D_MODEL), lambda i: (i, 0)), pl.BlockSpec((1, D_MODEL), lambda i: (0, 0))],
        out_specs=pl.BlockSpec((TM, D_MODEL), lambda i: (i, 0)),
        out_shape=jax.ShapeDtypeStruct((T_ALL, D_MODEL), F32),
        compiler_params=_cparams(("arbitrary",)),
        name="final_norm",
    )(x, g)


_ROPE_PERM = np.concatenate([np.arange(8, 16), np.arange(0, 8), np.arange(24, 32), np.arange(16, 24)])


def _rope_tables():
    half = MLA_ROPE // 2
    t = jnp.arange(L_LAT)
    row = (t // GRID_W).astype(F32)
    col = (t % GRID_W).astype(F32)
    inv = ROPE_BASE ** (-jnp.arange(0, half, 2, dtype=F32) / half)

    def part(pos):
        ang = pos[:, None] * inv[None, :]
        c, s = jnp.cos(ang), jnp.sin(ang)
        return jnp.concatenate([c, c], axis=-1), jnp.concatenate([-s, s], axis=-1)

    cr, sr = part(row)
    cc, sc = part(col)
    cos32 = jnp.concatenate([cr, cc], axis=-1)
    sin32 = jnp.concatenate([sr, sc], axis=-1)
    ck = jnp.concatenate([jnp.ones((TM, MLA_ROPE), F32), cos32], axis=0)
    sk = jnp.concatenate([jnp.zeros((TM, MLA_ROPE), F32), sin32], axis=0)
    padw = MLA_PAD - MLA_NOPE - MLA_ROPE
    cq = jnp.concatenate([jnp.ones((TM + L_LAT, MLA_NOPE), F32), ck, jnp.ones((TM + L_LAT, padw), F32)], axis=-1)
    sq = jnp.concatenate([jnp.zeros((TM + L_LAT, MLA_NOPE), F32), sk, jnp.zeros((TM + L_LAT, padw), F32)], axis=-1)
    return cq, sq, ck, sk


def _mla_weights(w_qb, w_kvb):
    dq = MLA_NOPE + MLA_ROPE
    wq = w_qb.reshape(MLA_QL, MLA_H, dq)
    zpad = jnp.zeros((MLA_QL, MLA_H, MLA_PAD - dq), F32)
    q_main = jnp.concatenate([wq, zpad], axis=-1)
    q_part = jnp.concatenate([jnp.zeros((MLA_QL, MLA_H, MLA_NOPE), F32), wq[:, :, MLA_NOPE + _ROPE_PERM], zpad], axis=-1)
    wq2 = jnp.concatenate([q_main.reshape(MLA_QL, -1), q_part.reshape(MLA_QL, -1)], axis=-1).astype(BF16)
    wkv = w_kvb.reshape(MLA_KVL, MLA_H, MLA_NOPE + MLA_V)
    k_top = jnp.concatenate([wkv[:, :, :MLA_NOPE], jnp.zeros((MLA_KVL, MLA_H, MLA_PAD - MLA_NOPE), F32)], axis=-1)
    place = jnp.concatenate([jnp.zeros((MLA_ROPE, MLA_NOPE), F32), jnp.eye(MLA_ROPE, dtype=F32),
                             jnp.zeros((MLA_ROPE, MLA_PAD - dq), F32)], axis=-1)
    k_bot = jnp.broadcast_to(place[:, None, :], (MLA_ROPE, MLA_H, MLA_PAD))
    wkk = jnp.concatenate([k_top, k_bot], axis=0).reshape(MLA_KVL + MLA_ROPE, MLA_H * MLA_PAD).astype(BF16)
    wv = wkv[:, :, MLA_NOPE:].reshape(MLA_KVL, MLA_H * MLA_V).astype(BF16)
    return wq2, wkk, wv


def kernel(x_prompt, x_sample, cache_na_k, cache_na_v, cache_mla_kv, state_s5_re, state_s5_im, c, c_ctx, w_mod, b_mod, norm1_g, norm2_g, w_in, w_out, s5_lambda_re, s5_lambda_im, s5_log_dt, s5_b_re, s5_b_im, s5_c_re, s5_c_im, s5_d, s5_glu_w, s5_glu_b, na_rpb, mla_q_norm_g, mla_w_qb, mla_kv_norm_g, mla_w_kvb, router_w, router_b, moe_w_gu, moe_b_gu, moe_w_down, moe_b_down, final_norm_g):
    x = jnp.concatenate([x_prompt.reshape(T_CTX, D_MODEL), x_sample.reshape(T_LAT, D_MODEL)], axis=0)
    cv = jnp.concatenate([c_ctx[None, :], c, jnp.zeros((MOD_ROWS - 1 - N_LAT, D_MODEL), F32)], axis=0)
    mod = _modulation(cv, w_mod, b_mod).reshape(DEPTH, MOD_ROWS, 6, D_MODEL)
    cq, sq, ck, sk = _rope_tables()
    mla_w = [_mla_weights(mla_w_qb[l], mla_w_kvb[l]) for l in range(DEPTH)]
    kc_all, vc_all = _cache_kv(cache_mla_kv, jnp.stack([m[1] for m in mla_w]), jnp.stack([m[2] for m in mla_w]))
    gp = S5_G * S5_P
    na_bias = _na_bias_tables(na_rpb)
    fin_k = jnp.zeros((N_CTX, DEPTH, NA_H, L_CTX, NA_D), F32)
    fin_v = jnp.zeros((N_CTX, DEPTH, NA_H, L_CTX, NA_D), F32)
    fin_m = jnp.zeros((N_CTX, DEPTH, L_CTX, MLA_KVL + MLA_ROPE), F32)

    s5_re_list, s5_im_list = [], []
    for l in range(DEPTH):
        mod_l = mod[l]
        wq2, wkk, wv = mla_w[l]
        w_in_ext = jnp.concatenate([w_in[l], w_in[l][:, OFF_KVA + MLA_KVL + _ROPE_PERM]], axis=-1).astype(BF16)
        u, naq, nak, nav, mq, mk, mv, ckv = _project(
            x, mod_l, norm1_g[l][None], w_in_ext, mla_q_norm_g[l][None], wq2, mla_kv_norm_g[l][None], wkk, wv,
            cq, sq, ck, sk)

        a, bb = _s5_discretize(s5_lambda_re[l], s5_lambda_im[l], s5_log_dt[l], s5_b_re[l], s5_b_im[l])
        a_lanes, wb, wc = _s5_matrices(a, bb, s5_c_re[l], s5_c_im[l])
        u_ctx = u[:T_CTX].reshape(N_CTX, L_CTX, S5_W).transpose(1, 0, 2)
        y_ctx, ht_ctx = _s5_scan(u_ctx, wb, wc, a_lanes, jnp.zeros((2, N_CTX, S5_STATE_W), F32))
        u_lat = u[T_CTX:].reshape(N_LAT, L_LAT, S5_W).transpose(1, 0, 2)
        u_lat = jnp.pad(u_lat, ((0, 0), (0, SCAN_ROWS - N_LAT), (0, 0)))
        h0 = jnp.concatenate([state_s5_re[:, l].reshape(N_LAT, 2, gp), state_s5_im[:, l].reshape(N_LAT, 2, gp)], axis=-1)
        h0 = jnp.pad(h0.transpose(1, 0, 2), ((0, 0), (0, SCAN_ROWS - N_LAT), (0, 0)))
        y_lat, _ = _s5_scan(u_lat, wb, wc, a_lanes, h0)
        y2 = jnp.concatenate([y_ctx.transpose(0, 2, 1, 3).reshape(2, T_CTX, S5_W),
                              y_lat[:, :, :N_LAT].transpose(0, 2, 1, 3).reshape(2, T_LAT, S5_W)], axis=1)
        st = ht_ctx.reshape(2, N_CTX, 2, S5_G, S5_P).transpose(1, 0, 2, 3, 4)
        s5_re_list.append(st[:, :, 0])
        s5_im_list.append(st[:, :, 1])

        na_ctx, mla_ctx, fin_k, fin_v, fin_m = _ctx_attention(naq, nak, nav, mq, mk, mv, ckv, fin_k, fin_v, fin_m, l)
        na_lat = _na_latent(naq, nak, nav, cache_na_k, cache_na_v, na_bias, l)
        mla_lat = _mla_latent(mq, mk, mv, kc_all, vc_all, l)

        rw = jnp.pad(router_w[l], ((0, 0), (0, 128 - N_EXP)))
        rw_hi = rw.astype(BF16)
        rw = jnp.stack([rw_hi, (rw - rw_hi.astype(F32)).astype(BF16)], axis=0)
        rb = jnp.pad(router_b[l], (0, 128 - N_EXP))[None]
        x1, h2, te, tg, rk, cnt = _post(y2, u, s5_d[l][None], s5_glu_w[l].astype(BF16), s5_glu_b[l][None],
                                        na_ctx, na_lat, mla_ctx, mla_lat, w_out[l].astype(BF16), x, mod_l,
                                        norm2_g[l][None], rw, rb)
        pos_km, be, first, nused = _moe_plan(te, rk, cnt)
        ys = _moe_fused(be, first, nused, pos_km, h2, moe_w_gu, moe_b_gu, moe_w_down, moe_b_down, l)
        x = _combine(ys, tg, x1, mod_l)

    y = _final_norm(x, final_norm_g[None])
    y_prompt = y[:T_CTX].reshape(N_CTX, L_CTX, D_MODEL)
    y_sample = y[T_CTX:].reshape(N_LAT, L_LAT, D_MODEL)
    return (y_prompt, y_sample, fin_k, fin_v, fin_m, jnp.stack(s5_re_list, axis=1), jnp.stack(s5_im_list, axis=1))
```

```python
import functools
import math

import numpy as np
import jax
import jax.numpy as jnp
from jax import lax
from jax.experimental import pallas as pl
from jax.experimental.pallas import tpu as pltpu

F32 = jnp.float32
BF16 = jnp.bfloat16

D_MODEL = 1024
N_CTX, L_CTX = 32, 256
N_LAT, L_LAT = 2, 2048
DEPTH = 4
PAST = 256
GRID_W = 64
S5_W, S5_G, S5_P, S5_C = 256, 16, 64, 16
NA_H, NA_D = 4, 64
WIN_H, WIN_W = 8, 16
MLA_H, MLA_NOPE, MLA_ROPE, MLA_V = 8, 64, 32, 64
MLA_QL, MLA_KVL = 256, 128
MLA_SCALE = (MLA_NOPE + MLA_ROPE) ** -0.5
NA_SCALE = NA_D ** -0.5
OFF_NA = S5_W
OFF_QA = OFF_NA + 3 * NA_H * NA_D
OFF_KVA = OFF_QA + MLA_QL
D_IN = OFF_KVA + MLA_KVL + MLA_ROPE
N_EXP, TOP_K, D_FF = 32, 4, 1024
SWIGLU_ALPHA, SWIGLU_LIMIT = 1.702, 7.0
ROPE_BASE = 10000.0
EPS = 1e-6

TM = 256
T_CTX = N_CTX * L_CTX
T_LAT = N_LAT * L_LAT
T_ALL = T_CTX + T_LAT
CTX_TILES = T_CTX // TM
LAT_TILES = L_LAT // TM
N_TILES = T_ALL // TM
MOD_ROWS = 8
MLA_PAD = 128
D_IN_EXT = D_IN + MLA_ROPE
SCAN_ROWS = 8
SCAN_CH = 64
S5_STATE_W = 2 * S5_G * S5_P
N_SEG = SCAN_ROWS // N_LAT
S5_SEG = L_LAT // N_SEG
MOE_BM = 256
MOE_SLOTS = T_ALL * TOP_K
MOE_ROWS = MOE_SLOTS + N_EXP * MOE_BM
MOE_BLOCKS = MOE_ROWS // MOE_BM
ASG_ROWS = 57 * 1024
NEG = -1e30
VMEM_LIMIT = 56 * 1024 * 1024


def _cparams(sem):
    return pltpu.CompilerParams(dimension_semantics=sem, vmem_limit_bytes=VMEM_LIMIT)


def _mod_row(i):
    return jnp.maximum(i - (CTX_TILES - LAT_TILES), 0) // LAT_TILES


def _rope_block(i):
    return jnp.where(i < CTX_TILES, 0, 1 + (i - CTX_TILES) % LAT_TILES)


def _rms(x, g):
    return x * lax.rsqrt(jnp.mean(x * x, axis=-1, keepdims=True) + EPS) * g


def _mod_kernel(cv_ref, w_ref, b_ref, o_ref):
    s = jax.nn.silu(cv_ref[...]).astype(BF16)
    o_ref[0] = jnp.dot(s, w_ref[0].astype(BF16), preferred_element_type=F32) + b_ref[0]


def _modulation(cv, w_mod, b_mod):
    tn = 1536
    return pl.pallas_call(
        _mod_kernel,
        grid=(DEPTH, 6 * D_MODEL // tn),
        in_specs=[pl.BlockSpec((MOD_ROWS, D_MODEL), lambda l, j: (0, 0)),
                  pl.BlockSpec((1, D_MODEL, tn), lambda l, j: (l, 0, j)),
                  pl.BlockSpec((1, 1, tn), lambda l, j: (l, 0, j))],
        out_specs=pl.BlockSpec((1, MOD_ROWS, tn), lambda l, j: (l, 0, j)),
        out_shape=jax.ShapeDtypeStruct((DEPTH, MOD_ROWS, 6 * D_MODEL), F32),
        compiler_params=_cparams(("arbitrary", "arbitrary")),
        name="modulation",
    )(cv, w_mod, b_mod.reshape(DEPTH, 1, 6 * D_MODEL))


def _proj_kernel(x_ref, mod_ref, g1_ref, win_ref, qg_ref, wq_ref, kvg_ref, wkk_ref, wv_ref,
                 cq_ref, sq_ref, ck_ref, sk_ref,
                 u_ref, naq_ref, nak_ref, nav_ref, mq_ref, mk_ref, mv_ref, ckv_ref):
    x = x_ref[...]
    mod = mod_ref[0]
    h = _rms(x, g1_ref[...]) * (1.0 + mod[1:2, :]) + mod[0:1, :]
    p = jnp.dot(h.astype(BF16), win_ref[...], preferred_element_type=F32)
    u_ref[...] = p[:, :OFF_NA]
    for hh in range(NA_H):
        naq_ref[0, hh] = p[:, OFF_NA + hh * NA_D: OFF_NA + (hh + 1) * NA_D].astype(BF16)
        nak_ref[0, hh] = p[:, OFF_NA + NA_H * NA_D + hh * NA_D: OFF_NA + NA_H * NA_D + (hh + 1) * NA_D]
        nav_ref[0, hh] = p[:, OFF_NA + 2 * NA_H * NA_D + hh * NA_D: OFF_NA + 2 * NA_H * NA_D + (hh + 1) * NA_D]
    qn = _rms(p[:, OFF_QA:OFF_KVA], qg_ref[...]).astype(BF16)
    qq = jnp.dot(qn, wq_ref[...], preferred_element_type=F32)
    cq = cq_ref[...]
    sq = sq_ref[...]
    for hh in range(MLA_H):
        qa = qq[:, hh * MLA_PAD:(hh + 1) * MLA_PAD]
        qb = qq[:, (MLA_H + hh) * MLA_PAD:(MLA_H + hh + 1) * MLA_PAD]
        mq_ref[0, hh] = (qa * cq + qb * sq).astype(BF16)
    ckv = _rms(p[:, OFF_KVA:OFF_KVA + MLA_KVL], kvg_ref[...])
    kr = p[:, OFF_KVA + MLA_KVL:D_IN]
    kr_partner = p[:, D_IN:D_IN_EXT]
    kr_rot = kr * ck_ref[...] + kr_partner * sk_ref[...]
    ckv_ref[...] = jnp.concatenate([ckv, kr], axis=-1)
    kin = jnp.concatenate([ckv, kr_rot], axis=-1).astype(BF16)
    kk = jnp.dot(kin, wkk_ref[...], preferred_element_type=F32)
    vv = jnp.dot(kin[:, :MLA_KVL], wv_ref[...], preferred_element_type=F32)
    for hh in range(MLA_H):
        mk_ref[0, hh] = kk[:, hh * MLA_PAD:(hh + 1) * MLA_PAD].astype(BF16)
        mv_ref[0, hh] = vv[:, hh * MLA_V:(hh + 1) * MLA_V].astype(BF16)


def _project(x, mod_l, g1, w_in_ext, qg, wq2, kvg, wkk, wv, cq, sq, ck, sk):
    const2 = lambda i: (0, 0)
    tile2 = lambda i: (i, 0)
    head4 = lambda i: (i, 0, 0, 0)
    rope2 = lambda i: (_rope_block(i), 0)
    return pl.pallas_call(
        _proj_kernel,
        grid=(N_TILES,),
        in_specs=[pl.BlockSpec((TM, D_MODEL), tile2),
                  pl.BlockSpec((1, 6, D_MODEL), lambda i: (_mod_row(i), 0, 0)),
                  pl.BlockSpec((1, D_MODEL), const2),
                  pl.BlockSpec((D_MODEL, D_IN_EXT), const2),
                  pl.BlockSpec((1, MLA_QL), const2),
                  pl.BlockSpec((MLA_QL, 2 * MLA_H * MLA_PAD), const2),
                  pl.BlockSpec((1, MLA_KVL), const2),
                  pl.BlockSpec((MLA_KVL + MLA_ROPE, MLA_H * MLA_PAD), const2),
                  pl.BlockSpec((MLA_KVL, MLA_H * MLA_V), const2),
                  pl.BlockSpec((TM, MLA_PAD), rope2),
                  pl.BlockSpec((TM, MLA_PAD), rope2),
                  pl.BlockSpec((TM, MLA_ROPE), rope2),
                  pl.BlockSpec((TM, MLA_ROPE), rope2)],
        out_specs=[pl.BlockSpec((TM, S5_W), tile2),
                   pl.BlockSpec((1, NA_H, TM, NA_D), head4),
                   pl.BlockSpec((1, NA_H, TM, NA_D), head4),
                   pl.BlockSpec((1, NA_H, TM, NA_D), head4),
                   pl.BlockSpec((1, MLA_H, TM, MLA_PAD), head4),
                   pl.BlockSpec((1, MLA_H, TM, MLA_PAD), head4),
                   pl.BlockSpec((1, MLA_H, TM, MLA_V), head4),
                   pl.BlockSpec((TM, MLA_KVL + MLA_ROPE), tile2)],
        out_shape=[jax.ShapeDtypeStruct((T_ALL, S5_W), F32),
                   jax.ShapeDtypeStruct((N_TILES, NA_H, TM, NA_D), BF16),
                   jax.ShapeDtypeStruct((N_TILES, NA_H, TM, NA_D), F32),
                   jax.ShapeDtypeStruct((N_TILES, NA_H, TM, NA_D), F32),
                   jax.ShapeDtypeStruct((N_TILES, MLA_H, TM, MLA_PAD), BF16),
                   jax.ShapeDtypeStruct((N_TILES, MLA_H, TM, MLA_PAD), BF16),
                   jax.ShapeDtypeStruct((N_TILES, MLA_H, TM, MLA_V), BF16),
                   jax.ShapeDtypeStruct((T_ALL, MLA_KVL + MLA_ROPE), F32)],
        compiler_params=_cparams(("arbitrary",)),
        name="proj",
    )(x, mod_l, g1, w_in_ext, qg, wq2, kvg, wkk, wv, cq, sq, ck, sk)


def _cache_kv_kernel(c_ref, wkk_ref, wv_ref, k_ref, v_ref):
    cin = c_ref[0, 0].astype(BF16)
    kk = jnp.dot(cin, wkk_ref[0], preferred_element_type=F32)
    vv = jnp.dot(cin[:, :MLA_KVL], wv_ref[0], preferred_element_type=F32)
    for hh in range(MLA_H):
        k_ref[0, 0, hh] = kk[:, hh * MLA_PAD:(hh + 1) * MLA_PAD].astype(BF16)
        v_ref[0, 0, hh] = vv[:, hh * MLA_V:(hh + 1) * MLA_V].astype(BF16)


def _cache_kv(cache_mla_kv, wkk_all, wv_all):
    return pl.pallas_call(
        _cache_kv_kernel,
        grid=(N_LAT, DEPTH),
        in_specs=[pl.BlockSpec((1, 1, PAST, MLA_KVL + MLA_ROPE), lambda n, l: (n, l, 0, 0)),
                  pl.BlockSpec((1, MLA_KVL + MLA_ROPE, MLA_H * MLA_PAD), lambda n, l: (l, 0, 0)),
                  pl.BlockSpec((1, MLA_KVL, MLA_H * MLA_V), lambda n, l: (l, 0, 0))],
        out_specs=[pl.BlockSpec((1, 1, MLA_H, PAST, MLA_PAD), lambda n, l: (n, l, 0, 0, 0)),
                   pl.BlockSpec((1, 1, MLA_H, PAST, MLA_V), lambda n, l: (n, l, 0, 0, 0))],
        out_shape=[jax.ShapeDtypeStruct((N_LAT, DEPTH, MLA_H, PAST, MLA_PAD), BF16),
                   jax.ShapeDtypeStruct((N_LAT, DEPTH, MLA_H, PAST, MLA_V), BF16)],
        compiler_params=_cparams(("arbitrary", "arbitrary")),
        name="cache_kv",
    )(cache_mla_kv, wkk_all, wv_all)


def _s5_disc_kernel(lr_ref, li_ref, ldt_ref, bre_ref, bim_ref, a_ref, bb_ref):
    lr = lr_ref[...]
    li = li_ref[...]
    dt = jnp.exp(ldt_ref[...])
    mag = jnp.exp(lr * dt)
    a_re = mag * jnp.cos(li * dt)
    a_im = mag * jnp.sin(li * dt)
    den = lr * lr + li * li
    nr = a_re - 1.0
    k_re = (nr * lr + a_im * li) / den
    k_im = (a_im * lr - nr * li) / den
    a_ref[0] = a_re
    a_ref[1] = a_im
    bre = bre_ref[...]
    bim = bim_ref[...]
    bb_ref[0] = k_re * bre - k_im * bim
    bb_ref[1] = k_re * bim + k_im * bre


def _s5_discretize(lam_re, lam_im, log_dt, b_re, b_im):
    g2 = 2 * S5_G
    a, bb = pl.pallas_call(
        _s5_disc_kernel,
        out_shape=[jax.ShapeDtypeStruct((2, g2, 1, S5_P), F32),
                   jax.ShapeDtypeStruct((2, g2, S5_C, S5_P), F32)],
        name="s5_disc",
    )(lam_re.reshape(g2, 1, S5_P), lam_im.reshape(g2, 1, S5_P), log_dt.reshape(g2, 1, 1),
      b_re.reshape(g2, S5_P, S5_C).transpose(0, 2, 1), b_im.reshape(g2, S5_P, S5_C).transpose(0, 2, 1))
    return a, bb


def _s5_matrices(a, bb, c_re, c_im):
    gp = S5_G * S5_P
    a_lanes = a.reshape(2, 2, S5_G, S5_P).transpose(1, 0, 2, 3).reshape(2, 2, gp)
    bbar = bb.reshape(2, 2, S5_G, S5_C, S5_P)
    eye = jnp.eye(S5_G, dtype=F32)
    wb = jnp.einsum('xdgcp,gh->dgcxhp', bbar, eye).reshape(2, S5_W, 2 * gp)
    cc = jnp.stack([c_re, -c_im], axis=1)
    wc = jnp.einsum('dxgcp,gh->dxhpgc', cc, eye).reshape(2, 2 * gp, S5_W)
    return a_lanes, wb.astype(BF16), wc.astype(BF16)


def _scan_kernel(u_ref, wb_ref, wc_ref, a_ref, h0_ref, y_ref, ht_ref, x_s, h_s, st_s):
    d = pl.program_id(0)
    c = pl.program_id(2)
    nc = pl.num_programs(2)
    gp = S5_G * S5_P

    @pl.when(c == 0)
    def _():
        st_s[...] = h0_ref[0]

    u = u_ref[...].reshape(SCAN_CH * SCAN_ROWS, S5_W).astype(BF16)
    x_s[...] = jnp.dot(u, wb_ref[0], preferred_element_type=F32)
    a_re = a_ref[0, 0:1, :]
    a_im = a_ref[0, 1:2, :]
    unroll = 4

    def outer(io, carry):
        h_re, h_im = carry
        for ii in range(unroll):
            i = io * unroll + ii
            t = jnp.where(d == 0, i, SCAN_CH - 1 - i)
            r = pl.multiple_of(t * SCAN_ROWS, SCAN_ROWS)
            x_re = x_s[pl.ds(r, SCAN_ROWS), pl.ds(0, gp)]
            x_im = x_s[pl.ds(r, SCAN_ROWS), pl.ds(gp, gp)]
            n_re = a_re * h_re - a_im * h_im + x_re
            n_im = a_re * h_im + a_im * h_re + x_im
            h_s[pl.ds(r, SCAN_ROWS), pl.ds(0, gp)] = n_re
            h_s[pl.ds(r, SCAN_ROWS), pl.ds(gp, gp)] = n_im
            h_re, h_im = n_re, n_im
        return h_re, h_im

    h_re, h_im = lax.fori_loop(0, SCAN_CH // unroll, outer, (st_s[:, pl.ds(0, gp)], st_s[:, pl.ds(gp, gp)]))
    st_s[:, pl.ds(0, gp)] = h_re
    st_s[:, pl.ds(gp, gp)] = h_im
    y = jnp.dot(h_s[...].astype(BF16), wc_ref[0], preferred_element_type=F32)
    y_ref[0] = y.reshape(SCAN_CH, SCAN_ROWS, S5_W)

    @pl.when(c == nc - 1)
    def _():
        ht_ref[0] = st_s[...]


def _s5_scan(u_tm, wb, wc, a_lanes, h0):
    L, n, _ = u_tm.shape
    nc = L // SCAN_CH
    ng = n // SCAN_ROWS

    def ceff(d, c):
        return c + d * (nc - 1 - 2 * c)

    return pl.pallas_call(
        _scan_kernel,
        grid=(2, ng, nc),
        in_specs=[pl.BlockSpec((SCAN_CH, SCAN_ROWS, S5_W), lambda d, g, c: (ceff(d, c), g, 0)),
                  pl.BlockSpec((1, S5_W, S5_STATE_W), lambda d, g, c: (d, 0, 0)),
                  pl.BlockSpec((1, S5_STATE_W, S5_W), lambda d, g, c: (d, 0, 0)),
                  pl.BlockSpec((1, 2, S5_G * S5_P), lambda d, g, c: (d, 0, 0)),
                  pl.BlockSpec((1, SCAN_ROWS, S5_STATE_W), lambda d, g, c: (d, g, 0))],
        out_specs=[pl.BlockSpec((1, SCAN_CH, SCAN_ROWS, S5_W), lambda d, g, c: (d, ceff(d, c), g, 0)),
                   pl.BlockSpec((1, SCAN_ROWS, S5_STATE_W), lambda d, g, c: (d, g, 0))],
        out_shape=[jax.ShapeDtypeStruct((2, L, n, S5_W), F32),
                   jax.ShapeDtypeStruct((2, n, S5_STATE_W), F32)],
        scratch_shapes=[pltpu.VMEM((SCAN_CH * SCAN_ROWS, S5_STATE_W), F32),
                        pltpu.VMEM((SCAN_CH * SCAN_ROWS, S5_STATE_W), F32),
                        pltpu.VMEM((SCAN_ROWS, S5_STATE_W), F32)],
        compiler_params=_cparams(("arbitrary", "arbitrary", "arbitrary")),
        name="s5_scan",
    )(u_tm, wb, wc, a_lanes, h0)


def _s5_pow_kernel(a_ref, p_ref):
    gp = S5_G * S5_P
    a_re = a_ref[0, 0:1, :]
    a_im = a_ref[0, 1:2, :]
    row = lax.broadcasted_iota(jnp.int32, (8, gp), 0)
    cur_re = jnp.broadcast_to(a_re, (8, gp))
    cur_im = jnp.broadcast_to(a_im, (8, gp))
    pr, pi = a_re, a_im
    for k in range(1, 8):
        pr, pi = pr * a_re - pi * a_im, pr * a_im + pi * a_re
        cur_re = jnp.where(row == k, pr, cur_re)
        cur_im = jnp.where(row == k, pi, cur_im)
    p_ref[0, 0, 0:8, :] = cur_re
    p_ref[0, 1, 0:8, :] = cur_im
    m = 8
    while m < S5_SEG:
        am_re = p_ref[0, 0, m - 1:m, :]
        am_im = p_ref[0, 1, m - 1:m, :]
        lo_re = p_ref[0, 0, 0:m, :]
        lo_im = p_ref[0, 1, 0:m, :]
        p_ref[0, 0, m:2 * m, :] = lo_re * am_re - lo_im * am_im
        p_ref[0, 1, m:2 * m, :] = lo_re * am_im + lo_im * am_re
        m *= 2


def _s5_powers(a_lanes):
    gp = S5_G * S5_P
    return pl.pallas_call(
        _s5_pow_kernel,
        grid=(2,),
        in_specs=[pl.BlockSpec((1, 2, gp), lambda d: (d, 0, 0))],
        out_specs=pl.BlockSpec((1, 2, S5_SEG, gp), lambda d: (d, 0, 0, 0)),
        out_shape=jax.ShapeDtypeStruct((2, 2, S5_SEG, gp), F32),
        compiler_params=_cparams(("arbitrary",)),
        name="s5_powers",
    )(a_lanes)


def _s5_fix_kernel(y_in, p_ref, ht_ref, wc_ref, y_out, hin_s):
    gp = S5_G * S5_P
    d = pl.program_id(0)
    r = pl.program_id(1)

    @pl.when(r == 0)
    def _():
        hin_s[...] = jnp.zeros_like(hin_s)

        def chain(order, pw_row):
            aw_re = p_ref[0, 0, pw_row:pw_row + 1, :]
            aw_im = p_ref[0, 1, pw_row:pw_row + 1, :]
            for n in range(N_LAT):
                h_re = jnp.zeros((1, gp), F32)
                h_im = jnp.zeros((1, gp), F32)
                for s_prev, s_cur in zip(order[:-1], order[1:]):
                    e_re = ht_ref[0, n * N_SEG + s_prev:n * N_SEG + s_prev + 1, 0:gp]
                    e_im = ht_ref[0, n * N_SEG + s_prev:n * N_SEG + s_prev + 1, gp:2 * gp]
                    h_re, h_im = aw_re * h_re - aw_im * h_im + e_re, aw_re * h_im + aw_im * h_re + e_im
                    hin_s[n * N_SEG + s_cur, :, 0:gp] = h_re
                    hin_s[n * N_SEG + s_cur, :, gp:2 * gp] = h_im

        @pl.when(d == 0)
        def _():
            chain(list(range(N_SEG)), S5_SEG - 1)

        @pl.when(d == 1)
        def _():
            chain(list(range(N_SEG - 1, -1, -1)), 0)

    hin = hin_s[r]
    h_re = hin[:, 0:gp]
    h_im = hin[:, gp:2 * gp]
    p_re = p_ref[0, 0]
    p_im = p_ref[0, 1]
    z = jnp.concatenate([p_re * h_re - p_im * h_im, p_re * h_im + p_im * h_re], axis=-1).astype(BF16)
    y_out[0] = y_in[0] + jnp.dot(z, wc_ref[0], preferred_element_type=F32)


def _s5_fix(y_tok, ptab, ht_loc, wc):
    gp = S5_G * S5_P
    return pl.pallas_call(
        _s5_fix_kernel,
        grid=(2, SCAN_ROWS),
        in_specs=[pl.BlockSpec((1, S5_SEG, S5_W), lambda d, r: (d, r, 0)),
                  pl.BlockSpec((1, 2, S5_SEG, gp), lambda d, r: (d, 0, 0, 0)),
                  pl.BlockSpec((1, SCAN_ROWS, S5_STATE_W), lambda d, r: (d, 0, 0)),
                  pl.BlockSpec((1, S5_STATE_W, S5_W), lambda d, r: (d, 0, 0))],
        out_specs=pl.BlockSpec((1, S5_SEG, S5_W), lambda d, r: (d, r, 0)),
        out_shape=jax.ShapeDtypeStruct(y_tok.shape, F32),
        scratch_shapes=[pltpu.VMEM((SCAN_ROWS, 1, S5_STATE_W), F32)],
        input_output_aliases={0: 0},
        compiler_params=_cparams(("arbitrary", "arbitrary")),
        name="s5_fix",
    )(y_tok, ptab, ht_loc, wc)


def _qk(q, k):
    return lax.dot_general(q, k, (((1,), (1,)), ((), ())), preferred_element_type=F32)


def _softmax_pv(s_list, v_list):
    m = s_list[0].max(axis=-1, keepdims=True)
    for s in s_list[1:]:
        m = jnp.maximum(m, s.max(axis=-1, keepdims=True))
    den = None
    acc = None
    for s, v in zip(s_list, v_list):
        e = jnp.exp(s - m)
        ds = e.sum(axis=-1, keepdims=True)
        pv = jnp.dot(e.astype(BF16), v, preferred_element_type=F32)
        den = ds if den is None else den + ds
        acc = pv if acc is None else acc + pv
    return acc / den


def _ctx_attn_kernel(nq_ref, nk_ref, nv_ref, mq_ref, mk_ref, mv_ref, ckv_ref, fk_in, fv_in, fm_in,
                     na_ref, mla_ref, fk_ref, fv_ref, fm_ref):
    del fk_in, fv_in, fm_in
    fk_ref[0, 0] = nk_ref[0]
    fv_ref[0, 0] = nv_ref[0]
    fm_ref[0, 0] = ckv_ref[...]
    for hh in range(NA_H):
        s = _qk(nq_ref[0, hh], nk_ref[0, hh].astype(BF16)) * NA_SCALE
        na_ref[:, hh * NA_D:(hh + 1) * NA_D] = _softmax_pv([s], [nv_ref[0, hh].astype(BF16)])
    for hh in range(MLA_H):
        s = _qk(mq_ref[0, hh], mk_ref[0, hh]) * MLA_SCALE
        mla_ref[:, hh * MLA_V:(hh + 1) * MLA_V] = _softmax_pv([s], [mv_ref[0, hh]])


def _ctx_attention(naq, nak, nav, mq, mk, mv, ckv, fin_k, fin_v, fin_m, layer):
    head4 = lambda i: (i, 0, 0, 0)
    tile2 = lambda i: (i, 0)
    anyspec = pl.BlockSpec(memory_space=pl.ANY)
    return pl.pallas_call(
        _ctx_attn_kernel,
        grid=(CTX_TILES,),
        in_specs=[pl.BlockSpec((1, NA_H, TM, NA_D), head4)] * 3
                 + [pl.BlockSpec((1, MLA_H, TM, MLA_PAD), head4)] * 2
                 + [pl.BlockSpec((1, MLA_H, TM, MLA_V), head4),
                    pl.BlockSpec((TM, MLA_KVL + MLA_ROPE), tile2), anyspec, anyspec, anyspec],
        out_specs=[pl.BlockSpec((TM, NA_H * NA_D), tile2), pl.BlockSpec((TM, MLA_H * MLA_V), tile2),
                   pl.BlockSpec((1, 1, NA_H, TM, NA_D), lambda i: (i, layer, 0, 0, 0)),
                   pl.BlockSpec((1, 1, NA_H, TM, NA_D), lambda i: (i, layer, 0, 0, 0)),
                   pl.BlockSpec((1, 1, TM, MLA_KVL + MLA_ROPE), lambda i: (i, layer, 0, 0))],
        out_shape=[jax.ShapeDtypeStruct((T_CTX, NA_H * NA_D), F32),
                   jax.ShapeDtypeStruct((T_CTX, MLA_H * MLA_V), F32),
                   jax.ShapeDtypeStruct(fin_k.shape, F32),
                   jax.ShapeDtypeStruct(fin_v.shape, F32),
                   jax.ShapeDtypeStruct(fin_m.shape, F32)],
        input_output_aliases={7: 2, 8: 3, 9: 4},
        compiler_params=_cparams(("arbitrary",)),
        name="ctx_attn",
    )(naq, nak, nav, mq, mk, mv, ckv, fin_k, fin_v, fin_m)


def _na_lat_kernel(q_ref, k0_ref, k1_ref, k2_ref, v0_ref, v1_ref, v2_ref, kc_ref, vc_ref, b_ref, o_ref):
    for hh in range(NA_H):
        q = q_ref[0, hh]
        kb = jnp.concatenate([k0_ref[0, hh], k1_ref[0, hh], k2_ref[0, hh]], axis=0).astype(BF16)
        vb = jnp.concatenate([v0_ref[0, hh], v1_ref[0, hh], v2_ref[0, hh]], axis=0).astype(BF16)
        s_win = _qk(q, kb) * NA_SCALE + b_ref[0, 0, hh]
        s_ctx = _qk(q, kc_ref[0, 0, hh].astype(BF16)) * NA_SCALE
        o_ref[:, hh * NA_D:(hh + 1) * NA_D] = _softmax_pv([s_win, s_ctx], [vb, vc_ref[0, 0, hh].astype(BF16)])


def _band_start(j):
    return jnp.clip(j - 1, 0, LAT_TILES - 3)


def _na_latent(naq, nak, nav, cache_k, cache_v, bias, layer):
    def qmap(n, j):
        return (CTX_TILES + n * LAT_TILES + j, 0, 0, 0)

    def kmap(off):
        return lambda n, j: (CTX_TILES + n * LAT_TILES + _band_start(j) + off, 0, 0, 0)

    def bmap(n, j):
        return (layer, jnp.where(j == 0, 0, jnp.where(j == LAT_TILES - 1, 2, 1)), 0, 0, 0)

    blk = (1, NA_H, TM, NA_D)
    return pl.pallas_call(
        _na_lat_kernel,
        grid=(N_LAT, LAT_TILES),
        in_specs=[pl.BlockSpec(blk, qmap)]
                 + [pl.BlockSpec(blk, kmap(o)) for o in range(3)]
                 + [pl.BlockSpec(blk, kmap(o)) for o in range(3)]
                 + [pl.BlockSpec((1, 1, NA_H, PAST, NA_D), lambda n, j: (n, layer, 0, 0, 0))] * 2
                 + [pl.BlockSpec((1, 1, NA_H, TM, 3 * TM), bmap)],
        out_specs=pl.BlockSpec((TM, NA_H * NA_D), lambda n, j: (n * LAT_TILES + j, 0)),
        out_shape=jax.ShapeDtypeStruct((T_LAT, NA_H * NA_D), F32),
        compiler_params=_cparams(("arbitrary", "arbitrary")),
        name="na_latent",
    )(naq, nak, nak, nak, nav, nav, nav, cache_k, cache_v, bias)


def _na_bias_tables(rpb_all):
    rows = L_LAT // GRID_W
    nr, ncol = 2 * WIN_H - 1, 2 * WIN_W - 1
    w = np.arange(GRID_W)
    cs = np.clip(w - WIN_W // 2, 0, GRID_W - WIN_W)
    cc = np.arange(GRID_W)
    col_ok = (cc[None, :] >= cs[:, None]) & (cc[None, :] < cs[:, None] + WIN_W)
    dc = cc[None, :] - w[:, None] + WIN_W - 1
    col_sel = (dc[:, :, None] == np.arange(ncol)) & col_ok[:, :, None]
    row_sel, oks = [], []
    for j, s in ((0, 0), (1, 0), (LAT_TILES - 1, LAT_TILES - 3)):
        r = 4 * j + np.arange(4)
        rs = np.clip(r - WIN_H // 2, 0, rows - WIN_H)
        krow = 4 * s + np.arange(12)
        row_ok = (krow[None, :] >= rs[:, None]) & (krow[None, :] < rs[:, None] + WIN_H)
        dr = krow[None, :] - r[:, None] + WIN_H - 1
        row_sel.append((dr[:, :, None] == np.arange(nr)) & row_ok[:, :, None])
        oks.append(row_ok[:, None, :, None] & col_ok[None, :, None, :])
    row_sel = jnp.asarray(np.stack(row_sel), F32)
    col_sel = jnp.asarray(col_sel, F32)
    mask = jnp.asarray(np.where(np.stack(oks), 0.0, NEG).reshape(3, TM, 3 * TM), F32)
    b = jnp.einsum('prka,lhab,wcb->lphrwkc', row_sel, rpb_all, col_sel, precision=lax.Precision.HIGHEST)
    return b.reshape(DEPTH, 3, NA_H, TM, 3 * TM) + mask[None, :, None]


def _mla_lat_kernel(q_ref, k_ref, v_ref, kc_ref, vc_ref, o_ref):
    for hh in range(MLA_H):
        q = q_ref[0, hh]
        k = k_ref[:, hh].reshape(L_LAT, MLA_PAD)
        v = v_ref[:, hh].reshape(L_LAT, MLA_V)
        s_lat = _qk(q, k) * MLA_SCALE
        s_ctx = _qk(q, kc_ref[0, 0, hh]) * MLA_SCALE
        o_ref[:, hh * MLA_V:(hh + 1) * MLA_V] = _softmax_pv([s_lat, s_ctx], [v, vc_ref[0, 0, hh]])


def _mla_latent(mq, mk, mv, kc, vc, layer):
    seq_blk = CTX_TILES // LAT_TILES
    return pl.pallas_call(
        _mla_lat_kernel,
        grid=(N_LAT, LAT_TILES),
        in_specs=[pl.BlockSpec((1, MLA_H, TM, MLA_PAD), lambda n, j: (CTX_TILES + n * LAT_TILES + j, 0, 0, 0)),
                  pl.BlockSpec((LAT_TILES, MLA_H, TM, MLA_PAD), lambda n, j: (seq_blk + n, 0, 0, 0)),
                  pl.BlockSpec((LAT_TILES, MLA_H, TM, MLA_V), lambda n, j: (seq_blk + n, 0, 0, 0)),
                  pl.BlockSpec((1, 1, MLA_H, PAST, MLA_PAD), lambda n, j: (n, layer, 0, 0, 0)),
                  pl.BlockSpec((1, 1, MLA_H, PAST, MLA_V), lambda n, j: (n, layer, 0, 0, 0))],
        out_specs=pl.BlockSpec((TM, MLA_H * MLA_V), lambda n, j: (n * LAT_TILES + j, 0)),
        out_shape=jax.ShapeDtypeStruct((T_LAT, MLA_H * MLA_V), F32),
        compiler_params=_cparams(("arbitrary", "arbitrary")),
        name="mla_latent",
    )(mq, mk, mv, kc, vc)


def _post_kernel(y_ref, u_ref, d_ref, gw_ref, gb_ref, nac_ref, nal_ref, mlac_ref, mlal_ref, wo_ref, x_ref, mod_ref,
                 g2_ref, rw_ref, rb_ref, x1_ref, h2_ref, te_ref, tg_ref, rk_ref, cnt_ref):
    ys = y_ref[0] + y_ref[1] + u_ref[...] * d_ref[...]
    g = jax.nn.gelu(ys)
    s5o = g * jax.nn.sigmoid(jnp.dot(g.astype(BF16), gw_ref[...], preferred_element_type=F32) + gb_ref[...])
    is_ctx = pl.program_id(0) < CTX_TILES
    na = jnp.where(is_ctx, nac_ref[...], nal_ref[...])
    mla = jnp.where(is_ctx, mlac_ref[...], mlal_ref[...])
    out = (jnp.dot(s5o.astype(BF16), wo_ref[0:S5_W, :], preferred_element_type=F32)
           + jnp.dot(na.astype(BF16), wo_ref[S5_W:S5_W + NA_H * NA_D, :], preferred_element_type=F32)
           + jnp.dot(mla.astype(BF16), wo_ref[S5_W + NA_H * NA_D:, :], preferred_element_type=F32))
    mod = mod_ref[0]
    x1 = x_ref[...] + mod[2:3, :] * out
    x1_ref[...] = x1
    h2 = _rms(x1, g2_ref[...]) * (1.0 + mod[4:5, :]) + mod[3:4, :]
    h2_ref[...] = h2
    h_hi = h2.astype(BF16)
    h_lo = (h2 - h_hi.astype(F32)).astype(BF16)
    logits = (jnp.dot(h_hi, rw_ref[0], preferred_element_type=F32)
              + jnp.dot(h_hi, rw_ref[1], preferred_element_type=F32)
              + jnp.dot(h_lo, rw_ref[0], preferred_element_type=F32)) + rb_ref[...]
    lane_i = lax.broadcasted_iota(jnp.int32, (TM, 128), 1)
    lane = lane_i.astype(F32)
    cur = jnp.where(lane_i < N_EXP, logits, -jnp.inf)
    te = jnp.zeros((TM, 128), F32)
    tv = jnp.zeros((TM, 128), F32)
    hot = jnp.zeros((TM, 128), F32)
    idxs = []
    top = None
    for k in range(TOP_K):
        m = cur.max(axis=-1, keepdims=True)
        idx = jnp.where(cur == m, lane, 128.0).min(axis=-1, keepdims=True)
        sel = lane == idx
        top = m if top is None else top
        idxs.append(idx)
        te = jnp.where(lane_i == k, idx, te)
        tv = jnp.where(lane_i == k, jnp.exp(m - top), tv)
        hot = jnp.where(sel, 1.0, hot)
        cur = jnp.where(sel, -jnp.inf, cur)
    te_ref[...] = te.astype(jnp.int32)
    tg_ref[...] = tv / tv.sum(axis=-1, keepdims=True)
    row = lax.broadcasted_iota(jnp.int32, (TM, TM), 0)
    col = lax.broadcasted_iota(jnp.int32, (TM, TM), 1)
    tri = jnp.where(col < row, 1.0, 0.0).astype(BF16)
    before = jnp.dot(tri, hot.astype(BF16), preferred_element_type=F32)
    rk = jnp.zeros((TM, 128), F32)
    for k in range(TOP_K):
        rk = jnp.where(lane_i == k, jnp.where(lane == idxs[k], before, 0.0).sum(axis=-1, keepdims=True), rk)
    rk_ref[...] = rk.astype(jnp.int32)
    cnt_ref[0] = jnp.broadcast_to(hot.sum(axis=0, keepdims=True), (8, 128)).astype(jnp.int32)


def _post(y2, u, s5_d, glu_w, glu_b, na_ctx, na_lat, mla_ctx, mla_lat, w_out, x, mod_l, g2, rw, rb):
    const2 = lambda i: (0, 0)
    tile2 = lambda i: (i, 0)
    ctx2 = lambda i: (jnp.minimum(i, CTX_TILES - 1), 0)
    lat2 = lambda i: (jnp.maximum(i - CTX_TILES, 0), 0)
    return pl.pallas_call(
        _post_kernel,
        grid=(N_TILES,),
        in_specs=[pl.BlockSpec((2, TM, S5_W), lambda i: (0, i, 0)),
                  pl.BlockSpec((TM, S5_W), tile2),
                  pl.BlockSpec((1, S5_W), const2),
                  pl.BlockSpec((S5_W, S5_W), const2),
                  pl.BlockSpec((1, S5_W), const2),
                  pl.BlockSpec((TM, NA_H * NA_D), ctx2),
                  pl.BlockSpec((TM, NA_H * NA_D), lat2),
                  pl.BlockSpec((TM, MLA_H * MLA_V), ctx2),
                  pl.BlockSpec((TM, MLA_H * MLA_V), lat2),
                  pl.BlockSpec((D_MODEL, D_MODEL), const2),
                  pl.BlockSpec((TM, D_MODEL), tile2),
                  pl.BlockSpec((1, 6, D_MODEL), lambda i: (_mod_row(i), 0, 0)),
                  pl.BlockSpec((1, D_MODEL), const2),
                  pl.BlockSpec((2, D_MODEL, 128), lambda i: (0, 0, 0)),
                  pl.BlockSpec((1, 128), const2)],
        out_specs=[pl.BlockSpec((TM, D_MODEL), tile2),
                   pl.BlockSpec((TM, D_MODEL), tile2),
                   pl.BlockSpec((TM, 128), tile2),
                   pl.BlockSpec((TM, 128), tile2),
                   pl.BlockSpec((TM, 128), tile2),
                   pl.BlockSpec((1, 8, 128), lambda i: (i, 0, 0))],
        out_shape=[jax.ShapeDtypeStruct((T_ALL, D_MODEL), F32),
                   jax.ShapeDtypeStruct((T_ALL, D_MODEL), F32),
                   jax.ShapeDtypeStruct((T_ALL, 128), jnp.int32),
                   jax.ShapeDtypeStruct((T_ALL, 128), F32),
                   jax.ShapeDtypeStruct((T_ALL, 128), jnp.int32),
                   jax.ShapeDtypeStruct((N_TILES, 8, 128), jnp.int32)],
        compiler_params=_cparams(("arbitrary",)),
        name="post",
    )(y2, u, s5_d, glu_w, glu_b, na_ctx, na_lat, mla_ctx, mla_lat, w_out, x, mod_l, g2, rw, rb)


def _moe_fused_kernel(be_ref, first_ref, nused_ref, pos_ref,
                      dflt_hbm, h_hbm, wgu_ref, bgu_ref, wd_ref, bd_ref, ys_hbm,
                      asg_s, xs0, xs1, yo0, yo1, wgu_s, wd_s, act_s, gsem, ssem, tsem):
    b = pl.program_id(0)
    nused = nused_ref[0]
    xs = (xs0, xs1)
    yo = (yo0, yo1)

    def wait_gather(p):
        pltpu.make_async_copy(h_hbm.at[pl.ds(0, MOE_BM), :], xs[p], gsem.at[p]).wait()

    def wait_scatter(p):
        pltpu.make_async_copy(yo[p], ys_hbm.at[pl.ds(0, MOE_BM), :], ssem.at[p]).wait()

    def issue_gather(blk, p):
        r0 = (blk + 1) * MOE_BM
        for j in range(MOE_BM):
            tok = asg_s[r0 + j] >> 16
            pltpu.make_async_copy(h_hbm.at[pl.ds(tok, 1), :], xs[p].at[pl.ds(j, 1), :],
                                  gsem.at[p]).start(priority=j % 2)

    def issue_scatter(blk, p, lo=0, hi=MOE_BM):
        r0 = (blk + 1) * MOE_BM
        for j in range(lo, hi):
            d = asg_s[r0 + j] & 0xFFFF
            pltpu.make_async_copy(yo[p].at[pl.ds(j, 1), :], ys_hbm.at[pl.ds(d, 1), :],
                                  ssem.at[p]).start(priority=j % 2)

    @pl.when(b == 0)
    def _():
        cp = pltpu.make_async_copy(dflt_hbm, asg_s, tsem)
        cp.start()
        cp.wait()

        def inv(i, carry):
            a0 = i * 8
            v0 = a0 | ((((a0 >> 10) << 8) + (a0 & (TM - 1))) << 16)
            for uu in range(8):
                asg_s[pos_ref[a0 + uu]] = v0 + uu * ((1 << 16) + 1)
            return carry
        lax.fori_loop(0, MOE_SLOTS // 8, inv, 0)
        yo1[...] = jnp.zeros_like(yo1)
        zc = pltpu.make_async_copy(yo1, ys_hbm.at[pl.ds(MOE_SLOTS, MOE_BM), :], tsem)
        zc.start()
        zc.wait()
        issue_gather(0, 0)

    def step(p):
        q = 1 - p
        wait_gather(p)
        if p == 0:
            @pl.when(b >= 1)
            def _():
                wait_scatter(p)
        else:
            wait_scatter(p)

        @pl.when(first_ref[b] == 1)
        def _():
            wgu_s[...] = wgu_ref[0, 0].astype(BF16)
            wd_s[...] = wd_ref[0, 0].astype(BF16)

        issue_gather(jnp.minimum(b + 1, nused - 1), q)
        issue_scatter(b - 1, q, 0, MOE_BM // 2)
        gu = jnp.dot(xs[p][...].astype(BF16), wgu_s[...], preferred_element_type=F32) + bgu_ref[0, 0]
        gate = jnp.minimum(gu[:, :D_FF], SWIGLU_LIMIT)
        up = jnp.clip(gu[:, D_FF:], -SWIGLU_LIMIT, SWIGLU_LIMIT)
        act_s[...] = ((up + 1.0) * (gate * jax.nn.sigmoid(SWIGLU_ALPHA * gate))).astype(BF16)

        @pl.when(b >= 0)
        def _():
            issue_scatter(b - 1, q, MOE_BM // 2, MOE_BM)
            yo[p][...] = jnp.dot(act_s[...], wd_s[...], preferred_element_type=F32) + bd_ref[0, 0]

    def drain(p):
        q = 1 - p
        wait_gather(p)
        wait_scatter(p)
        issue_scatter(nused - 1, q)
        wait_scatter(q)

    for p in range(2):
        @pl.when((b < nused) & (b % 2 == p))
        def _():
            step(p)

        @pl.when((b == nused) & (b % 2 == p))
        def _():
            drain(p)


def _moe_fused(block_e, first, nused, pos_km, h2, w_gu, b_gu, w_down, b_down, layer):
    def wmap(b, be, fi, nu, pos):
        return (layer, be[b], 0, 0)

    r = jnp.arange(ASG_ROWS, dtype=jnp.int32)
    dflt = MOE_SLOTS + (((r >> 8) + 1) & 1) * MOE_BM + (r & (MOE_BM - 1))

    return pl.pallas_call(
        _moe_fused_kernel,
        grid_spec=pltpu.PrefetchScalarGridSpec(
            num_scalar_prefetch=4, grid=(MOE_BLOCKS + 1,),
            in_specs=[pl.BlockSpec(memory_space=pl.ANY), pl.BlockSpec(memory_space=pl.ANY),
                      pl.BlockSpec((1, 1, D_MODEL, 2 * D_FF), wmap),
                      pl.BlockSpec((1, 1, 1, 2 * D_FF), wmap),
                      pl.BlockSpec((1, 1, D_FF, D_MODEL), wmap),
                      pl.BlockSpec((1, 1, 1, D_MODEL), wmap)],
            out_specs=pl.BlockSpec(memory_space=pl.ANY),
            scratch_shapes=[pltpu.SMEM((ASG_ROWS,), jnp.int32),
                            pltpu.VMEM((MOE_BM, D_MODEL), F32), pltpu.VMEM((MOE_BM, D_MODEL), F32),
                            pltpu.VMEM((MOE_BM, D_MODEL), F32), pltpu.VMEM((MOE_BM, D_MODEL), F32),
                            pltpu.VMEM((D_MODEL, 2 * D_FF), BF16), pltpu.VMEM((D_FF, D_MODEL), BF16),
                            pltpu.VMEM((MOE_BM, D_FF), BF16),
                            pltpu.SemaphoreType.DMA((2,)), pltpu.SemaphoreType.DMA((2,)),
                            pltpu.SemaphoreType.DMA(())]),
        out_shape=jax.ShapeDtypeStruct((MOE_SLOTS + 2 * MOE_BM, D_MODEL), F32),
        compiler_params=_cparams(("arbitrary",)),
        name="moe_fused",
    )(block_e, first, nused, pos_km, dflt, h2, w_gu, b_gu.reshape(DEPTH, N_EXP, 1, 2 * D_FF), w_down,
      b_down.reshape(DEPTH, N_EXP, 1, D_MODEL))


def _combine_kernel(ys_ref, tg_ref, x1_ref, mod_ref, o_ref):
    tg = tg_ref[...]
    y = ys_ref[0:TM, :] * tg[:, 0:1]
    for k in range(1, TOP_K):
        y = y + ys_ref[k * TM:(k + 1) * TM, :] * tg[:, k:k + 1]
    o_ref[...] = x1_ref[...] + mod_ref[0][5:6, :] * y


def _combine(ys, tg, x1, mod_l):
    return pl.pallas_call(
        _combine_kernel,
        grid=(N_TILES,),
        in_specs=[pl.BlockSpec((TOP_K * TM, D_MODEL), lambda i: (i, 0)),
                  pl.BlockSpec((TM, 128), lambda i: (i, 0)),
                  pl.BlockSpec((TM, D_MODEL), lambda i: (i, 0)),
                  pl.BlockSpec((1, 6, D_MODEL), lambda i: (_mod_row(i), 0, 0))],
        out_specs=pl.BlockSpec((TM, D_MODEL), lambda i: (i, 0)),
        out_shape=jax.ShapeDtypeStruct((T_ALL, D_MODEL), F32),
        compiler_params=_cparams(("arbitrary",)),
        name="moe_combine",
    )(ys, tg, x1, mod_l)


def _moe_plan(te, rk, cnt):
    tile_cnt = cnt[:, 0, :N_EXP]
    counts = tile_cnt.sum(axis=0)
    padded = (counts + MOE_BM - 1) // MOE_BM * MOE_BM
    pad_end = jnp.cumsum(padded)
    pad_start = pad_end - padded
    tile_off = jnp.cumsum(tile_cnt, axis=0) - tile_cnt
    base = (pad_start[None, :] + tile_off).astype(jnp.int32)
    e = te[:, :TOP_K].reshape(N_TILES, TM, TOP_K)
    hot = e[..., None] == jnp.arange(N_EXP, dtype=jnp.int32)
    pos = jnp.sum(jnp.where(hot, base[:, None, None, :], 0), axis=-1) + rk[:, :TOP_K].reshape(N_TILES, TM, TOP_K)
    pos_km = (pos.transpose(0, 2, 1).reshape(-1) + MOE_BM).astype(jnp.int32)
    nused = (pad_end[-1] // MOE_BM).astype(jnp.int32)
    blk = jnp.arange(MOE_BLOCKS + 1, dtype=jnp.int32)
    be = jnp.minimum(jnp.sum((pad_end[None, :] <= (blk * MOE_BM)[:, None]).astype(jnp.int32), axis=1), N_EXP - 1)
    be = jnp.where(blk < nused, be, be[nused - 1])
    first = jnp.concatenate([jnp.ones((1,), jnp.int32), (be[1:] != be[:-1]).astype(jnp.int32)])
    return pos_km, be, first, nused.reshape(1)


def _final_kernel(x_ref, g_ref, o_ref):
    o_ref[...] = _rms(x_ref[...], g_ref[...])


def _final_norm(x, g):
    return pl.pallas_call(
        _final_kernel,
        grid=(N_TILES,),
        in_specs=[pl.BlockSpec((TM, D_MODEL), lambda i: (i, 0)), pl.BlockSpec((1, D_MODEL), lambda i: (0, 0))],
        out_specs=pl.BlockSpec((TM, D_MODEL), lambda i: (i, 0)),
        out_shape=jax.ShapeDtypeStruct((T_ALL, D_MODEL), F32),
        compiler_params=_cparams(("arbitrary",)),
        name="final_norm",
    )(x, g)


_ROPE_PERM = np.concatenate([np.arange(8, 16), np.arange(0, 8), np.arange(24, 32), np.arange(16, 24)])


def _rope_tables():
    half = MLA_ROPE // 2
    t = jnp.arange(L_LAT)
    row = (t // GRID_W).astype(F32)
    col = (t % GRID_W).astype(F32)
    inv = ROPE_BASE ** (-jnp.arange(0, half, 2, dtype=F32) / half)

    def part(pos):
        ang = pos[:, None] * inv[None, :]
        c, s = jnp.cos(ang), jnp.sin(ang)
        return jnp.concatenate([c, c], axis=-1), jnp.concatenate([-s, s], axis=-1)

    cr, sr = part(row)
    cc, sc = part(col)
    cos32 = jnp.concatenate([cr, cc], axis=-1)
    sin32 = jnp.concatenate([sr, sc], axis=-1)
    ck = jnp.concatenate([jnp.ones((TM, MLA_ROPE), F32), cos32], axis=0)
    sk = jnp.concatenate([jnp.zeros((TM, MLA_ROPE), F32), sin32], axis=0)
    padw = MLA_PAD - MLA_NOPE - MLA_ROPE
    cq = jnp.concatenate([jnp.ones((TM + L_LAT, MLA_NOPE), F32), ck, jnp.ones((TM + L_LAT, padw), F32)], axis=-1)
    sq = jnp.concatenate([jnp.zeros((TM + L_LAT, MLA_NOPE), F32), sk, jnp.zeros((TM + L_LAT, padw), F32)], axis=-1)
    return cq, sq, ck, sk


def _mla_weights(w_qb, w_kvb):
    dq = MLA_NOPE + MLA_ROPE
    wq = w_qb.reshape(MLA_QL, MLA_H, dq)
    zpad = jnp.zeros((MLA_QL, MLA_H, MLA_PAD - dq), F32)
    q_main = jnp.concatenate([wq, zpad], axis=-1)
    q_part = jnp.concatenate([jnp.zeros((MLA_QL, MLA_H, MLA_NOPE), F32), wq[:, :, MLA_NOPE + _ROPE_PERM], zpad], axis=-1)
    wq2 = jnp.concatenate([q_main.reshape(MLA_QL, -1), q_part.reshape(MLA_QL, -1)], axis=-1).astype(BF16)
    wkv = w_kvb.reshape(MLA_KVL, MLA_H, MLA_NOPE + MLA_V)
    k_top = jnp.concatenate([wkv[:, :, :MLA_NOPE], jnp.zeros((MLA_KVL, MLA_H, MLA_PAD - MLA_NOPE), F32)], axis=-1)
    place = jnp.concatenate([jnp.zeros((MLA_ROPE, MLA_NOPE), F32), jnp.eye(MLA_ROPE, dtype=F32),
                             jnp.zeros((MLA_ROPE, MLA_PAD - dq), F32)], axis=-1)
    k_bot = jnp.broadcast_to(place[:, None, :], (MLA_ROPE, MLA_H, MLA_PAD))
    wkk = jnp.concatenate([k_top, k_bot], axis=0).reshape(MLA_KVL + MLA_ROPE, MLA_H * MLA_PAD).astype(BF16)
    wv = wkv[:, :, MLA_NOPE:].reshape(MLA_KVL, MLA_H * MLA_V).astype(BF16)
    return wq2, wkk, wv


def kernel(x_prompt, x_sample, cache_na_k, cache_na_v, cache_mla_kv, state_s5_re, state_s5_im, c, c_ctx, w_mod, b_mod, norm1_g, norm2_g, w_in, w_out, s5_lambda_re, s5_lambda_im, s5_log_dt, s5_b_re, s5_b_im, s5_c_re, s5_c_im, s5_d, s5_glu_w, s5_glu_b, na_rpb, mla_q_norm_g, mla_w_qb, mla_kv_norm_g, mla_w_kvb, router_w, router_b, moe_w_gu, moe_b_gu, moe_w_down, moe_b_down, final_norm_g):
    x = jnp.concatenate([x_prompt.reshape(T_CTX, D_MODEL), x_sample.reshape(T_LAT, D_MODEL)], axis=0)
    cv = jnp.concatenate([c_ctx[None, :], c, jnp.zeros((MOD_ROWS - 1 - N_LAT, D_MODEL), F32)], axis=0)
    mod = _modulation(cv, w_mod, b_mod).reshape(DEPTH, MOD_ROWS, 6, D_MODEL)
    cq, sq, ck, sk = _rope_tables()
    mla_w = [_mla_weights(mla_w_qb[l], mla_w_kvb[l]) for l in range(DEPTH)]
    kc_all, vc_all = _cache_kv(cache_mla_kv, jnp.stack([m[1] for m in mla_w]), jnp.stack([m[2] for m in mla_w]))
    gp = S5_G * S5_P
    na_bias = _na_bias_tables(na_rpb)
    fin_k = jnp.zeros((N_CTX, DEPTH, NA_H, L_CTX, NA_D), F32)
    fin_v = jnp.zeros((N_CTX, DEPTH, NA_H, L_CTX, NA_D), F32)
    fin_m = jnp.zeros((N_CTX, DEPTH, L_CTX, MLA_KVL + MLA_ROPE), F32)

    s5_re_list, s5_im_list = [], []
    for l in range(DEPTH):
        mod_l = mod[l]
        wq2, wkk, wv = mla_w[l]
        w_in_ext = jnp.concatenate([w_in[l], w_in[l][:, OFF_KVA + MLA_KVL + _ROPE_PERM]], axis=-1).astype(BF16)
        u, naq, nak, nav, mq, mk, mv, ckv = _project(
            x, mod_l, norm1_g[l][None], w_in_ext, mla_q_norm_g[l][None], wq2, mla_kv_norm_g[l][None], wkk, wv,
            cq, sq, ck, sk)

        a, bb = _s5_discretize(s5_lambda_re[l], s5_lambda_im[l], s5_log_dt[l], s5_b_re[l], s5_b_im[l])
        a_lanes, wb, wc = _s5_matrices(a, bb, s5_c_re[l], s5_c_im[l])
        u_ctx = u[:T_CTX].reshape(N_CTX, L_CTX, S5_W).transpose(1, 0, 2)
        y_ctx, ht_ctx = _s5_scan(u_ctx, wb, wc, a_lanes, jnp.zeros((2, N_CTX, S5_STATE_W), F32))
        u_lat = u[T_CTX:].reshape(SCAN_ROWS, S5_SEG, S5_W).transpose(1, 0, 2)
        h0 = jnp.concatenate([state_s5_re[:, l].reshape(N_LAT, 2, gp), state_s5_im[:, l].reshape(N_LAT, 2, gp)], axis=-1)
        h0z = jnp.zeros((N_LAT, N_SEG - 1, S5_STATE_W), F32)
        h0 = jnp.stack([jnp.concatenate([h0[:, 0:1], h0z], axis=1).reshape(SCAN_ROWS, S5_STATE_W),
                        jnp.concatenate([h0z, h0[:, 1:2]], axis=1).reshape(SCAN_ROWS, S5_STATE_W)], axis=0)
        y_loc, ht_loc = _s5_scan(u_lat, wb, wc, a_lanes, h0)
        ptab = _s5_powers(a_lanes)
        ptab = jnp.concatenate([ptab[0:1], jnp.flip(ptab[1:2], axis=2)], axis=0)
        y_lat = _s5_fix(y_loc.transpose(0, 2, 1, 3).reshape(2, T_LAT, S5_W), ptab, ht_loc, wc)
        y2 = jnp.concatenate([y_ctx.transpose(0, 2, 1, 3).reshape(2, T_CTX, S5_W), y_lat], axis=1)
        st = ht_ctx.reshape(2, N_CTX, 2, S5_G, S5_P).transpose(1, 0, 2, 3, 4)
        s5_re_list.append(st[:, :, 0])
        s5_im_list.append(st[:, :, 1])

        na_ctx, mla_ctx, fin_k, fin_v, fin_m = _ctx_attention(naq, nak, nav, mq, mk, mv, ckv, fin_k, fin_v, fin_m, l)
        na_lat = _na_latent(naq, nak, nav, cache_na_k, cache_na_v, na_bias, l)
        mla_lat = _mla_latent(mq, mk, mv, kc_all, vc_all, l)

        rw = jnp.pad(router_w[l], ((0, 0), (0, 128 - N_EXP)))
        rw_hi = rw.astype(BF16)
        rw = jnp.stack([rw_hi, (rw - rw_hi.astype(F32)).astype(BF16)], axis=0)
        rb = jnp.pad(router_b[l], (0, 128 - N_EXP))[None]
        x1, h2, te, tg, rk, cnt = _post(y2, u, s5_d[l][None], s5_glu_w[l].astype(BF16), s5_glu_b[l][None],
                                        na_ctx, na_lat, mla_ctx, mla_lat, w_out[l].astype(BF16), x, mod_l,
                                        norm2_g[l][None], rw, rb)
        pos_km, be, first, nused = _moe_plan(te, rk, cnt)
        ys = _moe_fused(be, first, nused, pos_km, h2, moe_w_gu, moe_b_gu, moe_w_down, moe_b_down, l)
        x = _combine(ys, tg, x1, mod_l)

    y = _final_norm(x, final_norm_g[None])
    y_prompt = y[:T_CTX].reshape(N_CTX, L_CTX, D_MODEL)
    y_sample = y[T_CTX:].reshape(N_LAT, L_LAT, D_MODEL)
    return (y_prompt, y_sample, fin_k, fin_v, fin_m, jnp.stack(s5_re_list, axis=1), jnp.stack(s5_im_list, axis=1))
```

```python
import functools
import math

import numpy as np
import jax
import jax.numpy as jnp
from jax import lax
from jax.experimental import pallas as pl
from jax.experimental.pallas import tpu as pltpu

F32 = jnp.float32
BF16 = jnp.bfloat16

D_MODEL = 1024
N_CTX, L_CTX = 32, 256
N_LAT, L_LAT = 2, 2048
DEPTH = 4
PAST = 256
GRID_W = 64
S5_W, S5_G, S5_P, S5_C = 256, 16, 64, 16
NA_H, NA_D = 4, 64
WIN_H, WIN_W = 8, 16
MLA_H, MLA_NOPE, MLA_ROPE, MLA_V = 8, 64, 32, 64
MLA_QL, MLA_KVL = 256, 128
MLA_SCALE = (MLA_NOPE + MLA_ROPE) ** -0.5
NA_SCALE = NA_D ** -0.5
OFF_NA = S5_W
OFF_QA = OFF_NA + 3 * NA_H * NA_D
OFF_KVA = OFF_QA + MLA_QL
D_IN = OFF_KVA + MLA_KVL + MLA_ROPE
N_EXP, TOP_K, D_FF = 32, 4, 1024
SWIGLU_ALPHA, SWIGLU_LIMIT = 1.702, 7.0
ROPE_BASE = 10000.0
EPS = 1e-6

TM = 256
T_CTX = N_CTX * L_CTX
T_LAT = N_LAT * L_LAT
T_ALL = T_CTX + T_LAT
CTX_TILES = T_CTX // TM
LAT_TILES = L_LAT // TM
N_TILES = T_ALL // TM
MOD_ROWS = 8
MLA_PAD = 128
D_IN_EXT = D_IN + MLA_ROPE
SCAN_ROWS = 8
SCAN_CH = 64
S5_STATE_W = 2 * S5_G * S5_P
N_SEG = SCAN_ROWS // N_LAT
S5_SEG = L_LAT // N_SEG
MOE_BM = 256
MOE_CHUNKS = 4
MOE_SLOTS = T_ALL * TOP_K
MOE_ROWS = MOE_SLOTS + N_EXP * MOE_BM
MOE_BLOCKS = MOE_ROWS // MOE_BM
ASG_ROWS = 57 * 1024
NEG = -1e30
VMEM_LIMIT = 56 * 1024 * 1024


def _cparams(sem):
    return pltpu.CompilerParams(dimension_semantics=sem, vmem_limit_bytes=VMEM_LIMIT)


def _mod_row(i):
    return jnp.maximum(i - (CTX_TILES - LAT_TILES), 0) // LAT_TILES


def _rope_block(i):
    return jnp.where(i < CTX_TILES, 0, 1 + (i - CTX_TILES) % LAT_TILES)


def _rms(x, g):
    return x * lax.rsqrt(jnp.mean(x * x, axis=-1, keepdims=True) + EPS) * g


def _mod_kernel(cv_ref, w_ref, b_ref, o_ref):
    s = jax.nn.silu(cv_ref[...]).astype(BF16)
    o_ref[0] = jnp.dot(s, w_ref[0].astype(BF16), preferred_element_type=F32) + b_ref[0]


def _modulation(cv, w_mod, b_mod):
    tn = 1536
    return pl.pallas_call(
        _mod_kernel,
        grid=(DEPTH, 6 * D_MODEL // tn),
        in_specs=[pl.BlockSpec((MOD_ROWS, D_MODEL), lambda l, j: (0, 0)),
                  pl.BlockSpec((1, D_MODEL, tn), lambda l, j: (l, 0, j)),
                  pl.BlockSpec((1, 1, tn), lambda l, j: (l, 0, j))],
        out_specs=pl.BlockSpec((1, MOD_ROWS, tn), lambda l, j: (l, 0, j)),
        out_shape=jax.ShapeDtypeStruct((DEPTH, MOD_ROWS, 6 * D_MODEL), F32),
        compiler_params=_cparams(("arbitrary", "arbitrary")),
        name="modulation",
    )(cv, w_mod, b_mod.reshape(DEPTH, 1, 6 * D_MODEL))


def _proj_kernel(x_ref, mod_ref, g1_ref, win_ref, qg_ref, wq_ref, kvg_ref, wkk_ref, wv_ref,
                 cq_ref, sq_ref, ck_ref, sk_ref,
                 u_ref, naq_ref, nak_ref, nav_ref, mq_ref, mk_ref, mv_ref, ckv_ref):
    x = x_ref[...]
    mod = mod_ref[0]
    h = _rms(x, g1_ref[...]) * (1.0 + mod[1:2, :]) + mod[0:1, :]
    p = jnp.dot(h.astype(BF16), win_ref[...], preferred_element_type=F32)
    u_ref[...] = p[:, :OFF_NA]
    for hh in range(NA_H):
        naq_ref[0, hh] = p[:, OFF_NA + hh * NA_D: OFF_NA + (hh + 1) * NA_D].astype(BF16)
        nak_ref[0, hh] = p[:, OFF_NA + NA_H * NA_D + hh * NA_D: OFF_NA + NA_H * NA_D + (hh + 1) * NA_D]
        nav_ref[0, hh] = p[:, OFF_NA + 2 * NA_H * NA_D + hh * NA_D: OFF_NA + 2 * NA_H * NA_D + (hh + 1) * NA_D]
    qn = _rms(p[:, OFF_QA:OFF_KVA], qg_ref[...]).astype(BF16)
    qq = jnp.dot(qn, wq_ref[...], preferred_element_type=F32)
    cq = cq_ref[...]
    sq = sq_ref[...]
    for hh in range(MLA_H):
        qa = qq[:, hh * MLA_PAD:(hh + 1) * MLA_PAD]
        qb = qq[:, (MLA_H + hh) * MLA_PAD:(MLA_H + hh + 1) * MLA_PAD]
        mq_ref[0, hh] = (qa * cq + qb * sq).astype(BF16)
    ckv = _rms(p[:, OFF_KVA:OFF_KVA + MLA_KVL], kvg_ref[...])
    kr = p[:, OFF_KVA + MLA_KVL:D_IN]
    kr_partner = p[:, D_IN:D_IN_EXT]
    kr_rot = kr * ck_ref[...] + kr_partner * sk_ref[...]
    ckv_ref[...] = jnp.concatenate([ckv, kr], axis=-1)
    kin = jnp.concatenate([ckv, kr_rot], axis=-1).astype(BF16)
    kk = jnp.dot(kin, wkk_ref[...], preferred_element_type=F32)
    vv = jnp.dot(kin[:, :MLA_KVL], wv_ref[...], preferred_element_type=F32)
    for hh in range(MLA_H):
        mk_ref[0, hh] = kk[:, hh * MLA_PAD:(hh + 1) * MLA_PAD].astype(BF16)
        mv_ref[0, hh] = vv[:, hh * MLA_V:(hh + 1) * MLA_V].astype(BF16)


def _project(x, mod_l, g1, w_in_ext, qg, wq2, kvg, wkk, wv, cq, sq, ck, sk):
    const2 = lambda i: (0, 0)
    tile2 = lambda i: (i, 0)
    head4 = lambda i: (i, 0, 0, 0)
    rope2 = lambda i: (_rope_block(i), 0)
    return pl.pallas_call(
        _proj_kernel,
        grid=(N_TILES,),
        in_specs=[pl.BlockSpec((TM, D_MODEL), tile2),
                  pl.BlockSpec((1, 6, D_MODEL), lambda i: (_mod_row(i), 0, 0)),
                  pl.BlockSpec((1, D_MODEL), const2),
                  pl.BlockSpec((D_MODEL, D_IN_EXT), const2),
                  pl.BlockSpec((1, MLA_QL), const2),
                  pl.BlockSpec((MLA_QL, 2 * MLA_H * MLA_PAD), const2),
                  pl.BlockSpec((1, MLA_KVL), const2),
                  pl.BlockSpec((MLA_KVL + MLA_ROPE, MLA_H * MLA_PAD), const2),
                  pl.BlockSpec((MLA_KVL, MLA_H * MLA_V), const2),
                  pl.BlockSpec((TM, MLA_PAD), rope2),
                  pl.BlockSpec((TM, MLA_PAD), rope2),
                  pl.BlockSpec((TM, MLA_ROPE), rope2),
                  pl.BlockSpec((TM, MLA_ROPE), rope2)],
        out_specs=[pl.BlockSpec((TM, S5_W), tile2),
                   pl.BlockSpec((1, NA_H, TM, NA_D), head4),
                   pl.BlockSpec((1, NA_H, TM, NA_D), head4),
                   pl.BlockSpec((1, NA_H, TM, NA_D), head4),
                   pl.BlockSpec((1, MLA_H, TM, MLA_PAD), head4),
                   pl.BlockSpec((1, MLA_H, TM, MLA_PAD), head4),
                   pl.BlockSpec((1, MLA_H, TM, MLA_V), head4),
                   pl.BlockSpec((TM, MLA_KVL + MLA_ROPE), tile2)],
        out_shape=[jax.ShapeDtypeStruct((T_ALL, S5_W), F32),
                   jax.ShapeDtypeStruct((N_TILES, NA_H, TM, NA_D), BF16),
                   jax.ShapeDtypeStruct((N_TILES, NA_H, TM, NA_D), F32),
                   jax.ShapeDtypeStruct((N_TILES, NA_H, TM, NA_D), F32),
                   jax.ShapeDtypeStruct((N_TILES, MLA_H, TM, MLA_PAD), BF16),
                   jax.ShapeDtypeStruct((N_TILES, MLA_H, TM, MLA_PAD), BF16),
                   jax.ShapeDtypeStruct((N_TILES, MLA_H, TM, MLA_V), BF16),
                   jax.ShapeDtypeStruct((T_ALL, MLA_KVL + MLA_ROPE), F32)],
        compiler_params=_cparams(("arbitrary",)),
        name="proj",
    )(x, mod_l, g1, w_in_ext, qg, wq2, kvg, wkk, wv, cq, sq, ck, sk)


def _cache_kv_kernel(c_ref, wkk_ref, wv_ref, k_ref, v_ref):
    cin = c_ref[0, 0].astype(BF16)
    kk = jnp.dot(cin, wkk_ref[0], preferred_element_type=F32)
    vv = jnp.dot(cin[:, :MLA_KVL], wv_ref[0], preferred_element_type=F32)
    for hh in range(MLA_H):
        k_ref[0, 0, hh] = kk[:, hh * MLA_PAD:(hh + 1) * MLA_PAD].astype(BF16)
        v_ref[0, 0, hh] = vv[:, hh * MLA_V:(hh + 1) * MLA_V].astype(BF16)


def _cache_kv(cache_mla_kv, wkk_all, wv_all):
    return pl.pallas_call(
        _cache_kv_kernel,
        grid=(N_LAT, DEPTH),
        in_specs=[pl.BlockSpec((1, 1, PAST, MLA_KVL + MLA_ROPE), lambda n, l: (n, l, 0, 0)),
                  pl.BlockSpec((1, MLA_KVL + MLA_ROPE, MLA_H * MLA_PAD), lambda n, l: (l, 0, 0)),
                  pl.BlockSpec((1, MLA_KVL, MLA_H * MLA_V), lambda n, l: (l, 0, 0))],
        out_specs=[pl.BlockSpec((1, 1, MLA_H, PAST, MLA_PAD), lambda n, l: (n, l, 0, 0, 0)),
                   pl.BlockSpec((1, 1, MLA_H, PAST, MLA_V), lambda n, l: (n, l, 0, 0, 0))],
        out_shape=[jax.ShapeDtypeStruct((N_LAT, DEPTH, MLA_H, PAST, MLA_PAD), BF16),
                   jax.ShapeDtypeStruct((N_LAT, DEPTH, MLA_H, PAST, MLA_V), BF16)],
        compiler_params=_cparams(("arbitrary", "arbitrary")),
        name="cache_kv",
    )(cache_mla_kv, wkk_all, wv_all)


def _s5_disc_kernel(lr_ref, li_ref, ldt_ref, bre_ref, bim_ref, a_ref, bb_ref):
    lr = lr_ref[...]
    li = li_ref[...]
    dt = jnp.exp(ldt_ref[...])
    mag = jnp.exp(lr * dt)
    a_re = mag * jnp.cos(li * dt)
    a_im = mag * jnp.sin(li * dt)
    den = lr * lr + li * li
    nr = a_re - 1.0
    k_re = (nr * lr + a_im * li) / den
    k_im = (a_im * lr - nr * li) / den
    a_ref[0] = a_re
    a_ref[1] = a_im
    bre = bre_ref[...]
    bim = bim_ref[...]
    bb_ref[0] = k_re * bre - k_im * bim
    bb_ref[1] = k_re * bim + k_im * bre


def _s5_discretize(lam_re, lam_im, log_dt, b_re, b_im):
    g2 = 2 * S5_G
    a, bb = pl.pallas_call(
        _s5_disc_kernel,
        out_shape=[jax.ShapeDtypeStruct((2, g2, 1, S5_P), F32),
                   jax.ShapeDtypeStruct((2, g2, S5_C, S5_P), F32)],
        name="s5_disc",
    )(lam_re.reshape(g2, 1, S5_P), lam_im.reshape(g2, 1, S5_P), log_dt.reshape(g2, 1, 1),
      b_re.reshape(g2, S5_P, S5_C).transpose(0, 2, 1), b_im.reshape(g2, S5_P, S5_C).transpose(0, 2, 1))
    return a, bb


def _s5_matrices(a, bb, c_re, c_im):
    gp = S5_G * S5_P
    a_lanes = a.reshape(2, 2, S5_G, S5_P).transpose(1, 0, 2, 3).reshape(2, 2, gp)
    bbar = bb.reshape(2, 2, S5_G, S5_C, S5_P)
    eye = jnp.eye(S5_G, dtype=F32)
    wb = jnp.einsum('xdgcp,gh->dgcxhp', bbar, eye).reshape(2, S5_W, 2 * gp)
    cc = jnp.stack([c_re, -c_im], axis=1)
    wc = jnp.einsum('dxgcp,gh->dxhpgc', cc, eye).reshape(2, 2 * gp, S5_W)
    return a_lanes, wb.astype(BF16), wc.astype(BF16)


def _scan_kernel(u_ref, wb_ref, wc_ref, a_ref, h0_ref, y_ref, ht_ref, x_s, h_s, st_s):
    d = pl.program_id(0)
    c = pl.program_id(2)
    nc = pl.num_programs(2)
    gp = S5_G * S5_P

    @pl.when(c == 0)
    def _():
        st_s[...] = h0_ref[0]

    u = u_ref[...].reshape(SCAN_CH * SCAN_ROWS, S5_W).astype(BF16)
    x_s[...] = jnp.dot(u, wb_ref[0], preferred_element_type=F32)
    a_re = a_ref[0, 0:1, :]
    a_im = a_ref[0, 1:2, :]
    unroll = 4

    def outer(io, carry):
        h_re, h_im = carry
        for ii in range(unroll):
            i = io * unroll + ii
            t = jnp.where(d == 0, i, SCAN_CH - 1 - i)
            r = pl.multiple_of(t * SCAN_ROWS, SCAN_ROWS)
            x_re = x_s[pl.ds(r, SCAN_ROWS), pl.ds(0, gp)]
            x_im = x_s[pl.ds(r, SCAN_ROWS), pl.ds(gp, gp)]
            n_re = a_re * h_re - a_im * h_im + x_re
            n_im = a_re * h_im + a_im * h_re + x_im
            h_s[pl.ds(r, SCAN_ROWS), pl.ds(0, gp)] = n_re
            h_s[pl.ds(r, SCAN_ROWS), pl.ds(gp, gp)] = n_im
            h_re, h_im = n_re, n_im
        return h_re, h_im

    h_re, h_im = lax.fori_loop(0, SCAN_CH // unroll, outer, (st_s[:, pl.ds(0, gp)], st_s[:, pl.ds(gp, gp)]))
    st_s[:, pl.ds(0, gp)] = h_re
    st_s[:, pl.ds(gp, gp)] = h_im
    y = jnp.dot(h_s[...].astype(BF16), wc_ref[0], preferred_element_type=F32)
    y_ref[0] = y.reshape(SCAN_CH, SCAN_ROWS, S5_W)

    @pl.when(c == nc - 1)
    def _():
        ht_ref[0] = st_s[...]


def _s5_scan(u_tm, wb, wc, a_lanes, h0):
    L, n, _ = u_tm.shape
    nc = L // SCAN_CH
    ng = n // SCAN_ROWS

    def ceff(d, c):
        return c + d * (nc - 1 - 2 * c)

    return pl.pallas_call(
        _scan_kernel,
        grid=(2, ng, nc),
        in_specs=[pl.BlockSpec((SCAN_CH, SCAN_ROWS, S5_W), lambda d, g, c: (ceff(d, c), g, 0)),
                  pl.BlockSpec((1, S5_W, S5_STATE_W), lambda d, g, c: (d, 0, 0)),
                  pl.BlockSpec((1, S5_STATE_W, S5_W), lambda d, g, c: (d, 0, 0)),
                  pl.BlockSpec((1, 2, S5_G * S5_P), lambda d, g, c: (d, 0, 0)),
                  pl.BlockSpec((1, SCAN_ROWS, S5_STATE_W), lambda d, g, c: (d, g, 0))],
        out_specs=[pl.BlockSpec((1, SCAN_CH, SCAN_ROWS, S5_W), lambda d, g, c: (d, ceff(d, c), g, 0)),
                   pl.BlockSpec((1, SCAN_ROWS, S5_STATE_W), lambda d, g, c: (d, g, 0))],
        out_shape=[jax.ShapeDtypeStruct((2, L, n, S5_W), F32),
                   jax.ShapeDtypeStruct((2, n, S5_STATE_W), F32)],
        scratch_shapes=[pltpu.VMEM((SCAN_CH * SCAN_ROWS, S5_STATE_W), F32),
                        pltpu.VMEM((SCAN_CH * SCAN_ROWS, S5_STATE_W), F32),
                        pltpu.VMEM((SCAN_ROWS, S5_STATE_W), F32)],
        compiler_params=_cparams(("arbitrary", "arbitrary", "arbitrary")),
        name="s5_scan",
    )(u_tm, wb, wc, a_lanes, h0)


def _s5_pow_kernel(a_ref, p_ref):
    gp = S5_G * S5_P
    a_re = a_ref[0, 0:1, :]
    a_im = a_ref[0, 1:2, :]
    row = lax.broadcasted_iota(jnp.int32, (8, gp), 0)
    cur_re = jnp.broadcast_to(a_re, (8, gp))
    cur_im = jnp.broadcast_to(a_im, (8, gp))
    pr, pi = a_re, a_im
    for k in range(1, 8):
        pr, pi = pr * a_re - pi * a_im, pr * a_im + pi * a_re
        cur_re = jnp.where(row == k, pr, cur_re)
        cur_im = jnp.where(row == k, pi, cur_im)
    p_ref[0, 0, 0:8, :] = cur_re
    p_ref[0, 1, 0:8, :] = cur_im
    m = 8
    while m < S5_SEG:
        am_re = p_ref[0, 0, m - 1:m, :]
        am_im = p_ref[0, 1, m - 1:m, :]
        lo_re = p_ref[0, 0, 0:m, :]
        lo_im = p_ref[0, 1, 0:m, :]
        p_ref[0, 0, m:2 * m, :] = lo_re * am_re - lo_im * am_im
        p_ref[0, 1, m:2 * m, :] = lo_re * am_im + lo_im * am_re
        m *= 2


def _s5_powers(a_lanes):
    gp = S5_G * S5_P
    return pl.pallas_call(
        _s5_pow_kernel,
        grid=(2,),
        in_specs=[pl.BlockSpec((1, 2, gp), lambda d: (d, 0, 0))],
        out_specs=pl.BlockSpec((1, 2, S5_SEG, gp), lambda d: (d, 0, 0, 0)),
        out_shape=jax.ShapeDtypeStruct((2, 2, S5_SEG, gp), F32),
        compiler_params=_cparams(("arbitrary",)),
        name="s5_powers",
    )(a_lanes)


def _s5_fix_kernel(y_in, p_ref, ht_ref, wc_ref, y_out, hin_s):
    gp = S5_G * S5_P
    d = pl.program_id(0)
    r = pl.program_id(1)

    @pl.when(r == 0)
    def _():
        hin_s[...] = jnp.zeros_like(hin_s)

        def chain(order, pw_row):
            aw_re = p_ref[0, 0, pw_row:pw_row + 1, :]
            aw_im = p_ref[0, 1, pw_row:pw_row + 1, :]
            for n in range(N_LAT):
                h_re = jnp.zeros((1, gp), F32)
                h_im = jnp.zeros((1, gp), F32)
                for s_prev, s_cur in zip(order[:-1], order[1:]):
                    e_re = ht_ref[0, n * N_SEG + s_prev:n * N_SEG + s_prev + 1, 0:gp]
                    e_im = ht_ref[0, n * N_SEG + s_prev:n * N_SEG + s_prev + 1, gp:2 * gp]
                    h_re, h_im = aw_re * h_re - aw_im * h_im + e_re, aw_re * h_im + aw_im * h_re + e_im
                    hin_s[n * N_SEG + s_cur, :, 0:gp] = h_re
                    hin_s[n * N_SEG + s_cur, :, gp:2 * gp] = h_im

        @pl.when(d == 0)
        def _():
            chain(list(range(N_SEG)), S5_SEG - 1)

        @pl.when(d == 1)
        def _():
            chain(list(range(N_SEG - 1, -1, -1)), 0)

    hin = hin_s[r]
    h_re = hin[:, 0:gp]
    h_im = hin[:, gp:2 * gp]
    p_re = p_ref[0, 0]
    p_im = p_ref[0, 1]
    z = jnp.concatenate([p_re * h_re - p_im * h_im, p_re * h_im + p_im * h_re], axis=-1).astype(BF16)
    y_out[0] = y_in[0] + jnp.dot(z, wc_ref[0], preferred_element_type=F32)


def _s5_fix(y_tok, ptab, ht_loc, wc):
    gp = S5_G * S5_P
    return pl.pallas_call(
        _s5_fix_kernel,
        grid=(2, SCAN_ROWS),
        in_specs=[pl.BlockSpec((1, S5_SEG, S5_W), lambda d, r: (d, r, 0)),
                  pl.BlockSpec((1, 2, S5_SEG, gp), lambda d, r: (d, 0, 0, 0)),
                  pl.BlockSpec((1, SCAN_ROWS, S5_STATE_W), lambda d, r: (d, 0, 0)),
                  pl.BlockSpec((1, S5_STATE_W, S5_W), lambda d, r: (d, 0, 0))],
        out_specs=pl.BlockSpec((1, S5_SEG, S5_W), lambda d, r: (d, r, 0)),
        out_shape=jax.ShapeDtypeStruct(y_tok.shape, F32),
        scratch_shapes=[pltpu.VMEM((SCAN_ROWS, 1, S5_STATE_W), F32)],
        input_output_aliases={0: 0},
        compiler_params=_cparams(("arbitrary", "arbitrary")),
        name="s5_fix",
    )(y_tok, ptab, ht_loc, wc)


def _qk(q, k):
    return lax.dot_general(q, k, (((1,), (1,)), ((), ())), preferred_element_type=F32)


def _softmax_pv(s_list, v_list):
    m = s_list[0].max(axis=-1, keepdims=True)
    for s in s_list[1:]:
        m = jnp.maximum(m, s.max(axis=-1, keepdims=True))
    den = None
    acc = None
    for s, v in zip(s_list, v_list):
        e = jnp.exp(s - m)
        ds = e.sum(axis=-1, keepdims=True)
        pv = jnp.dot(e.astype(BF16), v, preferred_element_type=F32)
        den = ds if den is None else den + ds
        acc = pv if acc is None else acc + pv
    return acc / den


def _ctx_attn_kernel(nq_ref, nk_ref, nv_ref, mq_ref, mk_ref, mv_ref, ckv_ref, fk_in, fv_in, fm_in,
                     na_ref, mla_ref, fk_ref, fv_ref, fm_ref):
    del fk_in, fv_in, fm_in
    fk_ref[0, 0] = nk_ref[0]
    fv_ref[0, 0] = nv_ref[0]
    fm_ref[0, 0] = ckv_ref[...]
    for hh in range(NA_H):
        s = _qk(nq_ref[0, hh], nk_ref[0, hh].astype(BF16)) * NA_SCALE
        na_ref[:, hh * NA_D:(hh + 1) * NA_D] = _softmax_pv([s], [nv_ref[0, hh].astype(BF16)])
    for hh in range(MLA_H):
        s = _qk(mq_ref[0, hh], mk_ref[0, hh]) * MLA_SCALE
        mla_ref[:, hh * MLA_V:(hh + 1) * MLA_V] = _softmax_pv([s], [mv_ref[0, hh]])


def _ctx_attention(naq, nak, nav, mq, mk, mv, ckv, fin_k, fin_v, fin_m, layer):
    head4 = lambda i: (i, 0, 0, 0)
    tile2 = lambda i: (i, 0)
    anyspec = pl.BlockSpec(memory_space=pl.ANY)
    return pl.pallas_call(
        _ctx_attn_kernel,
        grid=(CTX_TILES,),
        in_specs=[pl.BlockSpec((1, NA_H, TM, NA_D), head4)] * 3
                 + [pl.BlockSpec((1, MLA_H, TM, MLA_PAD), head4)] * 2
                 + [pl.BlockSpec((1, MLA_H, TM, MLA_V), head4),
                    pl.BlockSpec((TM, MLA_KVL + MLA_ROPE), tile2), anyspec, anyspec, anyspec],
        out_specs=[pl.BlockSpec((TM, NA_H * NA_D), tile2), pl.BlockSpec((TM, MLA_H * MLA_V), tile2),
                   pl.BlockSpec((1, 1, NA_H, TM, NA_D), lambda i: (i, layer, 0, 0, 0)),
                   pl.BlockSpec((1, 1, NA_H, TM, NA_D), lambda i: (i, layer, 0, 0, 0)),
                   pl.BlockSpec((1, 1, TM, MLA_KVL + MLA_ROPE), lambda i: (i, layer, 0, 0))],
        out_shape=[jax.ShapeDtypeStruct((T_CTX, NA_H * NA_D), F32),
                   jax.ShapeDtypeStruct((T_CTX, MLA_H * MLA_V), F32),
                   jax.ShapeDtypeStruct(fin_k.shape, F32),
                   jax.ShapeDtypeStruct(fin_v.shape, F32),
                   jax.ShapeDtypeStruct(fin_m.shape, F32)],
        input_output_aliases={7: 2, 8: 3, 9: 4},
        compiler_params=_cparams(("arbitrary",)),
        name="ctx_attn",
    )(naq, nak, nav, mq, mk, mv, ckv, fin_k, fin_v, fin_m)


def _na_lat_kernel(q_ref, k0_ref, k1_ref, k2_ref, v0_ref, v1_ref, v2_ref, kc_ref, vc_ref, b_ref, o_ref):
    for hh in range(NA_H):
        q = q_ref[0, hh]
        kb = jnp.concatenate([k0_ref[0, hh], k1_ref[0, hh], k2_ref[0, hh]], axis=0).astype(BF16)
        vb = jnp.concatenate([v0_ref[0, hh], v1_ref[0, hh], v2_ref[0, hh]], axis=0).astype(BF16)
        s_win = _qk(q, kb) * NA_SCALE + b_ref[0, 0, hh]
        s_ctx = _qk(q, kc_ref[0, 0, hh].astype(BF16)) * NA_SCALE
        o_ref[:, hh * NA_D:(hh + 1) * NA_D] = _softmax_pv([s_win, s_ctx], [vb, vc_ref[0, 0, hh].astype(BF16)])


def _band_start(j):
    return jnp.clip(j - 1, 0, LAT_TILES - 3)


def _na_latent(naq, nak, nav, cache_k, cache_v, bias, layer):
    def qmap(n, j):
        return (CTX_TILES + n * LAT_TILES + j, 0, 0, 0)

    def kmap(off):
        return lambda n, j: (CTX_TILES + n * LAT_TILES + _band_start(j) + off, 0, 0, 0)

    def bmap(n, j):
        return (layer, jnp.where(j == 0, 0, jnp.where(j == LAT_TILES - 1, 2, 1)), 0, 0, 0)

    blk = (1, NA_H, TM, NA_D)
    return pl.pallas_call(
        _na_lat_kernel,
        grid=(N_LAT, LAT_TILES),
        in_specs=[pl.BlockSpec(blk, qmap)]
                 + [pl.BlockSpec(blk, kmap(o)) for o in range(3)]
                 + [pl.BlockSpec(blk, kmap(o)) for o in range(3)]
                 + [pl.BlockSpec((1, 1, NA_H, PAST, NA_D), lambda n, j: (n, layer, 0, 0, 0))] * 2
                 + [pl.BlockSpec((1, 1, NA_H, TM, 3 * TM), bmap)],
        out_specs=pl.BlockSpec((TM, NA_H * NA_D), lambda n, j: (n * LAT_TILES + j, 0)),
        out_shape=jax.ShapeDtypeStruct((T_LAT, NA_H * NA_D), F32),
        compiler_params=_cparams(("arbitrary", "arbitrary")),
        name="na_latent",
    )(naq, nak, nak, nak, nav, nav, nav, cache_k, cache_v, bias)


def _na_bias_tables(rpb_all):
    rows = L_LAT // GRID_W
    nr, ncol = 2 * WIN_H - 1, 2 * WIN_W - 1
    w = np.arange(GRID_W)
    cs = np.clip(w - WIN_W // 2, 0, GRID_W - WIN_W)
    cc = np.arange(GRID_W)
    col_ok = (cc[None, :] >= cs[:, None]) & (cc[None, :] < cs[:, None] + WIN_W)
    dc = cc[None, :] - w[:, None] + WIN_W - 1
    col_sel = (dc[:, :, None] == np.arange(ncol)) & col_ok[:, :, None]
    row_sel, oks = [], []
    for j, s in ((0, 0), (1, 0), (LAT_TILES - 1, LAT_TILES - 3)):
        r = 4 * j + np.arange(4)
        rs = np.clip(r - WIN_H // 2, 0, rows - WIN_H)
        krow = 4 * s + np.arange(12)
        row_ok = (krow[None, :] >= rs[:, None]) & (krow[None, :] < rs[:, None] + WIN_H)
        dr = krow[None, :] - r[:, None] + WIN_H - 1
        row_sel.append((dr[:, :, None] == np.arange(nr)) & row_ok[:, :, None])
        oks.append(row_ok[:, None, :, None] & col_ok[None, :, None, :])
    row_sel = jnp.asarray(np.stack(row_sel), F32)
    col_sel = jnp.asarray(col_sel, F32)
    mask = jnp.asarray(np.where(np.stack(oks), 0.0, NEG).reshape(3, TM, 3 * TM), F32)
    b = jnp.einsum('prka,lhab,wcb->lphrwkc', row_sel, rpb_all, col_sel, precision=lax.Precision.HIGHEST)
    return b.reshape(DEPTH, 3, NA_H, TM, 3 * TM) + mask[None, :, None]


def _mla_lat_kernel(q_ref, k_ref, v_ref, kc_ref, vc_ref, o_ref):
    for hh in range(MLA_H):
        q = q_ref[0, hh]
        k = k_ref[:, hh].reshape(L_LAT, MLA_PAD)
        v = v_ref[:, hh].reshape(L_LAT, MLA_V)
        s_lat = _qk(q, k) * MLA_SCALE
        s_ctx = _qk(q, kc_ref[0, 0, hh]) * MLA_SCALE
        o_ref[:, hh * MLA_V:(hh + 1) * MLA_V] = _softmax_pv([s_lat, s_ctx], [v, vc_ref[0, 0, hh]])


def _mla_latent(mq, mk, mv, kc, vc, layer):
    seq_blk = CTX_TILES // LAT_TILES
    return pl.pallas_call(
        _mla_lat_kernel,
        grid=(N_LAT, LAT_TILES),
        in_specs=[pl.BlockSpec((1, MLA_H, TM, MLA_PAD), lambda n, j: (CTX_TILES + n * LAT_TILES + j, 0, 0, 0)),
                  pl.BlockSpec((LAT_TILES, MLA_H, TM, MLA_PAD), lambda n, j: (seq_blk + n, 0, 0, 0)),
                  pl.BlockSpec((LAT_TILES, MLA_H, TM, MLA_V), lambda n, j: (seq_blk + n, 0, 0, 0)),
                  pl.BlockSpec((1, 1, MLA_H, PAST, MLA_PAD), lambda n, j: (n, layer, 0, 0, 0)),
                  pl.BlockSpec((1, 1, MLA_H, PAST, MLA_V), lambda n, j: (n, layer, 0, 0, 0))],
        out_specs=pl.BlockSpec((TM, MLA_H * MLA_V), lambda n, j: (n * LAT_TILES + j, 0)),
        out_shape=jax.ShapeDtypeStruct((T_LAT, MLA_H * MLA_V), F32),
        compiler_params=_cparams(("arbitrary", "arbitrary")),
        name="mla_latent",
    )(mq, mk, mv, kc, vc)


def _post_kernel(y_ref, u_ref, d_ref, gw_ref, gb_ref, nac_ref, nal_ref, mlac_ref, mlal_ref, wo_ref, x_ref, mod_ref,
                 g2_ref, rw_ref, rb_ref, x1_ref, h2_ref, te_ref, tg_ref, rk_ref, cnt_ref):
    ys = y_ref[0] + y_ref[1] + u_ref[...] * d_ref[...]
    g = jax.nn.gelu(ys)
    s5o = g * jax.nn.sigmoid(jnp.dot(g.astype(BF16), gw_ref[...], preferred_element_type=F32) + gb_ref[...])
    is_ctx = pl.program_id(0) < CTX_TILES
    na = jnp.where(is_ctx, nac_ref[...], nal_ref[...])
    mla = jnp.where(is_ctx, mlac_ref[...], mlal_ref[...])
    out = (jnp.dot(s5o.astype(BF16), wo_ref[0:S5_W, :], preferred_element_type=F32)
           + jnp.dot(na.astype(BF16), wo_ref[S5_W:S5_W + NA_H * NA_D, :], preferred_element_type=F32)
           + jnp.dot(mla.astype(BF16), wo_ref[S5_W + NA_H * NA_D:, :], preferred_element_type=F32))
    mod = mod_ref[0]
    x1 = x_ref[...] + mod[2:3, :] * out
    x1_ref[...] = x1
    h2 = _rms(x1, g2_ref[...]) * (1.0 + mod[4:5, :]) + mod[3:4, :]
    h2_ref[...] = h2
    h_hi = h2.astype(BF16)
    h_lo = (h2 - h_hi.astype(F32)).astype(BF16)
    logits = (jnp.dot(h_hi, rw_ref[0], preferred_element_type=F32)
              + jnp.dot(h_hi, rw_ref[1], preferred_element_type=F32)
              + jnp.dot(h_lo, rw_ref[0], preferred_element_type=F32)) + rb_ref[...]
    lane_i = lax.broadcasted_iota(jnp.int32, (TM, 128), 1)
    lane = lane_i.astype(F32)
    cur = jnp.where(lane_i < N_EXP, logits, -jnp.inf)
    te = jnp.zeros((TM, 128), F32)
    tv = jnp.zeros((TM, 128), F32)
    hot = jnp.zeros((TM, 128), F32)
    idxs = []
    top = None
    for k in range(TOP_K):
        m = cur.max(axis=-1, keepdims=True)
        idx = jnp.where(cur == m, lane, 128.0).min(axis=-1, keepdims=True)
        sel = lane == idx
        top = m if top is None else top
        idxs.append(idx)
        te = jnp.where(lane_i == k, idx, te)
        tv = jnp.where(lane_i == k, jnp.exp(m - top), tv)
        hot = jnp.where(sel, 1.0, hot)
        cur = jnp.where(sel, -jnp.inf, cur)
    te_ref[...] = te.astype(jnp.int32)
    tg_ref[...] = tv / tv.sum(axis=-1, keepdims=True)
    row = lax.broadcasted_iota(jnp.int32, (TM, TM), 0)
    col = lax.broadcasted_iota(jnp.int32, (TM, TM), 1)
    tri = jnp.where(col < row, 1.0, 0.0).astype(BF16)
    before = jnp.dot(tri, hot.astype(BF16), preferred_element_type=F32)
    rk = jnp.zeros((TM, 128), F32)
    for k in range(TOP_K):
        rk = jnp.where(lane_i == k, jnp.where(lane == idxs[k], before, 0.0).sum(axis=-1, keepdims=True), rk)
    rk_ref[...] = rk.astype(jnp.int32)
    cnt_ref[0] = jnp.broadcast_to(hot.sum(axis=0, keepdims=True), (8, 128)).astype(jnp.int32)


def _post(y2, u, s5_d, glu_w, glu_b, na_ctx, na_lat, mla_ctx, mla_lat, w_out, x, mod_l, g2, rw, rb):
    const2 = lambda i: (0, 0)
    tile2 = lambda i: (i, 0)
    ctx2 = lambda i: (jnp.minimum(i, CTX_TILES - 1), 0)
    lat2 = lambda i: (jnp.maximum(i - CTX_TILES, 0), 0)
    return pl.pallas_call(
        _post_kernel,
        grid=(N_TILES,),
        in_specs=[pl.BlockSpec((2, TM, S5_W), lambda i: (0, i, 0)),
                  pl.BlockSpec((TM, S5_W), tile2),
                  pl.BlockSpec((1, S5_W), const2),
                  pl.BlockSpec((S5_W, S5_W), const2),
                  pl.BlockSpec((1, S5_W), const2),
                  pl.BlockSpec((TM, NA_H * NA_D), ctx2),
                  pl.BlockSpec((TM, NA_H * NA_D), lat2),
                  pl.BlockSpec((TM, MLA_H * MLA_V), ctx2),
                  pl.BlockSpec((TM, MLA_H * MLA_V), lat2),
                  pl.BlockSpec((D_MODEL, D_MODEL), const2),
                  pl.BlockSpec((TM, D_MODEL), tile2),
                  pl.BlockSpec((1, 6, D_MODEL), lambda i: (_mod_row(i), 0, 0)),
                  pl.BlockSpec((1, D_MODEL), const2),
                  pl.BlockSpec((2, D_MODEL, 128), lambda i: (0, 0, 0)),
                  pl.BlockSpec((1, 128), const2)],
        out_specs=[pl.BlockSpec((TM, D_MODEL), tile2),
                   pl.BlockSpec((TM, D_MODEL), tile2),
                   pl.BlockSpec((TM, 128), tile2),
                   pl.BlockSpec((TM, 128), tile2),
                   pl.BlockSpec((TM, 128), tile2),
                   pl.BlockSpec((1, 8, 128), lambda i: (i, 0, 0))],
        out_shape=[jax.ShapeDtypeStruct((T_ALL, D_MODEL), F32),
                   jax.ShapeDtypeStruct((T_ALL, D_MODEL), F32),
                   jax.ShapeDtypeStruct((T_ALL, 128), jnp.int32),
                   jax.ShapeDtypeStruct((T_ALL, 128), F32),
                   jax.ShapeDtypeStruct((T_ALL, 128), jnp.int32),
                   jax.ShapeDtypeStruct((N_TILES, 8, 128), jnp.int32)],
        compiler_params=_cparams(("arbitrary",)),
        name="post",
    )(y2, u, s5_d, glu_w, glu_b, na_ctx, na_lat, mla_ctx, mla_lat, w_out, x, mod_l, g2, rw, rb)


def _moe_fused_kernel(be_ref, first_ref, nused_ref, pos_ref,
                      dflt_hbm, h_hbm, wgu_ref, bgu_ref, wd_ref, bd_ref, ys_hbm,
                      asg_s, xs0, xs1, yo0, yo1, wgu_s, wd_s, x_bf, gsem, ssem, tsem):
    b = pl.program_id(0)
    nused = nused_ref[0]
    xs = (xs0, xs1)
    yo = (yo0, yo1)

    def wait_gather(p):
        pltpu.make_async_copy(h_hbm.at[pl.ds(0, MOE_BM), :], xs[p], gsem.at[p]).wait()

    def wait_scatter(p):
        pltpu.make_async_copy(yo[p], ys_hbm.at[pl.ds(0, MOE_BM), :], ssem.at[p]).wait()

    def issue_gather(blk, p, lo=0, hi=MOE_BM):
        r0 = (blk + 1) * MOE_BM
        for j in range(lo, hi):
            tok = asg_s[r0 + j] >> 16
            pltpu.make_async_copy(h_hbm.at[pl.ds(tok, 1), :], xs[p].at[pl.ds(j, 1), :],
                                  gsem.at[p]).start(priority=j % 2)

    def issue_scatter(blk, p, lo=0, hi=MOE_BM):
        r0 = (blk + 1) * MOE_BM
        for j in range(lo, hi):
            d = asg_s[r0 + j] & 0xFFFF
            pltpu.make_async_copy(yo[p].at[pl.ds(j, 1), :], ys_hbm.at[pl.ds(d, 1), :],
                                  ssem.at[p]).start(priority=j % 2)

    @pl.when(b == 0)
    def _():
        cp = pltpu.make_async_copy(dflt_hbm, asg_s, tsem)
        cp.start()
        cp.wait()

        def inv(i, carry):
            a0 = i * 8
            v0 = a0 | ((((a0 >> 10) << 8) + (a0 & (TM - 1))) << 16)
            for uu in range(8):
                asg_s[pos_ref[a0 + uu]] = v0 + uu * ((1 << 16) + 1)
            return carry
        lax.fori_loop(0, MOE_SLOTS // 8, inv, 0)
        yo1[...] = jnp.zeros_like(yo1)
        zc = pltpu.make_async_copy(yo1, ys_hbm.at[pl.ds(MOE_SLOTS, MOE_BM), :], tsem)
        zc.start()
        zc.wait()
        issue_gather(0, 0)

    def step(p):
        q = 1 - p
        wait_gather(p)
        if p == 0:
            @pl.when(b >= 1)
            def _():
                wait_scatter(p)
        else:
            wait_scatter(p)

        @pl.when(first_ref[b] == 1)
        def _():
            wgu_s[...] = wgu_ref[0, 0].astype(BF16)
            wd_s[...] = wd_ref[0, 0].astype(BF16)

        x_bf[...] = xs[p][...].astype(BF16)
        nxt = jnp.minimum(b + 1, nused - 1)
        fc = D_FF // MOE_CHUNKS
        rc = MOE_BM // MOE_CHUNKS

        def region(c):
            issue_gather(nxt, q, c * rc, (c + 1) * rc)
            issue_scatter(b - 1, q, c * rc, (c + 1) * rc)
            gate = jnp.dot(x_bf[...], wgu_s[:, c * fc:(c + 1) * fc], preferred_element_type=F32)
            gate = jnp.minimum(gate + bgu_ref[0, 0, :, c * fc:(c + 1) * fc], SWIGLU_LIMIT)
            up = jnp.dot(x_bf[...], wgu_s[:, D_FF + c * fc:D_FF + (c + 1) * fc], preferred_element_type=F32)
            up = jnp.clip(up + bgu_ref[0, 0, :, D_FF + c * fc:D_FF + (c + 1) * fc], -SWIGLU_LIMIT, SWIGLU_LIMIT)
            act = ((up + 1.0) * (gate * jax.nn.sigmoid(SWIGLU_ALPHA * gate))).astype(BF16)
            part = jnp.dot(act, wd_s[c * fc:(c + 1) * fc, :], preferred_element_type=F32)
            if c == 0:
                yo[p][...] = part + bd_ref[0, 0]
            else:
                yo[p][...] += part

        region(0)
        for c in range(1, MOE_CHUNKS):
            pl.when(first_ref[b] > -c)(functools.partial(region, c))

    def drain(p):
        q = 1 - p
        wait_gather(p)
        wait_scatter(p)
        issue_scatter(nused - 1, q)
        wait_scatter(q)

    for p in range(2):
        @pl.when((b < nused) & (b % 2 == p))
        def _():
            step(p)

        @pl.when((b == nused) & (b % 2 == p))
        def _():
            drain(p)


def _moe_fused(block_e, first, nused, pos_km, h2, w_gu, b_gu, w_down, b_down, layer):
    def wmap(b, be, fi, nu, pos):
        return (layer, be[b], 0, 0)

    r = jnp.arange(ASG_ROWS, dtype=jnp.int32)
    dflt = MOE_SLOTS + (((r >> 8) + 1) & 1) * MOE_BM + (r & (MOE_BM - 1))

    return pl.pallas_call(
        _moe_fused_kernel,
        grid_spec=pltpu.PrefetchScalarGridSpec(
            num_scalar_prefetch=4, grid=(MOE_BLOCKS + 1,),
            in_specs=[pl.BlockSpec(memory_space=pl.ANY), pl.BlockSpec(memory_space=pl.ANY),
                      pl.BlockSpec((1, 1, D_MODEL, 2 * D_FF), wmap),
                      pl.BlockSpec((1, 1, 1, 2 * D_FF), wmap),
                      pl.BlockSpec((1, 1, D_FF, D_MODEL), wmap),
                      pl.BlockSpec((1, 1, 1, D_MODEL), wmap)],
            out_specs=pl.BlockSpec(memory_space=pl.ANY),
            scratch_shapes=[pltpu.SMEM((ASG_ROWS,), jnp.int32),
                            pltpu.VMEM((MOE_BM, D_MODEL), F32), pltpu.VMEM((MOE_BM, D_MODEL), F32),
                            pltpu.VMEM((MOE_BM, D_MODEL), F32), pltpu.VMEM((MOE_BM, D_MODEL), F32),
                            pltpu.VMEM((D_MODEL, 2 * D_FF), BF16), pltpu.VMEM((D_FF, D_MODEL), BF16),
                            pltpu.VMEM((MOE_BM, D_FF), BF16),
                            pltpu.SemaphoreType.DMA((2,)), pltpu.SemaphoreType.DMA((2,)),
                            pltpu.SemaphoreType.DMA(())]),
        out_shape=jax.ShapeDtypeStruct((MOE_SLOTS + 2 * MOE_BM, D_MODEL), F32),
        compiler_params=_cparams(("arbitrary",)),
        name="moe_fused",
    )(block_e, first, nused, pos_km, dflt, h2, w_gu, b_gu.reshape(DEPTH, N_EXP, 1, 2 * D_FF), w_down,
      b_down.reshape(DEPTH, N_EXP, 1, D_MODEL))


def _combine_kernel(ys_ref, tg_ref, x1_ref, mod_ref, o_ref):
    tg = tg_ref[...]
    y = ys_ref[0:TM, :] * tg[:, 0:1]
    for k in range(1, TOP_K):
        y = y + ys_ref[k * TM:(k + 1) * TM, :] * tg[:, k:k + 1]
    o_ref[...] = x1_ref[...] + mod_ref[0][5:6, :] * y


def _combine(ys, tg, x1, mod_l):
    return pl.pallas_call(
        _combine_kernel,
        grid=(N_TILES,),
        in_specs=[pl.BlockSpec((TOP_K * TM, D_MODEL), lambda i: (i, 0)),
                  pl.BlockSpec((TM, 128), lambda i: (i, 0)),
                  pl.BlockSpec((TM, D_MODEL), lambda i: (i, 0)),
                  pl.BlockSpec((1, 6, D_MODEL), lambda i: (_mod_row(i), 0, 0))],
        out_specs=pl.BlockSpec((TM, D_MODEL), lambda i: (i, 0)),
        out_shape=jax.ShapeDtypeStruct((T_ALL, D_MODEL), F32),
        compiler_params=_cparams(("arbitrary",)),
        name="moe_combine",
    )(ys, tg, x1, mod_l)


def _moe_plan(te, rk, cnt):
    tile_cnt = cnt[:, 0, :N_EXP]
    counts = tile_cnt.sum(axis=0)
    padded = (counts + MOE_BM - 1) // MOE_BM * MOE_BM
    pad_end = jnp.cumsum(padded)
    pad_start = pad_end - padded
    tile_off = jnp.cumsum(tile_cnt, axis=0) - tile_cnt
    base = (pad_start[None, :] + tile_off).astype(jnp.int32)
    e = te[:, :TOP_K].reshape(N_TILES, TM, TOP_K)
    hot = e[..., None] == jnp.arange(N_EXP, dtype=jnp.int32)
    pos = jnp.sum(jnp.where(hot, base[:, None, None, :], 0), axis=-1) + rk[:, :TOP_K].reshape(N_TILES, TM, TOP_K)
    pos_km = (pos.transpose(0, 2, 1).reshape(-1) + MOE_BM).astype(jnp.int32)
    nused = (pad_end[-1] // MOE_BM).astype(jnp.int32)
    blk = jnp.arange(MOE_BLOCKS + 1, dtype=jnp.int32)
    be = jnp.minimum(jnp.sum((pad_end[None, :] <= (blk * MOE_BM)[:, None]).astype(jnp.int32), axis=1), N_EXP - 1)
    be = jnp.where(blk < nused, be, be[nused - 1])
    first = jnp.concatenate([jnp.ones((1,), jnp.int32), (be[1:] != be[:-1]).astype(jnp.int32)])
    return pos_km, be, first, nused.reshape(1)


def _final_kernel(x_ref, g_ref, o_ref):
    o_ref[...] = _rms(x_ref[...], g_ref[...])


def _final_norm(x, g):
    return pl.pallas_call(
        _final_kernel,
        grid=(N_TILES,),
        in_specs=[pl.BlockSpec((TM, D_MODEL), lambda i: (i, 0)), pl.BlockSpec((1, D_MODEL), lambda i: (0, 0))],
        out_specs=pl.BlockSpec((TM, D_MODEL), lambda i: (i, 0)),
        out_shape=jax.ShapeDtypeStruct((T_ALL, D_MODEL), F32),
        compiler_params=_cparams(("arbitrary",)),
        name="final_norm",
    )(x, g)


_ROPE_PERM = np.concatenate([np.arange(8, 16), np.arange(0, 8), np.arange(24, 32), np.arange(16, 24)])


def _rope_tables():
    half = MLA_ROPE // 2
    t = jnp.arange(L_LAT)
    row = (t // GRID_W).astype(F32)
    col = (t % GRID_W).astype(F32)
    inv = ROPE_BASE ** (-jnp.arange(0, half, 2, dtype=F32) / half)

    def part(pos):
        ang = pos[:, None] * inv[None, :]
        c, s = jnp.cos(ang), jnp.sin(ang)
        return jnp.concatenate([c, c], axis=-1), jnp.concatenate([-s, s], axis=-1)

    cr, sr = part(row)
    cc, sc = part(col)
    cos32 = jnp.concatenate([cr, cc], axis=-1)
    sin32 = jnp.concatenate([sr, sc], axis=-1)
    ck = jnp.concatenate([jnp.ones((TM, MLA_ROPE), F32), cos32], axis=0)
    sk = jnp.concatenate([jnp.zeros((TM, MLA_ROPE), F32), sin32], axis=0)
    padw = MLA_PAD - MLA_NOPE - MLA_ROPE
    cq = jnp.concatenate([jnp.ones((TM + L_LAT, MLA_NOPE), F32), ck, jnp.ones((TM + L_LAT, padw), F32)], axis=-1)
    sq = jnp.concatenate([jnp.zeros((TM + L_LAT, MLA_NOPE), F32), sk, jnp.zeros((TM + L_LAT, padw), F32)], axis=-1)
    return cq, sq, ck, sk


def _mla_weights(w_qb, w_kvb):
    dq = MLA_NOPE + MLA_ROPE
    wq = w_qb.reshape(MLA_QL, MLA_H, dq)
    zpad = jnp.zeros((MLA_QL, MLA_H, MLA_PAD - dq), F32)
    q_main = jnp.concatenate([wq, zpad], axis=-1)
    q_part = jnp.concatenate([jnp.zeros((MLA_QL, MLA_H, MLA_NOPE), F32), wq[:, :, MLA_NOPE + _ROPE_PERM], zpad], axis=-1)
    wq2 = jnp.concatenate([q_main.reshape(MLA_QL, -1), q_part.reshape(MLA_QL, -1)], axis=-1).astype(BF16)
    wkv = w_kvb.reshape(MLA_KVL, MLA_H, MLA_NOPE + MLA_V)
    k_top = jnp.concatenate([wkv[:, :, :MLA_NOPE], jnp.zeros((MLA_KVL, MLA_H, MLA_PAD - MLA_NOPE), F32)], axis=-1)
    place = jnp.concatenate([jnp.zeros((MLA_ROPE, MLA_NOPE), F32), jnp.eye(MLA_ROPE, dtype=F32),
                             jnp.zeros((MLA_ROPE, MLA_PAD - dq), F32)], axis=-1)
    k_bot = jnp.broadcast_to(place[:, None, :], (MLA_ROPE, MLA_H, MLA_PAD))
    wkk = jnp.concatenate([k_top, k_bot], axis=0).reshape(MLA_KVL + MLA_ROPE, MLA_H * MLA_PAD).astype(BF16)
    wv = wkv[:, :, MLA_NOPE:].reshape(MLA_KVL, MLA_H * MLA_V).astype(BF16)
    return wq2, wkk, wv


def kernel(x_prompt, x_sample, cache_na_k, cache_na_v, cache_mla_kv, state_s5_re, state_s5_im, c, c_ctx, w_mod, b_mod, norm1_g, norm2_g, w_in, w_out, s5_lambda_re, s5_lambda_im, s5_log_dt, s5_b_re, s5_b_im, s5_c_re, s5_c_im, s5_d, s5_glu_w, s5_glu_b, na_rpb, mla_q_norm_g, mla_w_qb, mla_kv_norm_g, mla_w_kvb, router_w, router_b, moe_w_gu, moe_b_gu, moe_w_down, moe_b_down, final_norm_g):
    x = jnp.concatenate([x_prompt.reshape(T_CTX, D_MODEL), x_sample.reshape(T_LAT, D_MODEL)], axis=0)
    cv = jnp.concatenate([c_ctx[None, :], c, jnp.zeros((MOD_ROWS - 1 - N_LAT, D_MODEL), F32)], axis=0)
    mod = _modulation(cv, w_mod, b_mod).reshape(DEPTH, MOD_ROWS, 6, D_MODEL)
    cq, sq, ck, sk = _rope_tables()
    mla_w = [_mla_weights(mla_w_qb[l], mla_w_kvb[l]) for l in range(DEPTH)]
    kc_all, vc_all = _cache_kv(cache_mla_kv, jnp.stack([m[1] for m in mla_w]), jnp.stack([m[2] for m in mla_w]))
    gp = S5_G * S5_P
    na_bias = _na_bias_tables(na_rpb)
    fin_k = jnp.zeros((N_CTX, DEPTH, NA_H, L_CTX, NA_D), F32)
    fin_v = jnp.zeros((N_CTX, DEPTH, NA_H, L_CTX, NA_D), F32)
    fin_m = jnp.zeros((N_CTX, DEPTH, L_CTX, MLA_KVL + MLA_ROPE), F32)

    s5_re_list, s5_im_list = [], []
    for l in range(DEPTH):
        mod_l = mod[l]
        wq2, wkk, wv = mla_w[l]
        w_in_ext = jnp.concatenate([w_in[l], w_in[l][:, OFF_KVA + MLA_KVL + _ROPE_PERM]], axis=-1).astype(BF16)
        u, naq, nak, nav, mq, mk, mv, ckv = _project(
            x, mod_l, norm1_g[l][None], w_in_ext, mla_q_norm_g[l][None], wq2, mla_kv_norm_g[l][None], wkk, wv,
            cq, sq, ck, sk)

        a, bb = _s5_discretize(s5_lambda_re[l], s5_lambda_im[l], s5_log_dt[l], s5_b_re[l], s5_b_im[l])
        a_lanes, wb, wc = _s5_matrices(a, bb, s5_c_re[l], s5_c_im[l])
        u_ctx = u[:T_CTX].reshape(N_CTX, L_CTX, S5_W).transpose(1, 0, 2)
        y_ctx, ht_ctx = _s5_scan(u_ctx, wb, wc, a_lanes, jnp.zeros((2, N_CTX, S5_STATE_W), F32))
        u_lat = u[T_CTX:].reshape(SCAN_ROWS, S5_SEG, S5_W).transpose(1, 0, 2)
        h0 = jnp.concatenate([state_s5_re[:, l].reshape(N_LAT, 2, gp), state_s5_im[:, l].reshape(N_LAT, 2, gp)], axis=-1)
        h0z = jnp.zeros((N_LAT, N_SEG - 1, S5_STATE_W), F32)
        h0 = jnp.stack([jnp.concatenate([h0[:, 0:1], h0z], axis=1).reshape(SCAN_ROWS, S5_STATE_W),
                        jnp.concatenate([h0z, h0[:, 1:2]], axis=1).reshape(SCAN_ROWS, S5_STATE_W)], axis=0)
        y_loc, ht_loc = _s5_scan(u_lat, wb, wc, a_lanes, h0)
        ptab = _s5_powers(a_lanes)
        ptab = jnp.concatenate([ptab[0:1], jnp.flip(ptab[1:2], axis=2)], axis=0)
        y_lat = _s5_fix(y_loc.transpose(0, 2, 1, 3).reshape(2, T_LAT, S5_W), ptab, ht_loc, wc)
        y2 = jnp.concatenate([y_ctx.transpose(0, 2, 1, 3).reshape(2, T_CTX, S5_W), y_lat], axis=1)
        st = ht_ctx.reshape(2, N_CTX, 2, S5_G, S5_P).transpose(1, 0, 2, 3, 4)
        s5_re_list.append(st[:, :, 0])
        s5_im_list.append(st[:, :, 1])

        na_ctx, mla_ctx, fin_k, fin_v, fin_m = _ctx_attention(naq, nak, nav, mq, mk, mv, ckv, fin_k, fin_v, fin_m, l)
        na_lat = _na_latent(naq, nak, nav, cache_na_k, cache_na_v, na_bias, l)
        mla_lat = _mla_latent(mq, mk, mv, kc_all, vc_all, l)

        rw = jnp.pad(router_w[l], ((0, 0), (0, 128 - N_EXP)))
        rw_hi = rw.astype(BF16)
        rw = jnp.stack([rw_hi, (rw - rw_hi.astype(F32)).astype(BF16)], axis=0)
        rb = jnp.pad(router_b[l], (0, 128 - N_EXP))[None]
        x1, h2, te, tg, rk, cnt = _post(y2, u, s5_d[l][None], s5_glu_w[l].astype(BF16), s5_glu_b[l][None],
                                        na_ctx, na_lat, mla_ctx, mla_lat, w_out[l].astype(BF16), x, mod_l,
                                        norm2_g[l][None], rw, rb)
        pos_km, be, first, nused = _moe_plan(te, rk, cnt)
        ys = _moe_fused(be, first, nused, pos_km, h2, moe_w_gu, moe_b_gu, moe_w_down, moe_b_down, l)
        x = _combine(ys, tg, x1, mod_l)

    y = _final_norm(x, final_norm_g[None])
    y_prompt = y[:T_CTX].reshape(N_CTX, L_CTX, D_MODEL)
    y_sample = y[T_CTX:].reshape(N_LAT, L_LAT, D_MODEL)
    return (y_prompt, y_sample, fin_k, fin_v, fin_m, jnp.stack(s5_re_list, axis=1), jnp.stack(s5_im_list, axis=1))
```

```python
import functools
import math

import numpy as np
import jax
import jax.numpy as jnp
from jax import lax
from jax.experimental import pallas as pl
from jax.experimental.pallas import tpu as pltpu

F32 = jnp.float32
BF16 = jnp.bfloat16

D_MODEL = 1024
N_CTX, L_CTX = 32, 256
N_LAT, L_LAT = 2, 2048
DEPTH = 4
PAST = 256
GRID_W = 64
S5_W, S5_G, S5_P, S5_C = 256, 16, 64, 16
NA_H, NA_D = 4, 64
WIN_H, WIN_W = 8, 16
MLA_H, MLA_NOPE, MLA_ROPE, MLA_V = 8, 64, 32, 64
MLA_QL, MLA_KVL = 256, 128
MLA_SCALE = (MLA_NOPE + MLA_ROPE) ** -0.5
NA_SCALE = NA_D ** -0.5
OFF_NA = S5_W
OFF_QA = OFF_NA + 3 * NA_H * NA_D
OFF_KVA = OFF_QA + MLA_QL
D_IN = OFF_KVA + MLA_KVL + MLA_ROPE
N_EXP, TOP_K, D_FF = 32, 4, 1024
SWIGLU_ALPHA, SWIGLU_LIMIT = 1.702, 7.0
ROPE_BASE = 10000.0
EPS = 1e-6

TM = 256
T_CTX = N_CTX * L_CTX
T_LAT = N_LAT * L_LAT
T_ALL = T_CTX + T_LAT
CTX_TILES = T_CTX // TM
LAT_TILES = L_LAT // TM
N_TILES = T_ALL // TM
MOD_ROWS = 8
MLA_PAD = 128
D_IN_EXT = D_IN + MLA_ROPE
SCAN_ROWS = 8
SCAN_CH = 64
S5_STATE_W = 2 * S5_G * S5_P
N_SEG = SCAN_ROWS // N_LAT
S5_SEG = L_LAT // N_SEG
MOE_BM = 256
MOE_SLOTS = T_ALL * TOP_K
MOE_ROWS = MOE_SLOTS + N_EXP * MOE_BM
MOE_BLOCKS = MOE_ROWS // MOE_BM
ASG_ROWS = 57 * 1024
NEG = -1e30
VMEM_LIMIT = 56 * 1024 * 1024


def _cparams(sem):
    return pltpu.CompilerParams(dimension_semantics=sem, vmem_limit_bytes=VMEM_LIMIT)


def _mod_row(i):
    return jnp.maximum(i - (CTX_TILES - LAT_TILES), 0) // LAT_TILES


def _rope_block(i):
    return jnp.where(i < CTX_TILES, 0, 1 + (i - CTX_TILES) % LAT_TILES)


def _rms(x, g):
    return x * lax.rsqrt(jnp.mean(x * x, axis=-1, keepdims=True) + EPS) * g


def _mod_kernel(cv_ref, w_ref, b_ref, o_ref):
    s = jax.nn.silu(cv_ref[...]).astype(BF16)
    o_ref[0] = jnp.dot(s, w_ref[0].astype(BF16), preferred_element_type=F32) + b_ref[0]


def _modulation(cv, w_mod, b_mod):
    tn = 1536
    return pl.pallas_call(
        _mod_kernel,
        grid=(DEPTH, 6 * D_MODEL // tn),
        in_specs=[pl.BlockSpec((MOD_ROWS, D_MODEL), lambda l, j: (0, 0)),
                  pl.BlockSpec((1, D_MODEL, tn), lambda l, j: (l, 0, j)),
                  pl.BlockSpec((1, 1, tn), lambda l, j: (l, 0, j))],
        out_specs=pl.BlockSpec((1, MOD_ROWS, tn), lambda l, j: (l, 0, j)),
        out_shape=jax.ShapeDtypeStruct((DEPTH, MOD_ROWS, 6 * D_MODEL), F32),
        compiler_params=_cparams(("arbitrary", "arbitrary")),
        name="modulation",
    )(cv, w_mod, b_mod.reshape(DEPTH, 1, 6 * D_MODEL))


def _proj_kernel(x_ref, mod_ref, g1_ref, win_ref, qg_ref, wq_ref, kvg_ref, wkk_ref, wv_ref,
                 cq_ref, sq_ref, ck_ref, sk_ref,
                 u_ref, naq_ref, nak_ref, nav_ref, mq_ref, mk_ref, mv_ref, ckv_ref):
    x = x_ref[...]
    mod = mod_ref[0]
    h = _rms(x, g1_ref[...]) * (1.0 + mod[1:2, :]) + mod[0:1, :]
    p = jnp.dot(h.astype(BF16), win_ref[...], preferred_element_type=F32)
    u_ref[...] = p[:, :OFF_NA]
    for hh in range(NA_H):
        naq_ref[0, hh] = p[:, OFF_NA + hh * NA_D: OFF_NA + (hh + 1) * NA_D].astype(BF16)
        nak_ref[0, hh] = p[:, OFF_NA + NA_H * NA_D + hh * NA_D: OFF_NA + NA_H * NA_D + (hh + 1) * NA_D]
        nav_ref[0, hh] = p[:, OFF_NA + 2 * NA_H * NA_D + hh * NA_D: OFF_NA + 2 * NA_H * NA_D + (hh + 1) * NA_D]
    qn = _rms(p[:, OFF_QA:OFF_KVA], qg_ref[...]).astype(BF16)
    qq = jnp.dot(qn, wq_ref[...], preferred_element_type=F32)
    cq = cq_ref[...]
    sq = sq_ref[...]
    for hh in range(MLA_H):
        qa = qq[:, hh * MLA_PAD:(hh + 1) * MLA_PAD]
        qb = qq[:, (MLA_H + hh) * MLA_PAD:(MLA_H + hh + 1) * MLA_PAD]
        mq_ref[0, hh] = (qa * cq + qb * sq).astype(BF16)
    ckv = _rms(p[:, OFF_KVA:OFF_KVA + MLA_KVL], kvg_ref[...])
    kr = p[:, OFF_KVA + MLA_KVL:D_IN]
    kr_partner = p[:, D_IN:D_IN_EXT]
    kr_rot = kr * ck_ref[...] + kr_partner * sk_ref[...]
    ckv_ref[...] = jnp.concatenate([ckv, kr], axis=-1)
    kin = jnp.concatenate([ckv, kr_rot], axis=-1).astype(BF16)
    kk = jnp.dot(kin, wkk_ref[...], preferred_element_type=F32)
    vv = jnp.dot(kin[:, :MLA_KVL], wv_ref[...], preferred_element_type=F32)
    for hh in range(MLA_H):
        mk_ref[0, hh] = kk[:, hh * MLA_PAD:(hh + 1) * MLA_PAD].astype(BF16)
        mv_ref[0, hh] = vv[:, hh * MLA_V:(hh + 1) * MLA_V].astype(BF16)


def _project(x, mod, g1, w_in_ext, qg, wq2, kvg, wkk, wv, cq, sq, ck, sk, layer):
    lay3 = lambda i: (layer, 0, 0)
    tile2 = lambda i: (i, 0)
    head4 = lambda i: (i, 0, 0, 0)
    rope2 = lambda i: (_rope_block(i), 0)
    return pl.pallas_call(
        _proj_kernel,
        grid=(N_TILES,),
        in_specs=[pl.BlockSpec((TM, D_MODEL), tile2),
                  pl.BlockSpec((None, 1, 6, D_MODEL), lambda i: (layer, _mod_row(i), 0, 0)),
                  pl.BlockSpec((None, 1, D_MODEL), lay3),
                  pl.BlockSpec((None, D_MODEL, D_IN_EXT), lay3),
                  pl.BlockSpec((None, 1, MLA_QL), lay3),
                  pl.BlockSpec((None, MLA_QL, 2 * MLA_H * MLA_PAD), lay3),
                  pl.BlockSpec((None, 1, MLA_KVL), lay3),
                  pl.BlockSpec((None, MLA_KVL + MLA_ROPE, MLA_H * MLA_PAD), lay3),
                  pl.BlockSpec((None, MLA_KVL, MLA_H * MLA_V), lay3),
                  pl.BlockSpec((TM, MLA_PAD), rope2),
                  pl.BlockSpec((TM, MLA_PAD), rope2),
                  pl.BlockSpec((TM, MLA_ROPE), rope2),
                  pl.BlockSpec((TM, MLA_ROPE), rope2)],
        out_specs=[pl.BlockSpec((TM, S5_W), tile2),
                   pl.BlockSpec((1, NA_H, TM, NA_D), head4),
                   pl.BlockSpec((1, NA_H, TM, NA_D), head4),
                   pl.BlockSpec((1, NA_H, TM, NA_D), head4),
                   pl.BlockSpec((1, MLA_H, TM, MLA_PAD), head4),
                   pl.BlockSpec((1, MLA_H, TM, MLA_PAD), head4),
                   pl.BlockSpec((1, MLA_H, TM, MLA_V), head4),
                   pl.BlockSpec((TM, MLA_KVL + MLA_ROPE), tile2)],
        out_shape=[jax.ShapeDtypeStruct((T_ALL, S5_W), F32),
                   jax.ShapeDtypeStruct((N_TILES, NA_H, TM, NA_D), BF16),
                   jax.ShapeDtypeStruct((N_TILES, NA_H, TM, NA_D), F32),
                   jax.ShapeDtypeStruct((N_TILES, NA_H, TM, NA_D), F32),
                   jax.ShapeDtypeStruct((N_TILES, MLA_H, TM, MLA_PAD), BF16),
                   jax.ShapeDtypeStruct((N_TILES, MLA_H, TM, MLA_PAD), BF16),
                   jax.ShapeDtypeStruct((N_TILES, MLA_H, TM, MLA_V), BF16),
                   jax.ShapeDtypeStruct((T_ALL, MLA_KVL + MLA_ROPE), F32)],
        compiler_params=_cparams(("arbitrary",)),
        name="proj",
    )(x, mod, g1, w_in_ext, qg, wq2, kvg, wkk, wv, cq, sq, ck, sk)


def _cache_kv_kernel(c_ref, wkk_ref, wv_ref, k_ref, v_ref):
    cin = c_ref[0, 0].astype(BF16)
    kk = jnp.dot(cin, wkk_ref[0], preferred_element_type=F32)
    vv = jnp.dot(cin[:, :MLA_KVL], wv_ref[0], preferred_element_type=F32)
    for hh in range(MLA_H):
        k_ref[0, 0, hh] = kk[:, hh * MLA_PAD:(hh + 1) * MLA_PAD].astype(BF16)
        v_ref[0, 0, hh] = vv[:, hh * MLA_V:(hh + 1) * MLA_V].astype(BF16)


def _cache_kv(cache_mla_kv, wkk_all, wv_all):
    return pl.pallas_call(
        _cache_kv_kernel,
        grid=(N_LAT, DEPTH),
        in_specs=[pl.BlockSpec((1, 1, PAST, MLA_KVL + MLA_ROPE), lambda n, l: (n, l, 0, 0)),
                  pl.BlockSpec((1, MLA_KVL + MLA_ROPE, MLA_H * MLA_PAD), lambda n, l: (l, 0, 0)),
                  pl.BlockSpec((1, MLA_KVL, MLA_H * MLA_V), lambda n, l: (l, 0, 0))],
        out_specs=[pl.BlockSpec((1, 1, MLA_H, PAST, MLA_PAD), lambda n, l: (n, l, 0, 0, 0)),
                   pl.BlockSpec((1, 1, MLA_H, PAST, MLA_V), lambda n, l: (n, l, 0, 0, 0))],
        out_shape=[jax.ShapeDtypeStruct((N_LAT, DEPTH, MLA_H, PAST, MLA_PAD), BF16),
                   jax.ShapeDtypeStruct((N_LAT, DEPTH, MLA_H, PAST, MLA_V), BF16)],
        compiler_params=_cparams(("arbitrary", "arbitrary")),
        name="cache_kv",
    )(cache_mla_kv, wkk_all, wv_all)


def _s5_disc_kernel(lr_ref, li_ref, ldt_ref, bre_ref, bim_ref, a_ref, bb_ref):
    lr = lr_ref[...]
    li = li_ref[...]
    dt = jnp.exp(ldt_ref[...])
    mag = jnp.exp(lr * dt)
    a_re = mag * jnp.cos(li * dt)
    a_im = mag * jnp.sin(li * dt)
    den = lr * lr + li * li
    nr = a_re - 1.0
    k_re = (nr * lr + a_im * li) / den
    k_im = (a_im * lr - nr * li) / den
    a_ref[0] = a_re
    a_ref[1] = a_im
    bre = bre_ref[...]
    bim = bim_ref[...]
    bb_ref[0] = k_re * bre - k_im * bim
    bb_ref[1] = k_re * bim + k_im * bre


def _s5_discretize(lam_re, lam_im, log_dt, b_re, b_im):
    g2 = 2 * S5_G
    a, bb = pl.pallas_call(
        _s5_disc_kernel,
        out_shape=[jax.ShapeDtypeStruct((2, g2, 1, S5_P), F32),
                   jax.ShapeDtypeStruct((2, g2, S5_C, S5_P), F32)],
        name="s5_disc",
    )(lam_re.reshape(g2, 1, S5_P), lam_im.reshape(g2, 1, S5_P), log_dt.reshape(g2, 1, 1),
      b_re.reshape(g2, S5_P, S5_C).transpose(0, 2, 1), b_im.reshape(g2, S5_P, S5_C).transpose(0, 2, 1))
    return a, bb


def _s5_matrices(a, bb, c_re, c_im):
    gp = S5_G * S5_P
    a_lanes = a.reshape(2, 2, S5_G, S5_P).transpose(1, 0, 2, 3).reshape(2, 2, gp)
    bbar = bb.reshape(2, 2, S5_G, S5_C, S5_P)
    eye = jnp.eye(S5_G, dtype=F32)
    wb = jnp.einsum('xdgcp,gh->dgcxhp', bbar, eye).reshape(2, S5_W, 2 * gp)
    cc = jnp.stack([c_re, -c_im], axis=1)
    wc = jnp.einsum('dxgcp,gh->dxhpgc', cc, eye).reshape(2, 2 * gp, S5_W)
    return a_lanes, wb.astype(BF16), wc.astype(BF16)


def _scan_kernel(u_ref, wb_ref, wc_ref, a_ref, h0_ref, y_ref, ht_ref, x_s, h_s, st_s):
    d = pl.program_id(0)
    c = pl.program_id(2)
    nc = pl.num_programs(2)
    gp = S5_G * S5_P

    @pl.when(c == 0)
    def _():
        st_s[...] = h0_ref[0]

    u = u_ref[...].reshape(SCAN_CH * SCAN_ROWS, S5_W).astype(BF16)
    x_s[...] = jnp.dot(u, wb_ref[0], preferred_element_type=F32)
    a_re = a_ref[0, 0:1, :]
    a_im = a_ref[0, 1:2, :]
    unroll = 4

    def outer(io, carry):
        h_re, h_im = carry
        for ii in range(unroll):
            i = io * unroll + ii
            t = jnp.where(d == 0, i, SCAN_CH - 1 - i)
            r = pl.multiple_of(t * SCAN_ROWS, SCAN_ROWS)
            x_re = x_s[pl.ds(r, SCAN_ROWS), pl.ds(0, gp)]
            x_im = x_s[pl.ds(r, SCAN_ROWS), pl.ds(gp, gp)]
            n_re = a_re * h_re - a_im * h_im + x_re
            n_im = a_re * h_im + a_im * h_re + x_im
            h_s[pl.ds(r, SCAN_ROWS), pl.ds(0, gp)] = n_re
            h_s[pl.ds(r, SCAN_ROWS), pl.ds(gp, gp)] = n_im
            h_re, h_im = n_re, n_im
        return h_re, h_im

    h_re, h_im = lax.fori_loop(0, SCAN_CH // unroll, outer, (st_s[:, pl.ds(0, gp)], st_s[:, pl.ds(gp, gp)]))
    st_s[:, pl.ds(0, gp)] = h_re
    st_s[:, pl.ds(gp, gp)] = h_im
    y = jnp.dot(h_s[...].astype(BF16), wc_ref[0], preferred_element_type=F32)
    y_ref[0] = y.reshape(SCAN_CH, SCAN_ROWS, S5_W)

    @pl.when(c == nc - 1)
    def _():
        ht_ref[0] = st_s[...]


def _s5_scan(u_tm, wb, wc, a_lanes, h0):
    L, n, _ = u_tm.shape
    nc = L // SCAN_CH
    ng = n // SCAN_ROWS

    def ceff(d, c):
        return c + d * (nc - 1 - 2 * c)

    return pl.pallas_call(
        _scan_kernel,
        grid=(2, ng, nc),
        in_specs=[pl.BlockSpec((SCAN_CH, SCAN_ROWS, S5_W), lambda d, g, c: (ceff(d, c), g, 0)),
                  pl.BlockSpec((1, S5_W, S5_STATE_W), lambda d, g, c: (d, 0, 0)),
                  pl.BlockSpec((1, S5_STATE_W, S5_W), lambda d, g, c: (d, 0, 0)),
                  pl.BlockSpec((1, 2, S5_G * S5_P), lambda d, g, c: (d, 0, 0)),
                  pl.BlockSpec((1, SCAN_ROWS, S5_STATE_W), lambda d, g, c: (d, g, 0))],
        out_specs=[pl.BlockSpec((1, SCAN_CH, SCAN_ROWS, S5_W), lambda d, g, c: (d, ceff(d, c), g, 0)),
                   pl.BlockSpec((1, SCAN_ROWS, S5_STATE_W), lambda d, g, c: (d, g, 0))],
        out_shape=[jax.ShapeDtypeStruct((2, L, n, S5_W), F32),
                   jax.ShapeDtypeStruct((2, n, S5_STATE_W), F32)],
        scratch_shapes=[pltpu.VMEM((SCAN_CH * SCAN_ROWS, S5_STATE_W), F32),
                        pltpu.VMEM((SCAN_CH * SCAN_ROWS, S5_STATE_W), F32),
                        pltpu.VMEM((SCAN_ROWS, S5_STATE_W), F32)],
        compiler_params=_cparams(("arbitrary", "arbitrary", "arbitrary")),
        name="s5_scan",
    )(u_tm, wb, wc, a_lanes, h0)


def _s5_pow_kernel(a_ref, p_ref):
    gp = S5_G * S5_P
    a_re = a_ref[0, 0:1, :]
    a_im = a_ref[0, 1:2, :]
    row = lax.broadcasted_iota(jnp.int32, (8, gp), 0)
    cur_re = jnp.broadcast_to(a_re, (8, gp))
    cur_im = jnp.broadcast_to(a_im, (8, gp))
    pr, pi = a_re, a_im
    for k in range(1, 8):
        pr, pi = pr * a_re - pi * a_im, pr * a_im + pi * a_re
        cur_re = jnp.where(row == k, pr, cur_re)
        cur_im = jnp.where(row == k, pi, cur_im)
    p_ref[0, 0, 0:8, :] = cur_re
    p_ref[0, 1, 0:8, :] = cur_im
    m = 8
    while m < S5_SEG:
        am_re = p_ref[0, 0, m - 1:m, :]
        am_im = p_ref[0, 1, m - 1:m, :]
        lo_re = p_ref[0, 0, 0:m, :]
        lo_im = p_ref[0, 1, 0:m, :]
        p_ref[0, 0, m:2 * m, :] = lo_re * am_re - lo_im * am_im
        p_ref[0, 1, m:2 * m, :] = lo_re * am_im + lo_im * am_re
        m *= 2


def _s5_powers(a_lanes):
    gp = S5_G * S5_P
    return pl.pallas_call(
        _s5_pow_kernel,
        grid=(2,),
        in_specs=[pl.BlockSpec((1, 2, gp), lambda d: (d, 0, 0))],
        out_specs=pl.BlockSpec((1, 2, S5_SEG, gp), lambda d: (d, 0, 0, 0)),
        out_shape=jax.ShapeDtypeStruct((2, 2, S5_SEG, gp), F32),
        compiler_params=_cparams(("arbitrary",)),
        name="s5_powers",
    )(a_lanes)


def _s5_fix_kernel(y_in, p_ref, ht_ref, wc_ref, y_out, hin_s):
    gp = S5_G * S5_P
    d = pl.program_id(0)
    r = pl.program_id(1)

    @pl.when(r == 0)
    def _():
        hin_s[...] = jnp.zeros_like(hin_s)

        def chain(order, pw_row):
            aw_re = p_ref[0, 0, pw_row:pw_row + 1, :]
            aw_im = p_ref[0, 1, pw_row:pw_row + 1, :]
            for n in range(N_LAT):
                h_re = jnp.zeros((1, gp), F32)
                h_im = jnp.zeros((1, gp), F32)
                for s_prev, s_cur in zip(order[:-1], order[1:]):
                    e_re = ht_ref[0, n * N_SEG + s_prev:n * N_SEG + s_prev + 1, 0:gp]
                    e_im = ht_ref[0, n * N_SEG + s_prev:n * N_SEG + s_prev + 1, gp:2 * gp]
                    h_re, h_im = aw_re * h_re - aw_im * h_im + e_re, aw_re * h_im + aw_im * h_re + e_im
                    hin_s[n * N_SEG + s_cur, :, 0:gp] = h_re
                    hin_s[n * N_SEG + s_cur, :, gp:2 * gp] = h_im

        @pl.when(d == 0)
        def _():
            chain(list(range(N_SEG)), S5_SEG - 1)

        @pl.when(d == 1)
        def _():
            chain(list(range(N_SEG - 1, -1, -1)), 0)

    hin = hin_s[r]
    h_re = hin[:, 0:gp]
    h_im = hin[:, gp:2 * gp]
    p_re = p_ref[0, 0]
    p_im = p_ref[0, 1]
    z = jnp.concatenate([p_re * h_re - p_im * h_im, p_re * h_im + p_im * h_re], axis=-1).astype(BF16)
    y_out[0] = y_in[0] + jnp.dot(z, wc_ref[0], preferred_element_type=F32)


def _s5_fix(y_tok, ptab, ht_loc, wc):
    gp = S5_G * S5_P
    return pl.pallas_call(
        _s5_fix_kernel,
        grid=(2, SCAN_ROWS),
        in_specs=[pl.BlockSpec((1, S5_SEG, S5_W), lambda d, r: (d, r, 0)),
                  pl.BlockSpec((1, 2, S5_SEG, gp), lambda d, r: (d, 0, 0, 0)),
                  pl.BlockSpec((1, SCAN_ROWS, S5_STATE_W), lambda d, r: (d, 0, 0)),
                  pl.BlockSpec((1, S5_STATE_W, S5_W), lambda d, r: (d, 0, 0))],
        out_specs=pl.BlockSpec((1, S5_SEG, S5_W), lambda d, r: (d, r, 0)),
        out_shape=jax.ShapeDtypeStruct(y_tok.shape, F32),
        scratch_shapes=[pltpu.VMEM((SCAN_ROWS, 1, S5_STATE_W), F32)],
        input_output_aliases={0: 0},
        compiler_params=_cparams(("arbitrary", "arbitrary")),
        name="s5_fix",
    )(y_tok, ptab, ht_loc, wc)


def _qk(q, k):
    return lax.dot_general(q, k, (((1,), (1,)), ((), ())), preferred_element_type=F32)


def _softmax_pv(s_list, v_list):
    m = s_list[0].max(axis=-1, keepdims=True)
    for s in s_list[1:]:
        m = jnp.maximum(m, s.max(axis=-1, keepdims=True))
    den = None
    acc = None
    for s, v in zip(s_list, v_list):
        e = jnp.exp(s - m)
        ds = e.sum(axis=-1, keepdims=True)
        pv = jnp.dot(e.astype(BF16), v, preferred_element_type=F32)
        den = ds if den is None else den + ds
        acc = pv if acc is None else acc + pv
    return acc / den


def _ctx_attn_kernel(nq_ref, nk_ref, nv_ref, mq_ref, mk_ref, mv_ref, ckv_ref, fk_in, fv_in, fm_in,
                     na_ref, mla_ref, fk_ref, fv_ref, fm_ref):
    del fk_in, fv_in, fm_in
    fk_ref[0, 0] = nk_ref[0]
    fv_ref[0, 0] = nv_ref[0]
    fm_ref[0, 0] = ckv_ref[...]
    for hh in range(NA_H):
        s = _qk(nq_ref[0, hh], nk_ref[0, hh].astype(BF16)) * NA_SCALE
        na_ref[:, hh * NA_D:(hh + 1) * NA_D] = _softmax_pv([s], [nv_ref[0, hh].astype(BF16)])
    for hh in range(MLA_H):
        s = _qk(mq_ref[0, hh], mk_ref[0, hh]) * MLA_SCALE
        mla_ref[:, hh * MLA_V:(hh + 1) * MLA_V] = _softmax_pv([s], [mv_ref[0, hh]])


def _ctx_attention(naq, nak, nav, mq, mk, mv, ckv, fin_k, fin_v, fin_m, layer):
    head4 = lambda i: (i, 0, 0, 0)
    tile2 = lambda i: (i, 0)
    anyspec = pl.BlockSpec(memory_space=pl.ANY)
    return pl.pallas_call(
        _ctx_attn_kernel,
        grid=(CTX_TILES,),
        in_specs=[pl.BlockSpec((1, NA_H, TM, NA_D), head4)] * 3
                 + [pl.BlockSpec((1, MLA_H, TM, MLA_PAD), head4)] * 2
                 + [pl.BlockSpec((1, MLA_H, TM, MLA_V), head4),
                    pl.BlockSpec((TM, MLA_KVL + MLA_ROPE), tile2), anyspec, anyspec, anyspec],
        out_specs=[pl.BlockSpec((TM, NA_H * NA_D), tile2), pl.BlockSpec((TM, MLA_H * MLA_V), tile2),
                   pl.BlockSpec((1, 1, NA_H, TM, NA_D), lambda i: (i, layer, 0, 0, 0)),
                   pl.BlockSpec((1, 1, NA_H, TM, NA_D), lambda i: (i, layer, 0, 0, 0)),
                   pl.BlockSpec((1, 1, TM, MLA_KVL + MLA_ROPE), lambda i: (i, layer, 0, 0))],
        out_shape=[jax.ShapeDtypeStruct((T_CTX, NA_H * NA_D), F32),
                   jax.ShapeDtypeStruct((T_CTX, MLA_H * MLA_V), F32),
                   jax.ShapeDtypeStruct(fin_k.shape, F32),
                   jax.ShapeDtypeStruct(fin_v.shape, F32),
                   jax.ShapeDtypeStruct(fin_m.shape, F32)],
        input_output_aliases={7: 2, 8: 3, 9: 4},
        compiler_params=_cparams(("arbitrary",)),
        name="ctx_attn",
    )(naq, nak, nav, mq, mk, mv, ckv, fin_k, fin_v, fin_m)


def _na_lat_kernel(q_ref, k0_ref, k1_ref, k2_ref, v0_ref, v1_ref, v2_ref, kc_ref, vc_ref, b_ref, o_ref):
    for hh in range(NA_H):
        q = q_ref[0, hh]
        kb = jnp.concatenate([k0_ref[0, hh], k1_ref[0, hh], k2_ref[0, hh]], axis=0).astype(BF16)
        vb = jnp.concatenate([v0_ref[0, hh], v1_ref[0, hh], v2_ref[0, hh]], axis=0).astype(BF16)
        s_win = _qk(q, kb) * NA_SCALE + b_ref[0, 0, hh]
        s_ctx = _qk(q, kc_ref[0, 0, hh].astype(BF16)) * NA_SCALE
        o_ref[:, hh * NA_D:(hh + 1) * NA_D] = _softmax_pv([s_win, s_ctx], [vb, vc_ref[0, 0, hh].astype(BF16)])


def _band_start(j):
    return jnp.clip(j - 1, 0, LAT_TILES - 3)


def _na_latent(naq, nak, nav, cache_k, cache_v, bias, layer):
    def qmap(n, j):
        return (CTX_TILES + n * LAT_TILES + j, 0, 0, 0)

    def kmap(off):
        return lambda n, j: (CTX_TILES + n * LAT_TILES + _band_start(j) + off, 0, 0, 0)

    def bmap(n, j):
        return (layer, jnp.where(j == 0, 0, jnp.where(j == LAT_TILES - 1, 2, 1)), 0, 0, 0)

    blk = (1, NA_H, TM, NA_D)
    return pl.pallas_call(
        _na_lat_kernel,
        grid=(N_LAT, LAT_TILES),
        in_specs=[pl.BlockSpec(blk, qmap)]
                 + [pl.BlockSpec(blk, kmap(o)) for o in range(3)]
                 + [pl.BlockSpec(blk, kmap(o)) for o in range(3)]
                 + [pl.BlockSpec((1, 1, NA_H, PAST, NA_D), lambda n, j: (n, layer, 0, 0, 0))] * 2
                 + [pl.BlockSpec((1, 1, NA_H, TM, 3 * TM), bmap)],
        out_specs=pl.BlockSpec((TM, NA_H * NA_D), lambda n, j: (n * LAT_TILES + j, 0)),
        out_shape=jax.ShapeDtypeStruct((T_LAT, NA_H * NA_D), F32),
        compiler_params=_cparams(("arbitrary", "arbitrary")),
        name="na_latent",
    )(naq, nak, nak, nak, nav, nav, nav, cache_k, cache_v, bias)


def _na_bias_tables(rpb_all):
    rows = L_LAT // GRID_W
    nr, ncol = 2 * WIN_H - 1, 2 * WIN_W - 1
    w = np.arange(GRID_W)
    cs = np.clip(w - WIN_W // 2, 0, GRID_W - WIN_W)
    cc = np.arange(GRID_W)
    col_ok = (cc[None, :] >= cs[:, None]) & (cc[None, :] < cs[:, None] + WIN_W)
    dc = cc[None, :] - w[:, None] + WIN_W - 1
    col_sel = (dc[:, :, None] == np.arange(ncol)) & col_ok[:, :, None]
    row_sel, oks = [], []
    for j, s in ((0, 0), (1, 0), (LAT_TILES - 1, LAT_TILES - 3)):
        r = 4 * j + np.arange(4)
        rs = np.clip(r - WIN_H // 2, 0, rows - WIN_H)
        krow = 4 * s + np.arange(12)
        row_ok = (krow[None, :] >= rs[:, None]) & (krow[None, :] < rs[:, None] + WIN_H)
        dr = krow[None, :] - r[:, None] + WIN_H - 1
        row_sel.append((dr[:, :, None] == np.arange(nr)) & row_ok[:, :, None])
        oks.append(row_ok[:, None, :, None] & col_ok[None, :, None, :])
    row_sel = jnp.asarray(np.stack(row_sel), F32)
    col_sel = jnp.asarray(col_sel, F32)
    mask = jnp.asarray(np.where(np.stack(oks), 0.0, NEG).reshape(3, TM, 3 * TM), F32)
    b = jnp.einsum('prka,lhab,wcb->lphrwkc', row_sel, rpb_all, col_sel, precision=lax.Precision.HIGHEST)
    return b.reshape(DEPTH, 3, NA_H, TM, 3 * TM) + mask[None, :, None]


def _mla_lat_kernel(q_ref, k_ref, v_ref, kc_ref, vc_ref, o_ref):
    for hh in range(MLA_H):
        q = q_ref[0, hh]
        k = k_ref[:, hh].reshape(L_LAT, MLA_PAD)
        v = v_ref[:, hh].reshape(L_LAT, MLA_V)
        s_lat = _qk(q, k) * MLA_SCALE
        s_ctx = _qk(q, kc_ref[0, 0, hh]) * MLA_SCALE
        o_ref[:, hh * MLA_V:(hh + 1) * MLA_V] = _softmax_pv([s_lat, s_ctx], [v, vc_ref[0, 0, hh]])


def _mla_latent(mq, mk, mv, kc, vc, layer):
    seq_blk = CTX_TILES // LAT_TILES
    return pl.pallas_call(
        _mla_lat_kernel,
        grid=(N_LAT, LAT_TILES),
        in_specs=[pl.BlockSpec((1, MLA_H, TM, MLA_PAD), lambda n, j: (CTX_TILES + n * LAT_TILES + j, 0, 0, 0)),
                  pl.BlockSpec((LAT_TILES, MLA_H, TM, MLA_PAD), lambda n, j: (seq_blk + n, 0, 0, 0)),
                  pl.BlockSpec((LAT_TILES, MLA_H, TM, MLA_V), lambda n, j: (seq_blk + n, 0, 0, 0)),
                  pl.BlockSpec((1, 1, MLA_H, PAST, MLA_PAD), lambda n, j: (n, layer, 0, 0, 0)),
                  pl.BlockSpec((1, 1, MLA_H, PAST, MLA_V), lambda n, j: (n, layer, 0, 0, 0))],
        out_specs=pl.BlockSpec((TM, MLA_H * MLA_V), lambda n, j: (n * LAT_TILES + j, 0)),
        out_shape=jax.ShapeDtypeStruct((T_LAT, MLA_H * MLA_V), F32),
        compiler_params=_cparams(("arbitrary", "arbitrary")),
        name="mla_latent",
    )(mq, mk, mv, kc, vc)


def _post_kernel(y_ref, u_ref, d_ref, gw_ref, gb_ref, nac_ref, nal_ref, mlac_ref, mlal_ref, wo_ref, x_ref, mod_ref,
                 g2_ref, rw_ref, rb_ref, x1_ref, h2_ref, te_ref, tg_ref, rk_ref, cnt_ref):
    ys = y_ref[0] + y_ref[1] + u_ref[...] * d_ref[...]
    g = jax.nn.gelu(ys)
    s5o = g * jax.nn.sigmoid(jnp.dot(g.astype(BF16), gw_ref[...], preferred_element_type=F32) + gb_ref[...])
    is_ctx = pl.program_id(0) < CTX_TILES
    na = jnp.where(is_ctx, nac_ref[...], nal_ref[...])
    mla = jnp.where(is_ctx, mlac_ref[...], mlal_ref[...])
    out = (jnp.dot(s5o.astype(BF16), wo_ref[0:S5_W, :], preferred_element_type=F32)
           + jnp.dot(na.astype(BF16), wo_ref[S5_W:S5_W + NA_H * NA_D, :], preferred_element_type=F32)
           + jnp.dot(mla.astype(BF16), wo_ref[S5_W + NA_H * NA_D:, :], preferred_element_type=F32))
    mod = mod_ref[0]
    x1 = x_ref[...] + mod[2:3, :] * out
    x1_ref[...] = x1
    h2 = _rms(x1, g2_ref[...]) * (1.0 + mod[4:5, :]) + mod[3:4, :]
    h2_ref[...] = h2
    h_hi = h2.astype(BF16)
    h_lo = (h2 - h_hi.astype(F32)).astype(BF16)
    logits = (jnp.dot(h_hi, rw_ref[0], preferred_element_type=F32)
              + jnp.dot(h_hi, rw_ref[1], preferred_element_type=F32)
              + jnp.dot(h_lo, rw_ref[0], preferred_element_type=F32)) + rb_ref[...]
    lane_i = lax.broadcasted_iota(jnp.int32, (TM, 128), 1)
    lane = lane_i.astype(F32)
    cur = jnp.where(lane_i < N_EXP, logits, -jnp.inf)
    te = jnp.zeros((TM, 128), F32)
    tv = jnp.zeros((TM, 128), F32)
    hot = jnp.zeros((TM, 128), F32)
    idxs = []
    top = None
    for k in range(TOP_K):
        m = cur.max(axis=-1, keepdims=True)
        idx = jnp.where(cur == m, lane, 128.0).min(axis=-1, keepdims=True)
        sel = lane == idx
        top = m if top is None else top
        idxs.append(idx)
        te = jnp.where(lane_i == k, idx, te)
        tv = jnp.where(lane_i == k, jnp.exp(m - top), tv)
        hot = jnp.where(sel, 1.0, hot)
        cur = jnp.where(sel, -jnp.inf, cur)
    te_ref[...] = te.astype(jnp.int32)
    tg_ref[...] = tv / tv.sum(axis=-1, keepdims=True)
    row = lax.broadcasted_iota(jnp.int32, (TM, TM), 0)
    col = lax.broadcasted_iota(jnp.int32, (TM, TM), 1)
    tri = jnp.where(col < row, 1.0, 0.0).astype(BF16)
    before = jnp.dot(tri, hot.astype(BF16), preferred_element_type=F32)
    rk = jnp.zeros((TM, 128), F32)
    for k in range(TOP_K):
        rk = jnp.where(lane_i == k, jnp.where(lane == idxs[k], before, 0.0).sum(axis=-1, keepdims=True), rk)
    rk_ref[...] = rk.astype(jnp.int32)
    cnt_ref[0] = jnp.broadcast_to(hot.sum(axis=0, keepdims=True), (8, 128)).astype(jnp.int32)


def _post(y2, u, s5_d, glu_w, glu_b, na_ctx, na_lat, mla_ctx, mla_lat, w_out, x, mod, g2, rw, rb, layer):
    lay3 = lambda i: (layer, 0, 0)
    tile2 = lambda i: (i, 0)
    ctx2 = lambda i: (jnp.minimum(i, CTX_TILES - 1), 0)
    lat2 = lambda i: (jnp.maximum(i - CTX_TILES, 0), 0)
    return pl.pallas_call(
        _post_kernel,
        grid=(N_TILES,),
        in_specs=[pl.BlockSpec((2, TM, S5_W), lambda i: (0, i, 0)),
                  pl.BlockSpec((TM, S5_W), tile2),
                  pl.BlockSpec((None, 1, S5_W), lay3),
                  pl.BlockSpec((None, S5_W, S5_W), lay3),
                  pl.BlockSpec((None, 1, S5_W), lay3),
                  pl.BlockSpec((TM, NA_H * NA_D), ctx2),
                  pl.BlockSpec((TM, NA_H * NA_D), lat2),
                  pl.BlockSpec((TM, MLA_H * MLA_V), ctx2),
                  pl.BlockSpec((TM, MLA_H * MLA_V), lat2),
                  pl.BlockSpec((None, D_MODEL, D_MODEL), lay3),
                  pl.BlockSpec((TM, D_MODEL), tile2),
                  pl.BlockSpec((None, 1, 6, D_MODEL), lambda i: (layer, _mod_row(i), 0, 0)),
                  pl.BlockSpec((None, 1, D_MODEL), lay3),
                  pl.BlockSpec((None, 2, D_MODEL, 128), lambda i: (layer, 0, 0, 0)),
                  pl.BlockSpec((None, 1, 128), lay3)],
        out_specs=[pl.BlockSpec((TM, D_MODEL), tile2),
                   pl.BlockSpec((TM, D_MODEL), tile2),
                   pl.BlockSpec((TM, 128), tile2),
                   pl.BlockSpec((TM, 128), tile2),
                   pl.BlockSpec((TM, 128), tile2),
                   pl.BlockSpec((1, 8, 128), lambda i: (i, 0, 0))],
        out_shape=[jax.ShapeDtypeStruct((T_ALL, D_MODEL), F32),
                   jax.ShapeDtypeStruct((T_ALL, D_MODEL), F32),
                   jax.ShapeDtypeStruct((T_ALL, 128), jnp.int32),
                   jax.ShapeDtypeStruct((T_ALL, 128), F32),
                   jax.ShapeDtypeStruct((T_ALL, 128), jnp.int32),
                   jax.ShapeDtypeStruct((N_TILES, 8, 128), jnp.int32)],
        compiler_params=_cparams(("arbitrary",)),
        name="post",
    )(y2, u, s5_d, glu_w, glu_b, na_ctx, na_lat, mla_ctx, mla_lat, w_out, x, mod, g2, rw, rb)


def _moe_fused_kernel(be_ref, first_ref, nused_ref, pos_ref,
                      dflt_hbm, h_hbm, wgu_ref, bgu_ref, wd_ref, bd_ref, ys_hbm,
                      asg_s, xs0, xs1, yo0, yo1, wgu_s, wd_s, act_s, gsem, ssem, tsem):
    b = pl.program_id(0)
    nused = nused_ref[0]
    xs = (xs0, xs1)
    yo = (yo0, yo1)

    def wait_gather(p):
        pltpu.make_async_copy(h_hbm.at[pl.ds(0, MOE_BM), :], xs[p], gsem.at[p]).wait()

    def wait_scatter(p):
        pltpu.make_async_copy(yo[p], ys_hbm.at[pl.ds(0, MOE_BM), :], ssem.at[p]).wait()

    def issue_gather(blk, p):
        r0 = (blk + 1) * MOE_BM
        for j in range(MOE_BM):
            tok = asg_s[r0 + j] >> 16
            pltpu.make_async_copy(h_hbm.at[pl.ds(tok, 1), :], xs[p].at[pl.ds(j, 1), :],
                                  gsem.at[p]).start(priority=j % 2)

    def issue_scatter(blk, p, lo=0, hi=MOE_BM):
        r0 = (blk + 1) * MOE_BM
        for j in range(lo, hi):
            d = asg_s[r0 + j] & 0xFFFF
            pltpu.make_async_copy(yo[p].at[pl.ds(j, 1), :], ys_hbm.at[pl.ds(d, 1), :],
                                  ssem.at[p]).start(priority=j % 2)

    @pl.when(b == 0)
    def _():
        cp = pltpu.make_async_copy(dflt_hbm, asg_s, tsem)
        cp.start()
        cp.wait()

        def inv(i, carry):
            a0 = i * 8
            v0 = a0 | ((((a0 >> 10) << 8) + (a0 & (TM - 1))) << 16)
            for uu in range(8):
                asg_s[pos_ref[a0 + uu]] = v0 + uu * ((1 << 16) + 1)
            return carry
        lax.fori_loop(0, MOE_SLOTS // 8, inv, 0)
        yo1[...] = jnp.zeros_like(yo1)
        zc = pltpu.make_async_copy(yo1, ys_hbm.at[pl.ds(MOE_SLOTS, MOE_BM), :], tsem)
        zc.start()
        zc.wait()
        issue_gather(0, 0)

    def step(p):
        q = 1 - p
        wait_gather(p)
        if p == 0:
            @pl.when(b >= 1)
            def _():
                wait_scatter(p)
        else:
            wait_scatter(p)

        @pl.when(first_ref[b] == 1)
        def _():
            wgu_s[...] = wgu_ref[0, 0].astype(BF16)
            wd_s[...] = wd_ref[0, 0].astype(BF16)

        issue_gather(jnp.minimum(b + 1, nused - 1), q)
        issue_scatter(b - 1, q, 0, MOE_BM // 2)
        gu = jnp.dot(xs[p][...].astype(BF16), wgu_s[...], preferred_element_type=F32) + bgu_ref[0, 0]
        gate = jnp.minimum(gu[:, :D_FF], SWIGLU_LIMIT)
        up = jnp.clip(gu[:, D_FF:], -SWIGLU_LIMIT, SWIGLU_LIMIT)
        act_s[...] = ((up + 1.0) * (gate * jax.nn.sigmoid(SWIGLU_ALPHA * gate))).astype(BF16)

        @pl.when(b >= 0)
        def _():
            issue_scatter(b - 1, q, MOE_BM // 2, MOE_BM)
            yo[p][...] = jnp.dot(act_s[...], wd_s[...], preferred_element_type=F32) + bd_ref[0, 0]

    def drain(p):
        q = 1 - p
        wait_gather(p)
        wait_scatter(p)
        issue_scatter(nused - 1, q)
        wait_scatter(q)

    for p in range(2):
        @pl.when((b < nused) & (b % 2 == p))
        def _():
            step(p)

        @pl.when((b == nused) & (b % 2 == p))
        def _():
            drain(p)


def _moe_fused(block_e, first, nused, pos_km, h2, w_gu, b_gu, w_down, b_down, layer):
    def wmap(b, be, fi, nu, pos):
        return (layer, be[b], 0, 0)

    r = jnp.arange(ASG_ROWS, dtype=jnp.int32)
    dflt = MOE_SLOTS + (((r >> 8) + 1) & 1) * MOE_BM + (r & (MOE_BM - 1))

    return pl.pallas_call(
        _moe_fused_kernel,
        grid_spec=pltpu.PrefetchScalarGridSpec(
            num_scalar_prefetch=4, grid=(MOE_BLOCKS + 1,),
            in_specs=[pl.BlockSpec(memory_space=pl.ANY), pl.BlockSpec(memory_space=pl.ANY),
                      pl.BlockSpec((1, 1, D_MODEL, 2 * D_FF), wmap),
                      pl.BlockSpec((1, 1, 1, 2 * D_FF), wmap),
                      pl.BlockSpec((1, 1, D_FF, D_MODEL), wmap),
                      pl.BlockSpec((1, 1, 1, D_MODEL), wmap)],
            out_specs=pl.BlockSpec(memory_space=pl.ANY),
            scratch_shapes=[pltpu.SMEM((ASG_ROWS,), jnp.int32),
                            pltpu.VMEM((MOE_BM, D_MODEL), F32), pltpu.VMEM((MOE_BM, D_MODEL), F32),
                            pltpu.VMEM((MOE_BM, D_MODEL), F32), pltpu.VMEM((MOE_BM, D_MODEL), F32),
                            pltpu.VMEM((D_MODEL, 2 * D_FF), BF16), pltpu.VMEM((D_FF, D_MODEL), BF16),
                            pltpu.VMEM((MOE_BM, D_FF), BF16),
                            pltpu.SemaphoreType.DMA((2,)), pltpu.SemaphoreType.DMA((2,)),
                            pltpu.SemaphoreType.DMA(())]),
        out_shape=jax.ShapeDtypeStruct((MOE_SLOTS + 2 * MOE_BM, D_MODEL), F32),
        compiler_params=_cparams(("arbitrary",)),
        name="moe_fused",
    )(block_e, first, nused, pos_km, dflt, h2, w_gu, b_gu.reshape(DEPTH, N_EXP, 1, 2 * D_FF), w_down,
      b_down.reshape(DEPTH, N_EXP, 1, D_MODEL))


def _combine_kernel(ys_ref, tg_ref, x1_ref, mod_ref, *rest):
    tg = tg_ref[...]
    y = ys_ref[0:TM, :] * tg[:, 0:1]
    for k in range(1, TOP_K):
        y = y + ys_ref[k * TM:(k + 1) * TM, :] * tg[:, k:k + 1]
    x2 = x1_ref[...] + mod_ref[0][5:6, :] * y
    if len(rest) == 2:
        g_ref, o_ref = rest
        o_ref[...] = _rms(x2, g_ref[...])
    else:
        rest[0][...] = x2


def _combine(ys, tg, x1, mod, layer, tile0=0, ntiles=N_TILES, final_g=None):
    off = lambda i: (i + tile0, 0)
    in_specs = [pl.BlockSpec((TOP_K * TM, D_MODEL), off),
                pl.BlockSpec((TM, 128), off),
                pl.BlockSpec((TM, D_MODEL), off),
                pl.BlockSpec((None, 1, 6, D_MODEL), lambda i: (layer, _mod_row(i + tile0), 0, 0))]
    args = [ys, tg, x1, mod]
    if final_g is not None:
        in_specs.append(pl.BlockSpec((1, D_MODEL), lambda i: (0, 0)))
        args.append(final_g)
    return pl.pallas_call(
        _combine_kernel,
        grid=(ntiles,),
        in_specs=in_specs,
        out_specs=pl.BlockSpec((TM, D_MODEL), lambda i: (i, 0)),
        out_shape=jax.ShapeDtypeStruct((ntiles * TM, D_MODEL), F32),
        compiler_params=_cparams(("arbitrary",)),
        name="moe_combine",
    )(*args)


def _moe_plan(te, rk, cnt):
    tile_cnt = cnt[:, 0, :N_EXP]
    counts = tile_cnt.sum(axis=0)
    padded = (counts + MOE_BM - 1) // MOE_BM * MOE_BM
    pad_end = jnp.cumsum(padded)
    pad_start = pad_end - padded
    tile_off = jnp.cumsum(tile_cnt, axis=0) - tile_cnt
    base = (pad_start[None, :] + tile_off).astype(jnp.int32)
    e = te[:, :TOP_K].reshape(N_TILES, TM, TOP_K)
    hot = e[..., None] == jnp.arange(N_EXP, dtype=jnp.int32)
    pos = jnp.sum(jnp.where(hot, base[:, None, None, :], 0), axis=-1) + rk[:, :TOP_K].reshape(N_TILES, TM, TOP_K)
    pos_km = (pos.transpose(0, 2, 1).reshape(-1) + MOE_BM).astype(jnp.int32)
    nused = (pad_end[-1] // MOE_BM).astype(jnp.int32)
    blk = jnp.arange(MOE_BLOCKS + 1, dtype=jnp.int32)
    be = jnp.minimum(jnp.sum((pad_end[None, :] <= (blk * MOE_BM)[:, None]).astype(jnp.int32), axis=1), N_EXP - 1)
    be = jnp.where(blk < nused, be, be[nused - 1])
    first = jnp.concatenate([jnp.ones((1,), jnp.int32), (be[1:] != be[:-1]).astype(jnp.int32)])
    return pos_km, be, first, nused.reshape(1)


_ROPE_PERM = np.concatenate([np.arange(8, 16), np.arange(0, 8), np.arange(24, 32), np.arange(16, 24)])


def _rope_tables():
    half = MLA_ROPE // 2
    t = jnp.arange(L_LAT)
    row = (t // GRID_W).astype(F32)
    col = (t % GRID_W).astype(F32)
    inv = ROPE_BASE ** (-jnp.arange(0, half, 2, dtype=F32) / half)

    def part(pos):
        ang = pos[:, None] * inv[None, :]
        c, s = jnp.cos(ang), jnp.sin(ang)
        return jnp.concatenate([c, c], axis=-1), jnp.concatenate([-s, s], axis=-1)

    cr, sr = part(row)
    cc, sc = part(col)
    cos32 = jnp.concatenate([cr, cc], axis=-1)
    sin32 = jnp.concatenate([sr, sc], axis=-1)
    ck = jnp.concatenate([jnp.ones((TM, MLA_ROPE), F32), cos32], axis=0)
    sk = jnp.concatenate([jnp.zeros((TM, MLA_ROPE), F32), sin32], axis=0)
    padw = MLA_PAD - MLA_NOPE - MLA_ROPE
    cq = jnp.concatenate([jnp.ones((TM + L_LAT, MLA_NOPE), F32), ck, jnp.ones((TM + L_LAT, padw), F32)], axis=-1)
    sq = jnp.concatenate([jnp.zeros((TM + L_LAT, MLA_NOPE), F32), sk, jnp.zeros((TM + L_LAT, padw), F32)], axis=-1)
    return cq, sq, ck, sk


def _mla_weights(w_qb, w_kvb):
    dq = MLA_NOPE + MLA_ROPE
    wq = w_qb.reshape(MLA_QL, MLA_H, dq)
    zpad = jnp.zeros((MLA_QL, MLA_H, MLA_PAD - dq), F32)
    q_main = jnp.concatenate([wq, zpad], axis=-1)
    q_part = jnp.concatenate([jnp.zeros((MLA_QL, MLA_H, MLA_NOPE), F32), wq[:, :, MLA_NOPE + _ROPE_PERM], zpad], axis=-1)
    wq2 = jnp.concatenate([q_main.reshape(MLA_QL, -1), q_part.reshape(MLA_QL, -1)], axis=-1).astype(BF16)
    wkv = w_kvb.reshape(MLA_KVL, MLA_H, MLA_NOPE + MLA_V)
    k_top = jnp.concatenate([wkv[:, :, :MLA_NOPE], jnp.zeros((MLA_KVL, MLA_H, MLA_PAD - MLA_NOPE), F32)], axis=-1)
    place = jnp.concatenate([jnp.zeros((MLA_ROPE, MLA_NOPE), F32), jnp.eye(MLA_ROPE, dtype=F32),
                             jnp.zeros((MLA_ROPE, MLA_PAD - dq), F32)], axis=-1)
    k_bot = jnp.broadcast_to(place[:, None, :], (MLA_ROPE, MLA_H, MLA_PAD))
    wkk = jnp.concatenate([k_top, k_bot], axis=0).reshape(MLA_KVL + MLA_ROPE, MLA_H * MLA_PAD).astype(BF16)
    wv = wkv[:, :, MLA_NOPE:].reshape(MLA_KVL, MLA_H * MLA_V).astype(BF16)
    return wq2, wkk, wv


def kernel(x_prompt, x_sample, cache_na_k, cache_na_v, cache_mla_kv, state_s5_re, state_s5_im, c, c_ctx, w_mod, b_mod, norm1_g, norm2_g, w_in, w_out, s5_lambda_re, s5_lambda_im, s5_log_dt, s5_b_re, s5_b_im, s5_c_re, s5_c_im, s5_d, s5_glu_w, s5_glu_b, na_rpb, mla_q_norm_g, mla_w_qb, mla_kv_norm_g, mla_w_kvb, router_w, router_b, moe_w_gu, moe_b_gu, moe_w_down, moe_b_down, final_norm_g):
    x = jnp.concatenate([x_prompt.reshape(T_CTX, D_MODEL), x_sample.reshape(T_LAT, D_MODEL)], axis=0)
    cv = jnp.concatenate([c_ctx[None, :], c, jnp.zeros((MOD_ROWS - 1 - N_LAT, D_MODEL), F32)], axis=0)
    mod = _modulation(cv, w_mod, b_mod).reshape(DEPTH, MOD_ROWS, 6, D_MODEL)
    cq, sq, ck, sk = _rope_tables()
    wq2, wkk, wv = jax.vmap(_mla_weights)(mla_w_qb, mla_w_kvb)
    kc_all, vc_all = _cache_kv(cache_mla_kv, wkk, wv)
    w_in_ext = jnp.concatenate([w_in, w_in[:, :, OFF_KVA + MLA_KVL + _ROPE_PERM]], axis=-1).astype(BF16)
    w_out_bf = w_out.astype(BF16)
    glu_w_bf = s5_glu_w.astype(BF16)
    rw = jnp.pad(router_w, ((0, 0), (0, 0), (0, 128 - N_EXP)))
    rw_hi = rw.astype(BF16)
    rw = jnp.stack([rw_hi, (rw - rw_hi.astype(F32)).astype(BF16)], axis=1)
    rb = jnp.pad(router_b, ((0, 0), (0, 128 - N_EXP)))[:, None, :]
    vec = lambda v: v[:, None, :]
    gp = S5_G * S5_P
    na_bias = _na_bias_tables(na_rpb)
    fin_k = jnp.zeros((N_CTX, DEPTH, NA_H, L_CTX, NA_D), F32)
    fin_v = jnp.zeros((N_CTX, DEPTH, NA_H, L_CTX, NA_D), F32)
    fin_m = jnp.zeros((N_CTX, DEPTH, L_CTX, MLA_KVL + MLA_ROPE), F32)

    s5_re_list, s5_im_list = [], []
    for l in range(DEPTH):
        u, naq, nak, nav, mq, mk, mv, ckv = _project(
            x, mod, vec(norm1_g), w_in_ext, vec(mla_q_norm_g), wq2, vec(mla_kv_norm_g), wkk, wv,
            cq, sq, ck, sk, l)

        a, bb = _s5_discretize(s5_lambda_re[l], s5_lambda_im[l], s5_log_dt[l], s5_b_re[l], s5_b_im[l])
        a_lanes, wb, wc = _s5_matrices(a, bb, s5_c_re[l], s5_c_im[l])
        u_ctx = u[:T_CTX].reshape(N_CTX, L_CTX, S5_W).transpose(1, 0, 2)
        y_ctx, ht_ctx = _s5_scan(u_ctx, wb, wc, a_lanes, jnp.zeros((2, N_CTX, S5_STATE_W), F32))
        u_lat = u[T_CTX:].reshape(SCAN_ROWS, S5_SEG, S5_W).transpose(1, 0, 2)
        h0 = jnp.concatenate([state_s5_re[:, l].reshape(N_LAT, 2, gp), state_s5_im[:, l].reshape(N_LAT, 2, gp)], axis=-1)
        h0z = jnp.zeros((N_LAT, N_SEG - 1, S5_STATE_W), F32)
        h0 = jnp.stack([jnp.concatenate([h0[:, 0:1], h0z], axis=1).reshape(SCAN_ROWS, S5_STATE_W),
                        jnp.concatenate([h0z, h0[:, 1:2]], axis=1).reshape(SCAN_ROWS, S5_STATE_W)], axis=0)
        y_loc, ht_loc = _s5_scan(u_lat, wb, wc, a_lanes, h0)
        ptab = _s5_powers(a_lanes)
        ptab = jnp.concatenate([ptab[0:1], jnp.flip(ptab[1:2], axis=2)], axis=0)
        y_lat = _s5_fix(y_loc.transpose(0, 2, 1, 3).reshape(2, T_LAT, S5_W), ptab, ht_loc, wc)
        y2 = jnp.concatenate([y_ctx.transpose(0, 2, 1, 3).reshape(2, T_CTX, S5_W), y_lat], axis=1)
        st = ht_ctx.reshape(2, N_CTX, 2, S5_G, S5_P).transpose(1, 0, 2, 3, 4)
        s5_re_list.append(st[:, :, 0])
        s5_im_list.append(st[:, :, 1])

        na_ctx, mla_ctx, fin_k, fin_v, fin_m = _ctx_attention(naq, nak, nav, mq, mk, mv, ckv, fin_k, fin_v, fin_m, l)
        na_lat = _na_latent(naq, nak, nav, cache_na_k, cache_na_v, na_bias, l)
        mla_lat = _mla_latent(mq, mk, mv, kc_all, vc_all, l)

        x1, h2, te, tg, rk, cnt = _post(y2, u, vec(s5_d), glu_w_bf, vec(s5_glu_b), na_ctx, na_lat, mla_ctx, mla_lat,
                                        w_out_bf, x, mod, vec(norm2_g), rw, rb, l)
        pos_km, be, first, nused = _moe_plan(te, rk, cnt)
        ys = _moe_fused(be, first, nused, pos_km, h2, moe_w_gu, moe_b_gu, moe_w_down, moe_b_down, l)
        if l < DEPTH - 1:
            x = _combine(ys, tg, x1, mod, l)

    g_fin = final_norm_g[None]
    y_prompt = _combine(ys, tg, x1, mod, DEPTH - 1, 0, CTX_TILES, g_fin).reshape(N_CTX, L_CTX, D_MODEL)
    y_sample = _combine(ys, tg, x1, mod, DEPTH - 1, CTX_TILES, N_TILES - CTX_TILES, g_fin).reshape(N_LAT, L_LAT, D_MODEL)
    return (y_prompt, y_sample, fin_k, fin_v, fin_m, jnp.stack(s5_re_list, axis=1), jnp.stack(s5_im_list, axis=1))
```

```python
import functools
import math

import numpy as np
import jax
import jax.numpy as jnp
from jax import lax
from jax.experimental import pallas as pl
from jax.experimental.pallas import tpu as pltpu

F32 = jnp.float32
BF16 = jnp.bfloat16

D_MODEL = 1024
N_CTX, L_CTX = 32, 256
N_LAT, L_LAT = 2, 2048
DEPTH = 4
PAST = 256
GRID_W = 64
S5_W, S5_G, S5_P, S5_C = 256, 16, 64, 16
NA_H, NA_D = 4, 64
WIN_H, WIN_W = 8, 16
MLA_H, MLA_NOPE, MLA_ROPE, MLA_V = 8, 64, 32, 64
MLA_QL, MLA_KVL = 256, 128
MLA_SCALE = (MLA_NOPE + MLA_ROPE) ** -0.5
NA_SCALE = NA_D ** -0.5
OFF_NA = S5_W
OFF_QA = OFF_NA + 3 * NA_H * NA_D
OFF_KVA = OFF_QA + MLA_QL
D_IN = OFF_KVA + MLA_KVL + MLA_ROPE
N_EXP, TOP_K, D_FF = 32, 4, 1024
SWIGLU_ALPHA, SWIGLU_LIMIT = 1.702, 7.0
ROPE_BASE = 10000.0
EPS = 1e-6

TM = 256
T_CTX = N_CTX * L_CTX
T_LAT = N_LAT * L_LAT
T_ALL = T_CTX + T_LAT
CTX_TILES = T_CTX // TM
LAT_TILES = L_LAT // TM
N_TILES = T_ALL // TM
MOD_ROWS = 8
MLA_PAD = 128
D_IN_EXT = D_IN + MLA_ROPE
SCAN_ROWS = 8
SCAN_CH = 64
S5_STATE_W = 2 * S5_G * S5_P
N_SEG = SCAN_ROWS // N_LAT
S5_SEG = L_LAT // N_SEG
MOE_BM = 256
MOE_SLOTS = T_ALL * TOP_K
MOE_ROWS = MOE_SLOTS + N_EXP * MOE_BM
MOE_BLOCKS = MOE_ROWS // MOE_BM
ASG_ROWS = 57 * 1024
NEG = -1e30
VMEM_LIMIT = 56 * 1024 * 1024


def _cparams(sem):
    return pltpu.CompilerParams(dimension_semantics=sem, vmem_limit_bytes=VMEM_LIMIT)


def _mod_row(i):
    return jnp.maximum(i - (CTX_TILES - LAT_TILES), 0) // LAT_TILES


def _rope_block(i):
    return jnp.where(i < CTX_TILES, 0, 1 + (i - CTX_TILES) % LAT_TILES)


def _rms(x, g):
    return x * lax.rsqrt(jnp.mean(x * x, axis=-1, keepdims=True) + EPS) * g


def _mod_kernel(cv_ref, w_ref, b_ref, o_ref):
    s = jax.nn.silu(cv_ref[...]).astype(BF16)
    o_ref[0] = jnp.dot(s, w_ref[0].astype(BF16), preferred_element_type=F32) + b_ref[0]


def _modulation(cv, w_mod, b_mod):
    tn = 1536
    return pl.pallas_call(
        _mod_kernel,
        grid=(DEPTH, 6 * D_MODEL // tn),
        in_specs=[pl.BlockSpec((MOD_ROWS, D_MODEL), lambda l, j: (0, 0)),
                  pl.BlockSpec((1, D_MODEL, tn), lambda l, j: (l, 0, j)),
                  pl.BlockSpec((1, 1, tn), lambda l, j: (l, 0, j))],
        out_specs=pl.BlockSpec((1, MOD_ROWS, tn), lambda l, j: (l, 0, j)),
        out_shape=jax.ShapeDtypeStruct((DEPTH, MOD_ROWS, 6 * D_MODEL), F32),
        compiler_params=_cparams(("arbitrary", "arbitrary")),
        name="modulation",
    )(cv, w_mod, b_mod.reshape(DEPTH, 1, 6 * D_MODEL))


def _proj_kernel(x_ref, mod_ref, g1_ref, win_ref, qg_ref, wq_ref, kvg_ref, wkk_ref, wv_ref,
                 cq_ref, sq_ref, ck_ref, sk_ref,
                 u_ref, naq_ref, nak_ref, nav_ref, mq_ref, mk_ref, mv_ref, ckv_ref):
    x = x_ref[...]
    mod = mod_ref[0]
    h = _rms(x, g1_ref[...]) * (1.0 + mod[1:2, :]) + mod[0:1, :]
    p = jnp.dot(h.astype(BF16), win_ref[...], preferred_element_type=F32)
    u_ref[...] = p[:, :OFF_NA]
    for hh in range(NA_H):
        naq_ref[0, hh] = p[:, OFF_NA + hh * NA_D: OFF_NA + (hh + 1) * NA_D].astype(BF16)
        nak_ref[0, hh] = p[:, OFF_NA + NA_H * NA_D + hh * NA_D: OFF_NA + NA_H * NA_D + (hh + 1) * NA_D]
        nav_ref[0, hh] = p[:, OFF_NA + 2 * NA_H * NA_D + hh * NA_D: OFF_NA + 2 * NA_H * NA_D + (hh + 1) * NA_D]
    qn = _rms(p[:, OFF_QA:OFF_KVA], qg_ref[...]).astype(BF16)
    qq = jnp.dot(qn, wq_ref[...], preferred_element_type=F32)
    cq = cq_ref[...]
    sq = sq_ref[...]
    for hh in range(MLA_H):
        qa = qq[:, hh * MLA_PAD:(hh + 1) * MLA_PAD]
        qb = qq[:, (MLA_H + hh) * MLA_PAD:(MLA_H + hh + 1) * MLA_PAD]
        mq_ref[0, hh] = (qa * cq + qb * sq).astype(BF16)
    ckv = _rms(p[:, OFF_KVA:OFF_KVA + MLA_KVL], kvg_ref[...])
    kr = p[:, OFF_KVA + MLA_KVL:D_IN]
    kr_partner = p[:, D_IN:D_IN_EXT]
    kr_rot = kr * ck_ref[...] + kr_partner * sk_ref[...]
    ckv_ref[...] = jnp.concatenate([ckv, kr], axis=-1)
    kin = jnp.concatenate([ckv, kr_rot], axis=-1).astype(BF16)
    kk = jnp.dot(kin, wkk_ref[...], preferred_element_type=F32)
    vv = jnp.dot(kin[:, :MLA_KVL], wv_ref[...], preferred_element_type=F32)
    for hh in range(MLA_H):
        mk_ref[0, hh] = kk[:, hh * MLA_PAD:(hh + 1) * MLA_PAD].astype(BF16)
        mv_ref[0, hh] = vv[:, hh * MLA_V:(hh + 1) * MLA_V].astype(BF16)


def _project(x, mod, g1, w_in_ext, qg, wq2, kvg, wkk, wv, cq, sq, ck, sk, layer):
    lay3 = lambda i: (layer, 0, 0)
    tile2 = lambda i: (i, 0)
    head4 = lambda i: (i, 0, 0, 0)
    rope2 = lambda i: (_rope_block(i), 0)
    return pl.pallas_call(
        _proj_kernel,
        grid=(N_TILES,),
        in_specs=[pl.BlockSpec((TM, D_MODEL), tile2),
                  pl.BlockSpec((None, 1, 6, D_MODEL), lambda i: (layer, _mod_row(i), 0, 0)),
                  pl.BlockSpec((None, 1, D_MODEL), lay3),
                  pl.BlockSpec((None, D_MODEL, D_IN_EXT), lay3),
                  pl.BlockSpec((None, 1, MLA_QL), lay3),
                  pl.BlockSpec((None, MLA_QL, 2 * MLA_H * MLA_PAD), lay3),
                  pl.BlockSpec((None, 1, MLA_KVL), lay3),
                  pl.BlockSpec((None, MLA_KVL + MLA_ROPE, MLA_H * MLA_PAD), lay3),
                  pl.BlockSpec((None, MLA_KVL, MLA_H * MLA_V), lay3),
                  pl.BlockSpec((TM, MLA_PAD), rope2),
                  pl.BlockSpec((TM, MLA_PAD), rope2),
                  pl.BlockSpec((TM, MLA_ROPE), rope2),
                  pl.BlockSpec((TM, MLA_ROPE), rope2)],
        out_specs=[pl.BlockSpec((TM, S5_W), tile2),
                   pl.BlockSpec((1, NA_H, TM, NA_D), head4),
                   pl.BlockSpec((1, NA_H, TM, NA_D), head4),
                   pl.BlockSpec((1, NA_H, TM, NA_D), head4),
                   pl.BlockSpec((1, MLA_H, TM, MLA_PAD), head4),
                   pl.BlockSpec((1, MLA_H, TM, MLA_PAD), head4),
                   pl.BlockSpec((1, MLA_H, TM, MLA_V), head4),
                   pl.BlockSpec((TM, MLA_KVL + MLA_ROPE), tile2)],
        out_shape=[jax.ShapeDtypeStruct((T_ALL, S5_W), F32),
                   jax.ShapeDtypeStruct((N_TILES, NA_H, TM, NA_D), BF16),
                   jax.ShapeDtypeStruct((N_TILES, NA_H, TM, NA_D), F32),
                   jax.ShapeDtypeStruct((N_TILES, NA_H, TM, NA_D), F32),
                   jax.ShapeDtypeStruct((N_TILES, MLA_H, TM, MLA_PAD), BF16),
                   jax.ShapeDtypeStruct((N_TILES, MLA_H, TM, MLA_PAD), BF16),
                   jax.ShapeDtypeStruct((N_TILES, MLA_H, TM, MLA_V), BF16),
                   jax.ShapeDtypeStruct((T_ALL, MLA_KVL + MLA_ROPE), F32)],
        compiler_params=_cparams(("arbitrary",)),
        name="proj",
    )(x, mod, g1, w_in_ext, qg, wq2, kvg, wkk, wv, cq, sq, ck, sk)


def _cache_kv_kernel(c_ref, wkk_ref, wv_ref, k_ref, v_ref):
    cin = c_ref[0, 0].astype(BF16)
    kk = jnp.dot(cin, wkk_ref[0], preferred_element_type=F32)
    vv = jnp.dot(cin[:, :MLA_KVL], wv_ref[0], preferred_element_type=F32)
    for hh in range(MLA_H):
        k_ref[0, 0, hh] = kk[:, hh * MLA_PAD:(hh + 1) * MLA_PAD].astype(BF16)
        v_ref[0, 0, hh] = vv[:, hh * MLA_V:(hh + 1) * MLA_V].astype(BF16)


def _cache_kv(cache_mla_kv, wkk_all, wv_all):
    return pl.pallas_call(
        _cache_kv_kernel,
        grid=(N_LAT, DEPTH),
        in_specs=[pl.BlockSpec((1, 1, PAST, MLA_KVL + MLA_ROPE), lambda n, l: (n, l, 0, 0)),
                  pl.BlockSpec((1, MLA_KVL + MLA_ROPE, MLA_H * MLA_PAD), lambda n, l: (l, 0, 0)),
                  pl.BlockSpec((1, MLA_KVL, MLA_H * MLA_V), lambda n, l: (l, 0, 0))],
        out_specs=[pl.BlockSpec((1, 1, MLA_H, PAST, MLA_PAD), lambda n, l: (n, l, 0, 0, 0)),
                   pl.BlockSpec((1, 1, MLA_H, PAST, MLA_V), lambda n, l: (n, l, 0, 0, 0))],
        out_shape=[jax.ShapeDtypeStruct((N_LAT, DEPTH, MLA_H, PAST, MLA_PAD), BF16),
                   jax.ShapeDtypeStruct((N_LAT, DEPTH, MLA_H, PAST, MLA_V), BF16)],
        compiler_params=_cparams(("arbitrary", "arbitrary")),
        name="cache_kv",
    )(cache_mla_kv, wkk_all, wv_all)


def _s5_disc_kernel(lr_ref, li_ref, ldt_ref, bre_ref, bim_ref, a_ref, bb_ref):
    lr = lr_ref[...]
    li = li_ref[...]
    dt = jnp.exp(ldt_ref[...])
    mag = jnp.exp(lr * dt)
    a_re = mag * jnp.cos(li * dt)
    a_im = mag * jnp.sin(li * dt)
    den = lr * lr + li * li
    nr = a_re - 1.0
    k_re = (nr * lr + a_im * li) / den
    k_im = (a_im * lr - nr * li) / den
    a_ref[0] = a_re
    a_ref[1] = a_im
    bre = bre_ref[...]
    bim = bim_ref[...]
    bb_ref[0] = k_re * bre - k_im * bim
    bb_ref[1] = k_re * bim + k_im * bre


def _s5_discretize(lam_re, lam_im, log_dt, b_re, b_im):
    g2 = 2 * S5_G
    a, bb = pl.pallas_call(
        _s5_disc_kernel,
        out_shape=[jax.ShapeDtypeStruct((2, g2, 1, S5_P), F32),
                   jax.ShapeDtypeStruct((2, g2, S5_C, S5_P), F32)],
        name="s5_disc",
    )(lam_re.reshape(g2, 1, S5_P), lam_im.reshape(g2, 1, S5_P), log_dt.reshape(g2, 1, 1),
      b_re.reshape(g2, S5_P, S5_C).transpose(0, 2, 1), b_im.reshape(g2, S5_P, S5_C).transpose(0, 2, 1))
    return a, bb


def _s5_matrices(a, bb, c_re, c_im):
    gp = S5_G * S5_P
    a_lanes = a.reshape(2, 2, S5_G, S5_P).transpose(1, 0, 2, 3).reshape(2, 2, gp)
    bbar = bb.reshape(2, 2, S5_G, S5_C, S5_P)
    eye = jnp.eye(S5_G, dtype=F32)
    wb = jnp.einsum('xdgcp,gh->dgcxhp', bbar, eye).reshape(2, S5_W, 2 * gp)
    cc = jnp.stack([c_re, -c_im], axis=1)
    wc = jnp.einsum('dxgcp,gh->dxhpgc', cc, eye).reshape(2, 2 * gp, S5_W)
    return a_lanes, wb.astype(BF16), wc.astype(BF16)


def _scan_kernel(u_ref, wb_ref, wc_ref, a_ref, h0_ref, y_ref, ht_ref, x_s, h_s, st_s):
    d = pl.program_id(0)
    c = pl.program_id(2)
    nc = pl.num_programs(2)
    gp = S5_G * S5_P

    @pl.when(c == 0)
    def _():
        st_s[...] = h0_ref[0]

    u = u_ref[...].reshape(SCAN_CH * SCAN_ROWS, S5_W).astype(BF16)
    x_s[...] = jnp.dot(u, wb_ref[0], preferred_element_type=F32)
    a_re = a_ref[0, 0:1, :]
    a_im = a_ref[0, 1:2, :]
    unroll = 4

    def outer(io, carry):
        h_re, h_im = carry
        for ii in range(unroll):
            i = io * unroll + ii
            t = jnp.where(d == 0, i, SCAN_CH - 1 - i)
            r = pl.multiple_of(t * SCAN_ROWS, SCAN_ROWS)
            x_re = x_s[pl.ds(r, SCAN_ROWS), pl.ds(0, gp)]
            x_im = x_s[pl.ds(r, SCAN_ROWS), pl.ds(gp, gp)]
            n_re = a_re * h_re - a_im * h_im + x_re
            n_im = a_re * h_im + a_im * h_re + x_im
            h_s[pl.ds(r, SCAN_ROWS), pl.ds(0, gp)] = n_re
            h_s[pl.ds(r, SCAN_ROWS), pl.ds(gp, gp)] = n_im
            h_re, h_im = n_re, n_im
        return h_re, h_im

    h_re, h_im = lax.fori_loop(0, SCAN_CH // unroll, outer, (st_s[:, pl.ds(0, gp)], st_s[:, pl.ds(gp, gp)]))
    st_s[:, pl.ds(0, gp)] = h_re
    st_s[:, pl.ds(gp, gp)] = h_im
    y = jnp.dot(h_s[...].astype(BF16), wc_ref[0], preferred_element_type=F32)
    y_ref[0] = y.reshape(SCAN_CH, SCAN_ROWS, S5_W)

    @pl.when(c == nc - 1)
    def _():
        ht_ref[0] = st_s[...]


def _s5_scan(u_tm, wb, wc, a_lanes, h0):
    L, n, _ = u_tm.shape
    nc = L // SCAN_CH
    ng = n // SCAN_ROWS

    def ceff(d, c):
        return c + d * (nc - 1 - 2 * c)

    return pl.pallas_call(
        _scan_kernel,
        grid=(2, ng, nc),
        in_specs=[pl.BlockSpec((SCAN_CH, SCAN_ROWS, S5_W), lambda d, g, c: (ceff(d, c), g, 0)),
                  pl.BlockSpec((1, S5_W, S5_STATE_W), lambda d, g, c: (d, 0, 0)),
                  pl.BlockSpec((1, S5_STATE_W, S5_W), lambda d, g, c: (d, 0, 0)),
                  pl.BlockSpec((1, 2, S5_G * S5_P), lambda d, g, c: (d, 0, 0)),
                  pl.BlockSpec((1, SCAN_ROWS, S5_STATE_W), lambda d, g, c: (d, g, 0))],
        out_specs=[pl.BlockSpec((1, SCAN_CH, SCAN_ROWS, S5_W), lambda d, g, c: (d, ceff(d, c), g, 0)),
                   pl.BlockSpec((1, SCAN_ROWS, S5_STATE_W), lambda d, g, c: (d, g, 0))],
        out_shape=[jax.ShapeDtypeStruct((2, L, n, S5_W), F32),
                   jax.ShapeDtypeStruct((2, n, S5_STATE_W), F32)],
        scratch_shapes=[pltpu.VMEM((SCAN_CH * SCAN_ROWS, S5_STATE_W), F32),
                        pltpu.VMEM((SCAN_CH * SCAN_ROWS, S5_STATE_W), F32),
                        pltpu.VMEM((SCAN_ROWS, S5_STATE_W), F32)],
        compiler_params=_cparams(("arbitrary", "arbitrary", "arbitrary")),
        name="s5_scan",
    )(u_tm, wb, wc, a_lanes, h0)


def _s5_pow_kernel(a_ref, p_ref):
    gp = S5_G * S5_P
    a_re = a_ref[0, 0:1, :]
    a_im = a_ref[0, 1:2, :]
    row = lax.broadcasted_iota(jnp.int32, (8, gp), 0)

    def build(rev):
        blk = (lambda m: slice(S5_SEG - m, S5_SEG)) if rev else (lambda m: slice(0, m))
        nxt = (lambda m: slice(S5_SEG - 2 * m, S5_SEG - m)) if rev else (lambda m: slice(m, 2 * m))
        top = (lambda m: slice(S5_SEG - m, S5_SEG - m + 1)) if rev else (lambda m: slice(m - 1, m))
        cur_re = jnp.broadcast_to(a_re, (8, gp))
        cur_im = jnp.broadcast_to(a_im, (8, gp))
        pr, pi = a_re, a_im
        for k in range(1, 8):
            pr, pi = pr * a_re - pi * a_im, pr * a_im + pi * a_re
            here = row == (7 - k if rev else k)
            cur_re = jnp.where(here, pr, cur_re)
            cur_im = jnp.where(here, pi, cur_im)
        if rev:
            cur_re = jnp.where(row == 7, a_re, cur_re)
            cur_im = jnp.where(row == 7, a_im, cur_im)
        p_ref[0, 0, blk(8), :] = cur_re
        p_ref[0, 1, blk(8), :] = cur_im
        m = 8
        while m < S5_SEG:
            am_re = p_ref[0, 0, top(m), :]
            am_im = p_ref[0, 1, top(m), :]
            lo_re = p_ref[0, 0, blk(m), :]
            lo_im = p_ref[0, 1, blk(m), :]
            p_ref[0, 0, nxt(m), :] = lo_re * am_re - lo_im * am_im
            p_ref[0, 1, nxt(m), :] = lo_re * am_im + lo_im * am_re
            m *= 2

    @pl.when(pl.program_id(0) == 0)
    def _():
        build(False)

    @pl.when(pl.program_id(0) == 1)
    def _():
        build(True)


def _s5_powers(a_lanes):
    gp = S5_G * S5_P
    return pl.pallas_call(
        _s5_pow_kernel,
        grid=(2,),
        in_specs=[pl.BlockSpec((1, 2, gp), lambda d: (d, 0, 0))],
        out_specs=pl.BlockSpec((1, 2, S5_SEG, gp), lambda d: (d, 0, 0, 0)),
        out_shape=jax.ShapeDtypeStruct((2, 2, S5_SEG, gp), F32),
        compiler_params=_cparams(("arbitrary",)),
        name="s5_powers",
    )(a_lanes)


def _s5_fix_kernel(y_in, p_ref, ht_ref, wc_ref, y_out, hin_s):
    gp = S5_G * S5_P
    d = pl.program_id(0)
    r = pl.program_id(1)

    @pl.when(r == 0)
    def _():
        hin_s[...] = jnp.zeros_like(hin_s)

        def chain(order, pw_row):
            aw_re = p_ref[0, 0, pw_row:pw_row + 1, :]
            aw_im = p_ref[0, 1, pw_row:pw_row + 1, :]
            for n in range(N_LAT):
                h_re = jnp.zeros((1, gp), F32)
                h_im = jnp.zeros((1, gp), F32)
                for s_prev, s_cur in zip(order[:-1], order[1:]):
                    e_re = ht_ref[0, n * N_SEG + s_prev:n * N_SEG + s_prev + 1, 0:gp]
                    e_im = ht_ref[0, n * N_SEG + s_prev:n * N_SEG + s_prev + 1, gp:2 * gp]
                    h_re, h_im = aw_re * h_re - aw_im * h_im + e_re, aw_re * h_im + aw_im * h_re + e_im
                    hin_s[n * N_SEG + s_cur, :, 0:gp] = h_re
                    hin_s[n * N_SEG + s_cur, :, gp:2 * gp] = h_im

        @pl.when(d == 0)
        def _():
            chain(list(range(N_SEG)), S5_SEG - 1)

        @pl.when(d == 1)
        def _():
            chain(list(range(N_SEG - 1, -1, -1)), 0)

    hin = hin_s[r]
    h_re = hin[:, 0:gp]
    h_im = hin[:, gp:2 * gp]
    p_re = p_ref[0, 0]
    p_im = p_ref[0, 1]
    z = jnp.concatenate([p_re * h_re - p_im * h_im, p_re * h_im + p_im * h_re], axis=-1).astype(BF16)
    y_out[0] = y_in[0] + jnp.dot(z, wc_ref[0], preferred_element_type=F32)


def _s5_fix(y_tok, ptab, ht_loc, wc):
    gp = S5_G * S5_P
    return pl.pallas_call(
        _s5_fix_kernel,
        grid=(2, SCAN_ROWS),
        in_specs=[pl.BlockSpec((1, S5_SEG, S5_W), lambda d, r: (d, r, 0)),
                  pl.BlockSpec((1, 2, S5_SEG, gp), lambda d, r: (d, 0, 0, 0)),
                  pl.BlockSpec((1, SCAN_ROWS, S5_STATE_W), lambda d, r: (d, 0, 0)),
                  pl.BlockSpec((1, S5_STATE_W, S5_W), lambda d, r: (d, 0, 0))],
        out_specs=pl.BlockSpec((1, S5_SEG, S5_W), lambda d, r: (d, r, 0)),
        out_shape=jax.ShapeDtypeStruct(y_tok.shape, F32),
        scratch_shapes=[pltpu.VMEM((SCAN_ROWS, 1, S5_STATE_W), F32)],
        input_output_aliases={0: 0},
        compiler_params=_cparams(("arbitrary", "arbitrary")),
        name="s5_fix",
    )(y_tok, ptab, ht_loc, wc)


def _qk(q, k):
    return lax.dot_general(q, k, (((1,), (1,)), ((), ())), preferred_element_type=F32)


def _softmax_pv(s_list, v_list):
    m = s_list[0].max(axis=-1, keepdims=True)
    for s in s_list[1:]:
        m = jnp.maximum(m, s.max(axis=-1, keepdims=True))
    den = None
    acc = None
    for s, v in zip(s_list, v_list):
        e = jnp.exp(s - m)
        ds = e.sum(axis=-1, keepdims=True)
        pv = jnp.dot(e.astype(BF16), v, preferred_element_type=F32)
        den = ds if den is None else den + ds
        acc = pv if acc is None else acc + pv
    return acc / den


def _ctx_attn_kernel(nq_ref, nk_ref, nv_ref, mq_ref, mk_ref, mv_ref, ckv_ref, fk_in, fv_in, fm_in,
                     na_ref, mla_ref, fk_ref, fv_ref, fm_ref):
    del fk_in, fv_in, fm_in
    fk_ref[0, 0] = nk_ref[0]
    fv_ref[0, 0] = nv_ref[0]
    fm_ref[0, 0] = ckv_ref[...]
    for hh in range(NA_H):
        s = _qk(nq_ref[0, hh], nk_ref[0, hh].astype(BF16)) * NA_SCALE
        na_ref[:, hh * NA_D:(hh + 1) * NA_D] = _softmax_pv([s], [nv_ref[0, hh].astype(BF16)])
    for hh in range(MLA_H):
        s = _qk(mq_ref[0, hh], mk_ref[0, hh]) * MLA_SCALE
        mla_ref[:, hh * MLA_V:(hh + 1) * MLA_V] = _softmax_pv([s], [mv_ref[0, hh]])


def _ctx_attention(naq, nak, nav, mq, mk, mv, ckv, fin_k, fin_v, fin_m, layer):
    head4 = lambda i: (i, 0, 0, 0)
    tile2 = lambda i: (i, 0)
    anyspec = pl.BlockSpec(memory_space=pl.ANY)
    return pl.pallas_call(
        _ctx_attn_kernel,
        grid=(CTX_TILES,),
        in_specs=[pl.BlockSpec((1, NA_H, TM, NA_D), head4)] * 3
                 + [pl.BlockSpec((1, MLA_H, TM, MLA_PAD), head4)] * 2
                 + [pl.BlockSpec((1, MLA_H, TM, MLA_V), head4),
                    pl.BlockSpec((TM, MLA_KVL + MLA_ROPE), tile2), anyspec, anyspec, anyspec],
        out_specs=[pl.BlockSpec((TM, NA_H * NA_D), tile2), pl.BlockSpec((TM, MLA_H * MLA_V), tile2),
                   pl.BlockSpec((1, 1, NA_H, TM, NA_D), lambda i: (i, layer, 0, 0, 0)),
                   pl.BlockSpec((1, 1, NA_H, TM, NA_D), lambda i: (i, layer, 0, 0, 0)),
                   pl.BlockSpec((1, 1, TM, MLA_KVL + MLA_ROPE), lambda i: (i, layer, 0, 0))],
        out_shape=[jax.ShapeDtypeStruct((T_CTX, NA_H * NA_D), F32),
                   jax.ShapeDtypeStruct((T_CTX, MLA_H * MLA_V), F32),
                   jax.ShapeDtypeStruct(fin_k.shape, F32),
                   jax.ShapeDtypeStruct(fin_v.shape, F32),
                   jax.ShapeDtypeStruct(fin_m.shape, F32)],
        input_output_aliases={7: 2, 8: 3, 9: 4},
        compiler_params=_cparams(("arbitrary",)),
        name="ctx_attn",
    )(naq, nak, nav, mq, mk, mv, ckv, fin_k, fin_v, fin_m)


def _na_lat_kernel(q_ref, k0_ref, k1_ref, k2_ref, v0_ref, v1_ref, v2_ref, kc_ref, vc_ref, b_ref, o_ref):
    for hh in range(NA_H):
        q = q_ref[0, hh]
        kb = jnp.concatenate([k0_ref[0, hh], k1_ref[0, hh], k2_ref[0, hh]], axis=0).astype(BF16)
        vb = jnp.concatenate([v0_ref[0, hh], v1_ref[0, hh], v2_ref[0, hh]], axis=0).astype(BF16)
        s_win = _qk(q, kb) * NA_SCALE + b_ref[0, 0, hh]
        s_ctx = _qk(q, kc_ref[0, 0, hh].astype(BF16)) * NA_SCALE
        o_ref[:, hh * NA_D:(hh + 1) * NA_D] = _softmax_pv([s_win, s_ctx], [vb, vc_ref[0, 0, hh].astype(BF16)])


def _band_start(j):
    return jnp.clip(j - 1, 0, LAT_TILES - 3)


def _na_latent(naq, nak, nav, cache_k, cache_v, bias, layer):
    def qmap(n, j):
        return (CTX_TILES + n * LAT_TILES + j, 0, 0, 0)

    def kmap(off):
        return lambda n, j: (CTX_TILES + n * LAT_TILES + _band_start(j) + off, 0, 0, 0)

    def bmap(n, j):
        return (layer, jnp.where(j == 0, 0, jnp.where(j == LAT_TILES - 1, 2, 1)), 0, 0, 0)

    blk = (1, NA_H, TM, NA_D)
    return pl.pallas_call(
        _na_lat_kernel,
        grid=(N_LAT, LAT_TILES),
        in_specs=[pl.BlockSpec(blk, qmap)]
                 + [pl.BlockSpec(blk, kmap(o)) for o in range(3)]
                 + [pl.BlockSpec(blk, kmap(o)) for o in range(3)]
                 + [pl.BlockSpec((1, 1, NA_H, PAST, NA_D), lambda n, j: (n, layer, 0, 0, 0))] * 2
                 + [pl.BlockSpec((1, 1, NA_H, TM, 3 * TM), bmap)],
        out_specs=pl.BlockSpec((TM, NA_H * NA_D), lambda n, j: (n * LAT_TILES + j, 0)),
        out_shape=jax.ShapeDtypeStruct((T_LAT, NA_H * NA_D), F32),
        compiler_params=_cparams(("arbitrary", "arbitrary")),
        name="na_latent",
    )(naq, nak, nak, nak, nav, nav, nav, cache_k, cache_v, bias)


def _na_bias_tables(rpb_all):
    rows = L_LAT // GRID_W
    nr, ncol = 2 * WIN_H - 1, 2 * WIN_W - 1
    w = np.arange(GRID_W)
    cs = np.clip(w - WIN_W // 2, 0, GRID_W - WIN_W)
    cc = np.arange(GRID_W)
    col_ok = (cc[None, :] >= cs[:, None]) & (cc[None, :] < cs[:, None] + WIN_W)
    dc = cc[None, :] - w[:, None] + WIN_W - 1
    col_sel = (dc[:, :, None] == np.arange(ncol)) & col_ok[:, :, None]
    row_sel, oks = [], []
    for j, s in ((0, 0), (1, 0), (LAT_TILES - 1, LAT_TILES - 3)):
        r = 4 * j + np.arange(4)
        rs = np.clip(r - WIN_H // 2, 0, rows - WIN_H)
        krow = 4 * s + np.arange(12)
        row_ok = (krow[None, :] >= rs[:, None]) & (krow[None, :] < rs[:, None] + WIN_H)
        dr = krow[None, :] - r[:, None] + WIN_H - 1
        row_sel.append((dr[:, :, None] == np.arange(nr)) & row_ok[:, :, None])
        oks.append(row_ok[:, None, :, None] & col_ok[None, :, None, :])
    row_sel = jnp.asarray(np.stack(row_sel), F32)
    col_sel = jnp.asarray(col_sel, F32)
    mask = jnp.asarray(np.where(np.stack(oks), 0.0, NEG).reshape(3, TM, 3 * TM), F32)
    b = jnp.einsum('prka,lhab,wcb->lphrwkc', row_sel, rpb_all, col_sel, precision=lax.Precision.HIGHEST)
    return b.reshape(DEPTH, 3, NA_H, TM, 3 * TM) + mask[None, :, None]


def _mla_lat_kernel(q_ref, k_ref, v_ref, kc_ref, vc_ref, o_ref):
    for hh in range(MLA_H):
        q = q_ref[0, hh]
        k = k_ref[:, hh].reshape(L_LAT, MLA_PAD)
        v = v_ref[:, hh].reshape(L_LAT, MLA_V)
        s_lat = _qk(q, k) * MLA_SCALE
        s_ctx = _qk(q, kc_ref[0, 0, hh]) * MLA_SCALE
        o_ref[:, hh * MLA_V:(hh + 1) * MLA_V] = _softmax_pv([s_lat, s_ctx], [v, vc_ref[0, 0, hh]])


def _mla_latent(mq, mk, mv, kc, vc, layer):
    seq_blk = CTX_TILES // LAT_TILES
    return pl.pallas_call(
        _mla_lat_kernel,
        grid=(N_LAT, LAT_TILES),
        in_specs=[pl.BlockSpec((1, MLA_H, TM, MLA_PAD), lambda n, j: (CTX_TILES + n * LAT_TILES + j, 0, 0, 0)),
                  pl.BlockSpec((LAT_TILES, MLA_H, TM, MLA_PAD), lambda n, j: (seq_blk + n, 0, 0, 0)),
                  pl.BlockSpec((LAT_TILES, MLA_H, TM, MLA_V), lambda n, j: (seq_blk + n, 0, 0, 0)),
                  pl.BlockSpec((1, 1, MLA_H, PAST, MLA_PAD), lambda n, j: (n, layer, 0, 0, 0)),
                  pl.BlockSpec((1, 1, MLA_H, PAST, MLA_V), lambda n, j: (n, layer, 0, 0, 0))],
        out_specs=pl.BlockSpec((TM, MLA_H * MLA_V), lambda n, j: (n * LAT_TILES + j, 0)),
        out_shape=jax.ShapeDtypeStruct((T_LAT, MLA_H * MLA_V), F32),
        compiler_params=_cparams(("arbitrary", "arbitrary")),
        name="mla_latent",
    )(mq, mk, mv, kc, vc)


def _post_kernel(y_ref, u_ref, d_ref, gw_ref, gb_ref, nac_ref, nal_ref, mlac_ref, mlal_ref, wo_ref, x_ref, mod_ref,
                 g2_ref, rw_ref, rb_ref, x1_ref, h2_ref, te_ref, tg_ref, rk_ref, cnt_ref):
    ys = y_ref[0] + y_ref[1] + u_ref[...] * d_ref[...]
    g = jax.nn.gelu(ys)
    s5o = g * jax.nn.sigmoid(jnp.dot(g.astype(BF16), gw_ref[...], preferred_element_type=F32) + gb_ref[...])
    is_ctx = pl.program_id(0) < CTX_TILES
    na = jnp.where(is_ctx, nac_ref[...], nal_ref[...])
    mla = jnp.where(is_ctx, mlac_ref[...], mlal_ref[...])
    out = (jnp.dot(s5o.astype(BF16), wo_ref[0:S5_W, :], preferred_element_type=F32)
           + jnp.dot(na.astype(BF16), wo_ref[S5_W:S5_W + NA_H * NA_D, :], preferred_element_type=F32)
           + jnp.dot(mla.astype(BF16), wo_ref[S5_W + NA_H * NA_D:, :], preferred_element_type=F32))
    mod = mod_ref[0]
    x1 = x_ref[...] + mod[2:3, :] * out
    x1_ref[...] = x1
    h2 = _rms(x1, g2_ref[...]) * (1.0 + mod[4:5, :]) + mod[3:4, :]
    h2_ref[...] = h2
    h_hi = h2.astype(BF16)
    h_lo = (h2 - h_hi.astype(F32)).astype(BF16)
    logits = (jnp.dot(h_hi, rw_ref[0], preferred_element_type=F32)
              + jnp.dot(h_hi, rw_ref[1], preferred_element_type=F32)
              + jnp.dot(h_lo, rw_ref[0], preferred_element_type=F32)) + rb_ref[...]
    lane_i = lax.broadcasted_iota(jnp.int32, (TM, 128), 1)
    lane = lane_i.astype(F32)
    cur = jnp.where(lane_i < N_EXP, logits, -jnp.inf)
    te = jnp.zeros((TM, 128), F32)
    tv = jnp.zeros((TM, 128), F32)
    hot = jnp.zeros((TM, 128), F32)
    idxs = []
    top = None
    for k in range(TOP_K):
        m = cur.max(axis=-1, keepdims=True)
        idx = jnp.where(cur == m, lane, 128.0).min(axis=-1, keepdims=True)
        sel = lane == idx
        top = m if top is None else top
        idxs.append(idx)
        te = jnp.where(lane_i == k, idx, te)
        tv = jnp.where(lane_i == k, jnp.exp(m - top), tv)
        hot = jnp.where(sel, 1.0, hot)
        cur = jnp.where(sel, -jnp.inf, cur)
    te_ref[...] = te.astype(jnp.int32)
    tg_ref[...] = tv / tv.sum(axis=-1, keepdims=True)
    row = lax.broadcasted_iota(jnp.int32, (TM, TM), 0)
    col = lax.broadcasted_iota(jnp.int32, (TM, TM), 1)
    tri = jnp.where(col < row, 1.0, 0.0).astype(BF16)
    before = jnp.dot(tri, hot.astype(BF16), preferred_element_type=F32)
    rk = jnp.zeros((TM, 128), F32)
    for k in range(TOP_K):
        rk = jnp.where(lane_i == k, jnp.where(lane == idxs[k], before, 0.0).sum(axis=-1, keepdims=True), rk)
    rk_ref[...] = rk.astype(jnp.int32)
    cnt_ref[0] = jnp.broadcast_to(hot.sum(axis=0, keepdims=True), (8, 128)).astype(jnp.int32)


def _post(y2, u, s5_d, glu_w, glu_b, na_ctx, na_lat, mla_ctx, mla_lat, w_out, x, mod, g2, rw, rb, layer):
    lay3 = lambda i: (layer, 0, 0)
    tile2 = lambda i: (i, 0)
    ctx2 = lambda i: (jnp.minimum(i, CTX_TILES - 1), 0)
    lat2 = lambda i: (jnp.maximum(i - CTX_TILES, 0), 0)
    return pl.pallas_call(
        _post_kernel,
        grid=(N_TILES,),
        in_specs=[pl.BlockSpec((2, TM, S5_W), lambda i: (0, i, 0)),
                  pl.BlockSpec((TM, S5_W), tile2),
                  pl.BlockSpec((None, 1, S5_W), lay3),
                  pl.BlockSpec((None, S5_W, S5_W), lay3),
                  pl.BlockSpec((None, 1, S5_W), lay3),
                  pl.BlockSpec((TM, NA_H * NA_D), ctx2),
                  pl.BlockSpec((TM, NA_H * NA_D), lat2),
                  pl.BlockSpec((TM, MLA_H * MLA_V), ctx2),
                  pl.BlockSpec((TM, MLA_H * MLA_V), lat2),
                  pl.BlockSpec((None, D_MODEL, D_MODEL), lay3),
                  pl.BlockSpec((TM, D_MODEL), tile2),
                  pl.BlockSpec((None, 1, 6, D_MODEL), lambda i: (layer, _mod_row(i), 0, 0)),
                  pl.BlockSpec((None, 1, D_MODEL), lay3),
                  pl.BlockSpec((None, 2, D_MODEL, 128), lambda i: (layer, 0, 0, 0)),
                  pl.BlockSpec((None, 1, 128), lay3)],
        out_specs=[pl.BlockSpec((TM, D_MODEL), tile2),
                   pl.BlockSpec((TM, D_MODEL), tile2),
                   pl.BlockSpec((TM, 128), tile2),
                   pl.BlockSpec((TM, 128), tile2),
                   pl.BlockSpec((TM, 128), tile2),
                   pl.BlockSpec((1, 8, 128), lambda i: (i, 0, 0))],
        out_shape=[jax.ShapeDtypeStruct((T_ALL, D_MODEL), F32),
                   jax.ShapeDtypeStruct((T_ALL, D_MODEL), F32),
                   jax.ShapeDtypeStruct((T_ALL, 128), jnp.int32),
                   jax.ShapeDtypeStruct((T_ALL, 128), F32),
                   jax.ShapeDtypeStruct((T_ALL, 128), jnp.int32),
                   jax.ShapeDtypeStruct((N_TILES, 8, 128), jnp.int32)],
        compiler_params=_cparams(("arbitrary",)),
        name="post",
    )(y2, u, s5_d, glu_w, glu_b, na_ctx, na_lat, mla_ctx, mla_lat, w_out, x, mod, g2, rw, rb)


def _moe_fused_kernel(be_ref, first_ref, nused_ref, pos_ref,
                      dflt_hbm, h_hbm, wgu_ref, bgu_ref, wd_ref, bd_ref, ys_hbm,
                      asg_s, xs0, xs1, yo0, yo1, wgu_s, wd_s, act_s, gsem, ssem, tsem):
    b = pl.program_id(0)
    nused = nused_ref[0]
    xs = (xs0, xs1)
    yo = (yo0, yo1)

    def wait_gather(p):
        pltpu.make_async_copy(h_hbm.at[pl.ds(0, MOE_BM), :], xs[p], gsem.at[p]).wait()

    def wait_scatter(p):
        pltpu.make_async_copy(yo[p], ys_hbm.at[pl.ds(0, MOE_BM), :], ssem.at[p]).wait()

    def issue_gather(blk, p):
        r0 = (blk + 1) * MOE_BM
        for j in range(MOE_BM):
            tok = asg_s[r0 + j] >> 16
            pltpu.make_async_copy(h_hbm.at[pl.ds(tok, 1), :], xs[p].at[pl.ds(j, 1), :],
                                  gsem.at[p]).start(priority=j % 2)

    def issue_scatter(blk, p, lo=0, hi=MOE_BM):
        r0 = (blk + 1) * MOE_BM
        for j in range(lo, hi):
            d = asg_s[r0 + j] & 0xFFFF
            pltpu.make_async_copy(yo[p].at[pl.ds(j, 1), :], ys_hbm.at[pl.ds(d, 1), :],
                                  ssem.at[p]).start(priority=j % 2)

    @pl.when(b == 0)
    def _():
        cp = pltpu.make_async_copy(dflt_hbm, asg_s, tsem)
        cp.start()
        cp.wait()

        def inv(i, carry):
            a0 = i * 8
            v0 = a0 | ((((a0 >> 10) << 8) + (a0 & (TM - 1))) << 16)
            for uu in range(8):
                asg_s[pos_ref[a0 + uu]] = v0 + uu * ((1 << 16) + 1)
            return carry
        lax.fori_loop(0, MOE_SLOTS // 8, inv, 0)
        yo1[...] = jnp.zeros_like(yo1)
        zc = pltpu.make_async_copy(yo1, ys_hbm.at[pl.ds(MOE_SLOTS, MOE_BM), :], tsem)
        zc.start()
        zc.wait()
        issue_gather(0, 0)

    def step(p):
        q = 1 - p
        wait_gather(p)
        if p == 0:
            @pl.when(b >= 1)
            def _():
                wait_scatter(p)
        else:
            wait_scatter(p)

        @pl.when(first_ref[b] == 1)
        def _():
            wgu_s[...] = wgu_ref[0, 0].astype(BF16)
            wd_s[...] = wd_ref[0, 0].astype(BF16)

        issue_gather(jnp.minimum(b + 1, nused - 1), q)
        issue_scatter(b - 1, q, 0, MOE_BM // 2)
        gu = jnp.dot(xs[p][...].astype(BF16), wgu_s[...], preferred_element_type=F32) + bgu_ref[0, 0]
        gate = jnp.minimum(gu[:, :D_FF], SWIGLU_LIMIT)
        up = jnp.clip(gu[:, D_FF:], -SWIGLU_LIMIT, SWIGLU_LIMIT)
        act_s[...] = ((up + 1.0) * (gate * jax.nn.sigmoid(SWIGLU_ALPHA * gate))).astype(BF16)

        @pl.when(b >= 0)
        def _():
            issue_scatter(b - 1, q, MOE_BM // 2, MOE_BM)
            yo[p][...] = jnp.dot(act_s[...], wd_s[...], preferred_element_type=F32) + bd_ref[0, 0]

    def drain(p):
        q = 1 - p
        wait_gather(p)
        wait_scatter(p)
        issue_scatter(nused - 1, q)
        wait_scatter(q)

    for p in range(2):
        @pl.when((b < nused) & (b % 2 == p))
        def _():
            step(p)

        @pl.when((b == nused) & (b % 2 == p))
        def _():
            drain(p)


def _moe_fused(block_e, first, nused, pos_km, h2, w_gu, b_gu, w_down, b_down, layer):
    def wmap(b, be, fi, nu, pos):
        return (layer, be[b], 0, 0)

    r = jnp.arange(ASG_ROWS, dtype=jnp.int32)
    dflt = MOE_SLOTS + (((r >> 8) + 1) & 1) * MOE_BM + (r & (MOE_BM - 1))

    return pl.pallas_call(
        _moe_fused_kernel,
        grid_spec=pltpu.PrefetchScalarGridSpec(
            num_scalar_prefetch=4, grid=(MOE_BLOCKS + 1,),
            in_specs=[pl.BlockSpec(memory_space=pl.ANY), pl.BlockSpec(memory_space=pl.ANY),
                      pl.BlockSpec((1, 1, D_MODEL, 2 * D_FF), wmap),
                      pl.BlockSpec((1, 1, 1, 2 * D_FF), wmap),
                      pl.BlockSpec((1, 1, D_FF, D_MODEL), wmap),
                      pl.BlockSpec((1, 1, 1, D_MODEL), wmap)],
            out_specs=pl.BlockSpec(memory_space=pl.ANY),
            scratch_shapes=[pltpu.SMEM((ASG_ROWS,), jnp.int32),
                            pltpu.VMEM((MOE_BM, D_MODEL), F32), pltpu.VMEM((MOE_BM, D_MODEL), F32),
                            pltpu.VMEM((MOE_BM, D_MODEL), F32), pltpu.VMEM((MOE_BM, D_MODEL), F32),
                            pltpu.VMEM((D_MODEL, 2 * D_FF), BF16), pltpu.VMEM((D_FF, D_MODEL), BF16),
                            pltpu.VMEM((MOE_BM, D_FF), BF16),
                            pltpu.SemaphoreType.DMA((2,)), pltpu.SemaphoreType.DMA((2,)),
                            pltpu.SemaphoreType.DMA(())]),
        out_shape=jax.ShapeDtypeStruct((MOE_SLOTS + 2 * MOE_BM, D_MODEL), F32),
        compiler_params=_cparams(("arbitrary",)),
        name="moe_fused",
    )(block_e, first, nused, pos_km, dflt, h2, w_gu, b_gu.reshape(DEPTH, N_EXP, 1, 2 * D_FF), w_down,
      b_down.reshape(DEPTH, N_EXP, 1, D_MODEL))


def _combine_kernel(ys_ref, tg_ref, x1_ref, mod_ref, *rest):
    tg = tg_ref[...]
    y = ys_ref[0:TM, :] * tg[:, 0:1]
    for k in range(1, TOP_K):
        y = y + ys_ref[k * TM:(k + 1) * TM, :] * tg[:, k:k + 1]
    x2 = x1_ref[...] + mod_ref[0][5:6, :] * y
    if len(rest) == 2:
        g_ref, o_ref = rest
        o_ref[...] = _rms(x2, g_ref[...])
    else:
        rest[0][...] = x2


def _combine(ys, tg, x1, mod, layer, tile0=0, ntiles=N_TILES, final_g=None):
    off = lambda i: (i + tile0, 0)
    in_specs = [pl.BlockSpec((TOP_K * TM, D_MODEL), off),
                pl.BlockSpec((TM, 128), off),
                pl.BlockSpec((TM, D_MODEL), off),
                pl.BlockSpec((None, 1, 6, D_MODEL), lambda i: (layer, _mod_row(i + tile0), 0, 0))]
    args = [ys, tg, x1, mod]
    if final_g is not None:
        in_specs.append(pl.BlockSpec((1, D_MODEL), lambda i: (0, 0)))
        args.append(final_g)
    return pl.pallas_call(
        _combine_kernel,
        grid=(ntiles,),
        in_specs=in_specs,
        out_specs=pl.BlockSpec((TM, D_MODEL), lambda i: (i, 0)),
        out_shape=jax.ShapeDtypeStruct((ntiles * TM, D_MODEL), F32),
        compiler_params=_cparams(("arbitrary",)),
        name="moe_combine",
    )(*args)


def _moe_plan(te, rk, cnt):
    tile_cnt = cnt[:, 0, :N_EXP]
    counts = tile_cnt.sum(axis=0)
    padded = (counts + MOE_BM - 1) // MOE_BM * MOE_BM
    pad_end = jnp.cumsum(padded)
    pad_start = pad_end - padded
    tile_off = jnp.cumsum(tile_cnt, axis=0) - tile_cnt
    base = (pad_start[None, :] + tile_off).astype(jnp.int32)
    e = te[:, :TOP_K].reshape(N_TILES, TM, TOP_K)
    hot = e[..., None] == jnp.arange(N_EXP, dtype=jnp.int32)
    pos = jnp.sum(jnp.where(hot, base[:, None, None, :], 0), axis=-1) + rk[:, :TOP_K].reshape(N_TILES, TM, TOP_K)
    pos_km = (pos.transpose(0, 2, 1).reshape(-1) + MOE_BM).astype(jnp.int32)
    nused = (pad_end[-1] // MOE_BM).astype(jnp.int32)
    blk = jnp.arange(MOE_BLOCKS + 1, dtype=jnp.int32)
    be = jnp.minimum(jnp.sum((pad_end[None, :] <= (blk * MOE_BM)[:, None]).astype(jnp.int32), axis=1), N_EXP - 1)
    be = jnp.where(blk < nused, be, be[nused - 1])
    first = jnp.concatenate([jnp.ones((1,), jnp.int32), (be[1:] != be[:-1]).astype(jnp.int32)])
    return pos_km, be, first, nused.reshape(1)


_ROPE_PERM = np.concatenate([np.arange(8, 16), np.arange(0, 8), np.arange(24, 32), np.arange(16, 24)])


def _rope_tables():
    half = MLA_ROPE // 2
    t = jnp.arange(L_LAT)
    row = (t // GRID_W).astype(F32)
    col = (t % GRID_W).astype(F32)
    inv = ROPE_BASE ** (-jnp.arange(0, half, 2, dtype=F32) / half)

    def part(pos):
        ang = pos[:, None] * inv[None, :]
        c, s = jnp.cos(ang), jnp.sin(ang)
        return jnp.concatenate([c, c], axis=-1), jnp.concatenate([-s, s], axis=-1)

    cr, sr = part(row)
    cc, sc = part(col)
    cos32 = jnp.concatenate([cr, cc], axis=-1)
    sin32 = jnp.concatenate([sr, sc], axis=-1)
    ck = jnp.concatenate([jnp.ones((TM, MLA_ROPE), F32), cos32], axis=0)
    sk = jnp.concatenate([jnp.zeros((TM, MLA_ROPE), F32), sin32], axis=0)
    padw = MLA_PAD - MLA_NOPE - MLA_ROPE
    cq = jnp.concatenate([jnp.ones((TM + L_LAT, MLA_NOPE), F32), ck, jnp.ones((TM + L_LAT, padw), F32)], axis=-1)
    sq = jnp.concatenate([jnp.zeros((TM + L_LAT, MLA_NOPE), F32), sk, jnp.zeros((TM + L_LAT, padw), F32)], axis=-1)
    return cq, sq, ck, sk


def _mla_weights(w_qb, w_kvb):
    dq = MLA_NOPE + MLA_ROPE
    wq = w_qb.reshape(MLA_QL, MLA_H, dq)
    zpad = jnp.zeros((MLA_QL, MLA_H, MLA_PAD - dq), F32)
    q_main = jnp.concatenate([wq, zpad], axis=-1)
    q_part = jnp.concatenate([jnp.zeros((MLA_QL, MLA_H, MLA_NOPE), F32), wq[:, :, MLA_NOPE + _ROPE_PERM], zpad], axis=-1)
    wq2 = jnp.concatenate([q_main.reshape(MLA_QL, -1), q_part.reshape(MLA_QL, -1)], axis=-1).astype(BF16)
    wkv = w_kvb.reshape(MLA_KVL, MLA_H, MLA_NOPE + MLA_V)
    k_top = jnp.concatenate([wkv[:, :, :MLA_NOPE], jnp.zeros((MLA_KVL, MLA_H, MLA_PAD - MLA_NOPE), F32)], axis=-1)
    place = jnp.concatenate([jnp.zeros((MLA_ROPE, MLA_NOPE), F32), jnp.eye(MLA_ROPE, dtype=F32),
                             jnp.zeros((MLA_ROPE, MLA_PAD - dq), F32)], axis=-1)
    k_bot = jnp.broadcast_to(place[:, None, :], (MLA_ROPE, MLA_H, MLA_PAD))
    wkk = jnp.concatenate([k_top, k_bot], axis=0).reshape(MLA_KVL + MLA_ROPE, MLA_H * MLA_PAD).astype(BF16)
    wv = wkv[:, :, MLA_NOPE:].reshape(MLA_KVL, MLA_H * MLA_V).astype(BF16)
    return wq2, wkk, wv


def kernel(x_prompt, x_sample, cache_na_k, cache_na_v, cache_mla_kv, state_s5_re, state_s5_im, c, c_ctx, w_mod, b_mod, norm1_g, norm2_g, w_in, w_out, s5_lambda_re, s5_lambda_im, s5_log_dt, s5_b_re, s5_b_im, s5_c_re, s5_c_im, s5_d, s5_glu_w, s5_glu_b, na_rpb, mla_q_norm_g, mla_w_qb, mla_kv_norm_g, mla_w_kvb, router_w, router_b, moe_w_gu, moe_b_gu, moe_w_down, moe_b_down, final_norm_g):
    x = jnp.concatenate([x_prompt.reshape(T_CTX, D_MODEL), x_sample.reshape(T_LAT, D_MODEL)], axis=0)
    cv = jnp.concatenate([c_ctx[None, :], c, jnp.zeros((MOD_ROWS - 1 - N_LAT, D_MODEL), F32)], axis=0)
    mod = _modulation(cv, w_mod, b_mod).reshape(DEPTH, MOD_ROWS, 6, D_MODEL)
    cq, sq, ck, sk = _rope_tables()
    wq2, wkk, wv = jax.vmap(_mla_weights)(mla_w_qb, mla_w_kvb)
    kc_all, vc_all = _cache_kv(cache_mla_kv, wkk, wv)
    w_in_ext = jnp.concatenate([w_in, w_in[:, :, OFF_KVA + MLA_KVL + _ROPE_PERM]], axis=-1).astype(BF16)
    w_out_bf = w_out.astype(BF16)
    glu_w_bf = s5_glu_w.astype(BF16)
    rw = jnp.pad(router_w, ((0, 0), (0, 0), (0, 128 - N_EXP)))
    rw_hi = rw.astype(BF16)
    rw = jnp.stack([rw_hi, (rw - rw_hi.astype(F32)).astype(BF16)], axis=1)
    rb = jnp.pad(router_b, ((0, 0), (0, 128 - N_EXP)))[:, None, :]
    vec = lambda v: v[:, None, :]
    gp = S5_G * S5_P
    na_bias = _na_bias_tables(na_rpb)
    fin_k = jnp.zeros((N_CTX, DEPTH, NA_H, L_CTX, NA_D), F32)
    fin_v = jnp.zeros((N_CTX, DEPTH, NA_H, L_CTX, NA_D), F32)
    fin_m = jnp.zeros((N_CTX, DEPTH, L_CTX, MLA_KVL + MLA_ROPE), F32)

    s5_re_list, s5_im_list = [], []
    for l in range(DEPTH):
        u, naq, nak, nav, mq, mk, mv, ckv = _project(
            x, mod, vec(norm1_g), w_in_ext, vec(mla_q_norm_g), wq2, vec(mla_kv_norm_g), wkk, wv,
            cq, sq, ck, sk, l)

        a, bb = _s5_discretize(s5_lambda_re[l], s5_lambda_im[l], s5_log_dt[l], s5_b_re[l], s5_b_im[l])
        a_lanes, wb, wc = _s5_matrices(a, bb, s5_c_re[l], s5_c_im[l])
        u_ctx = u[:T_CTX].reshape(N_CTX, L_CTX, S5_W).transpose(1, 0, 2)
        y_ctx, ht_ctx = _s5_scan(u_ctx, wb, wc, a_lanes, jnp.zeros((2, N_CTX, S5_STATE_W), F32))
        u_lat = u[T_CTX:].reshape(SCAN_ROWS, S5_SEG, S5_W).transpose(1, 0, 2)
        h0 = jnp.concatenate([state_s5_re[:, l].reshape(N_LAT, 2, gp), state_s5_im[:, l].reshape(N_LAT, 2, gp)], axis=-1)
        h0z = jnp.zeros((N_LAT, N_SEG - 1, S5_STATE_W), F32)
        h0 = jnp.stack([jnp.concatenate([h0[:, 0:1], h0z], axis=1).reshape(SCAN_ROWS, S5_STATE_W),
                        jnp.concatenate([h0z, h0[:, 1:2]], axis=1).reshape(SCAN_ROWS, S5_STATE_W)], axis=0)
        y_loc, ht_loc = _s5_scan(u_lat, wb, wc, a_lanes, h0)
        ptab = _s5_powers(a_lanes)
        y_lat = _s5_fix(y_loc.transpose(0, 2, 1, 3).reshape(2, T_LAT, S5_W), ptab, ht_loc, wc)
        y2 = jnp.concatenate([y_ctx.transpose(0, 2, 1, 3).reshape(2, T_CTX, S5_W), y_lat], axis=1)
        st = ht_ctx.reshape(2, N_CTX, 2, S5_G, S5_P).transpose(1, 0, 2, 3, 4)
        s5_re_list.append(st[:, :, 0])
        s5_im_list.append(st[:, :, 1])

        na_ctx, mla_ctx, fin_k, fin_v, fin_m = _ctx_attention(naq, nak, nav, mq, mk, mv, ckv, fin_k, fin_v, fin_m, l)
        na_lat = _na_latent(naq, nak, nav, cache_na_k, cache_na_v, na_bias, l)
        mla_lat = _mla_latent(mq, mk, mv, kc_all, vc_all, l)

        x1, h2, te, tg, rk, cnt = _post(y2, u, vec(s5_d), glu_w_bf, vec(s5_glu_b), na_ctx, na_lat, mla_ctx, mla_lat,
                                        w_out_bf, x, mod, vec(norm2_g), rw, rb, l)
        pos_km, be, first, nused = _moe_plan(te, rk, cnt)
        ys = _moe_fused(be, first, nused, pos_km, h2, moe_w_gu, moe_b_gu, moe_w_down, moe_b_down, l)
        if l < DEPTH - 1:
            x = _combine(ys, tg, x1, mod, l)

    g_fin = final_norm_g[None]
    y_prompt = _combine(ys, tg, x1, mod, DEPTH - 1, 0, CTX_TILES, g_fin).reshape(N_CTX, L_CTX, D_MODEL)
    y_sample = _combine(ys, tg, x1, mod, DEPTH - 1, CTX_TILES, N_TILES - CTX_TILES, g_fin).reshape(N_LAT, L_LAT, D_MODEL)
    return (y_prompt, y_sample, fin_k, fin_v, fin_m, jnp.stack(s5_re_list, axis=1), jnp.stack(s5_im_list, axis=1))
```

```python
import functools
import math

import numpy as np
import jax
import jax.numpy as jnp
from jax import lax
from jax.experimental import pallas as pl
from jax.experimental.pallas import tpu as pltpu

F32 = jnp.float32
BF16 = jnp.bfloat16

D_MODEL = 1024
N_CTX, L_CTX = 32, 256
N_LAT, L_LAT = 2, 2048
DEPTH = 4
PAST = 256
GRID_W = 64
S5_W, S5_G, S5_P, S5_C = 256, 16, 64, 16
NA_H, NA_D = 4, 64
WIN_H, WIN_W = 8, 16
MLA_H, MLA_NOPE, MLA_ROPE, MLA_V = 8, 64, 32, 64
MLA_QL, MLA_KVL = 256, 128
MLA_SCALE = (MLA_NOPE + MLA_ROPE) ** -0.5
NA_SCALE = NA_D ** -0.5
OFF_NA = S5_W
OFF_QA = OFF_NA + 3 * NA_H * NA_D
OFF_KVA = OFF_QA + MLA_QL
D_IN = OFF_KVA + MLA_KVL + MLA_ROPE
N_EXP, TOP_K, D_FF = 32, 4, 1024
SWIGLU_ALPHA, SWIGLU_LIMIT = 1.702, 7.0
ROPE_BASE = 10000.0
EPS = 1e-6

TM = 256
T_CTX = N_CTX * L_CTX
T_LAT = N_LAT * L_LAT
T_ALL = T_CTX + T_LAT
CTX_TILES = T_CTX // TM
LAT_TILES = L_LAT // TM
N_TILES = T_ALL // TM
MOD_ROWS = 8
MLA_PAD = 128
D_IN_EXT = D_IN + MLA_ROPE
SCAN_ROWS = 8
SCAN_CH = 128
S5_STATE_W = 2 * S5_G * S5_P
N_SEG = SCAN_ROWS // N_LAT
S5_SEG = L_LAT // N_SEG
MOE_BM = 256
MOE_SLOTS = T_ALL * TOP_K
MOE_ROWS = MOE_SLOTS + N_EXP * MOE_BM
MOE_BLOCKS = MOE_ROWS // MOE_BM
ASG_ROWS = 57 * 1024
NEG = -1e30
VMEM_LIMIT = 56 * 1024 * 1024


def _cparams(sem):
    return pltpu.CompilerParams(dimension_semantics=sem, vmem_limit_bytes=VMEM_LIMIT)


def _mod_row(i):
    return jnp.maximum(i - (CTX_TILES - LAT_TILES), 0) // LAT_TILES


def _rope_block(i):
    return jnp.where(i < CTX_TILES, 0, 1 + (i - CTX_TILES) % LAT_TILES)


def _rms(x, g):
    return x * lax.rsqrt(jnp.mean(x * x, axis=-1, keepdims=True) + EPS) * g


def _mod_kernel(cv_ref, w_ref, b_ref, o_ref):
    s = jax.nn.silu(cv_ref[...]).astype(BF16)
    o_ref[0] = jnp.dot(s, w_ref[0].astype(BF16), preferred_element_type=F32) + b_ref[0]


def _modulation(cv, w_mod, b_mod):
    tn = 1536
    return pl.pallas_call(
        _mod_kernel,
        grid=(DEPTH, 6 * D_MODEL // tn),
        in_specs=[pl.BlockSpec((MOD_ROWS, D_MODEL), lambda l, j: (0, 0)),
                  pl.BlockSpec((1, D_MODEL, tn), lambda l, j: (l, 0, j)),
                  pl.BlockSpec((1, 1, tn), lambda l, j: (l, 0, j))],
        out_specs=pl.BlockSpec((1, MOD_ROWS, tn), lambda l, j: (l, 0, j)),
        out_shape=jax.ShapeDtypeStruct((DEPTH, MOD_ROWS, 6 * D_MODEL), F32),
        compiler_params=_cparams(("arbitrary", "arbitrary")),
        name="modulation",
    )(cv, w_mod, b_mod.reshape(DEPTH, 1, 6 * D_MODEL))


def _proj_kernel(x_ref, mod_ref, g1_ref, win_ref, qg_ref, wq_ref, kvg_ref, wkk_ref, wv_ref,
                 cq_ref, sq_ref, ck_ref, sk_ref,
                 u_ref, naq_ref, nak_ref, nav_ref, mq_ref, mk_ref, mv_ref, ckv_ref):
    x = x_ref[...]
    mod = mod_ref[0]
    h = _rms(x, g1_ref[...]) * (1.0 + mod[1:2, :]) + mod[0:1, :]
    p = jnp.dot(h.astype(BF16), win_ref[...], preferred_element_type=F32)
    u_ref[...] = p[:, :OFF_NA]
    for hh in range(NA_H):
        naq_ref[0, hh] = p[:, OFF_NA + hh * NA_D: OFF_NA + (hh + 1) * NA_D].astype(BF16)
        nak_ref[0, hh] = p[:, OFF_NA + NA_H * NA_D + hh * NA_D: OFF_NA + NA_H * NA_D + (hh + 1) * NA_D]
        nav_ref[0, hh] = p[:, OFF_NA + 2 * NA_H * NA_D + hh * NA_D: OFF_NA + 2 * NA_H * NA_D + (hh + 1) * NA_D]
    qn = _rms(p[:, OFF_QA:OFF_KVA], qg_ref[...]).astype(BF16)
    qq = jnp.dot(qn, wq_ref[...], preferred_element_type=F32)
    cq = cq_ref[...]
    sq = sq_ref[...]
    for hh in range(MLA_H):
        qa = qq[:, hh * MLA_PAD:(hh + 1) * MLA_PAD]
        qb = qq[:, (MLA_H + hh) * MLA_PAD:(MLA_H + hh + 1) * MLA_PAD]
        mq_ref[0, hh] = (qa * cq + qb * sq).astype(BF16)
    ckv = _rms(p[:, OFF_KVA:OFF_KVA + MLA_KVL], kvg_ref[...])
    kr = p[:, OFF_KVA + MLA_KVL:D_IN]
    kr_partner = p[:, D_IN:D_IN_EXT]
    kr_rot = kr * ck_ref[...] + kr_partner * sk_ref[...]
    ckv_ref[...] = jnp.concatenate([ckv, kr], axis=-1)
    kin = jnp.concatenate([ckv, kr_rot], axis=-1).astype(BF16)
    kk = jnp.dot(kin, wkk_ref[...], preferred_element_type=F32)
    vv = jnp.dot(kin[:, :MLA_KVL], wv_ref[...], preferred_element_type=F32)
    for hh in range(MLA_H):
        mk_ref[0, hh] = kk[:, hh * MLA_PAD:(hh + 1) * MLA_PAD].astype(BF16)
        mv_ref[0, hh] = vv[:, hh * MLA_V:(hh + 1) * MLA_V].astype(BF16)


def _project(x, mod, g1, w_in_ext, qg, wq2, kvg, wkk, wv, cq, sq, ck, sk, layer):
    lay3 = lambda i: (layer, 0, 0)
    tile2 = lambda i: (i, 0)
    head4 = lambda i: (i, 0, 0, 0)
    rope2 = lambda i: (_rope_block(i), 0)
    return pl.pallas_call(
        _proj_kernel,
        grid=(N_TILES,),
        in_specs=[pl.BlockSpec((TM, D_MODEL), tile2),
                  pl.BlockSpec((None, 1, 6, D_MODEL), lambda i: (layer, _mod_row(i), 0, 0)),
                  pl.BlockSpec((None, 1, D_MODEL), lay3),
                  pl.BlockSpec((None, D_MODEL, D_IN_EXT), lay3),
                  pl.BlockSpec((None, 1, MLA_QL), lay3),
                  pl.BlockSpec((None, MLA_QL, 2 * MLA_H * MLA_PAD), lay3),
                  pl.BlockSpec((None, 1, MLA_KVL), lay3),
                  pl.BlockSpec((None, MLA_KVL + MLA_ROPE, MLA_H * MLA_PAD), lay3),
                  pl.BlockSpec((None, MLA_KVL, MLA_H * MLA_V), lay3),
                  pl.BlockSpec((TM, MLA_PAD), rope2),
                  pl.BlockSpec((TM, MLA_PAD), rope2),
                  pl.BlockSpec((TM, MLA_ROPE), rope2),
                  pl.BlockSpec((TM, MLA_ROPE), rope2)],
        out_specs=[pl.BlockSpec((TM, S5_W), tile2),
                   pl.BlockSpec((1, NA_H, TM, NA_D), head4),
                   pl.BlockSpec((1, NA_H, TM, NA_D), head4),
                   pl.BlockSpec((1, NA_H, TM, NA_D), head4),
                   pl.BlockSpec((1, MLA_H, TM, MLA_PAD), head4),
                   pl.BlockSpec((1, MLA_H, TM, MLA_PAD), head4),
                   pl.BlockSpec((1, MLA_H, TM, MLA_V), head4),
                   pl.BlockSpec((TM, MLA_KVL + MLA_ROPE), tile2)],
        out_shape=[jax.ShapeDtypeStruct((T_ALL, S5_W), F32),
                   jax.ShapeDtypeStruct((N_TILES, NA_H, TM, NA_D), BF16),
                   jax.ShapeDtypeStruct((N_TILES, NA_H, TM, NA_D), F32),
                   jax.ShapeDtypeStruct((N_TILES, NA_H, TM, NA_D), F32),
                   jax.ShapeDtypeStruct((N_TILES, MLA_H, TM, MLA_PAD), BF16),
                   jax.ShapeDtypeStruct((N_TILES, MLA_H, TM, MLA_PAD), BF16),
                   jax.ShapeDtypeStruct((N_TILES, MLA_H, TM, MLA_V), BF16),
                   jax.ShapeDtypeStruct((T_ALL, MLA_KVL + MLA_ROPE), F32)],
        compiler_params=_cparams(("arbitrary",)),
        name="proj",
    )(x, mod, g1, w_in_ext, qg, wq2, kvg, wkk, wv, cq, sq, ck, sk)


def _cache_kv_kernel(c_ref, wkk_ref, wv_ref, k_ref, v_ref):
    cin = c_ref[0, 0].astype(BF16)
    kk = jnp.dot(cin, wkk_ref[0], preferred_element_type=F32)
    vv = jnp.dot(cin[:, :MLA_KVL], wv_ref[0], preferred_element_type=F32)
    for hh in range(MLA_H):
        k_ref[0, 0, hh] = kk[:, hh * MLA_PAD:(hh + 1) * MLA_PAD].astype(BF16)
        v_ref[0, 0, hh] = vv[:, hh * MLA_V:(hh + 1) * MLA_V].astype(BF16)


def _cache_kv(cache_mla_kv, wkk_all, wv_all):
    return pl.pallas_call(
        _cache_kv_kernel,
        grid=(N_LAT, DEPTH),
        in_specs=[pl.BlockSpec((1, 1, PAST, MLA_KVL + MLA_ROPE), lambda n, l: (n, l, 0, 0)),
                  pl.BlockSpec((1, MLA_KVL + MLA_ROPE, MLA_H * MLA_PAD), lambda n, l: (l, 0, 0)),
                  pl.BlockSpec((1, MLA_KVL, MLA_H * MLA_V), lambda n, l: (l, 0, 0))],
        out_specs=[pl.BlockSpec((1, 1, MLA_H, PAST, MLA_PAD), lambda n, l: (n, l, 0, 0, 0)),
                   pl.BlockSpec((1, 1, MLA_H, PAST, MLA_V), lambda n, l: (n, l, 0, 0, 0))],
        out_shape=[jax.ShapeDtypeStruct((N_LAT, DEPTH, MLA_H, PAST, MLA_PAD), BF16),
                   jax.ShapeDtypeStruct((N_LAT, DEPTH, MLA_H, PAST, MLA_V), BF16)],
        compiler_params=_cparams(("arbitrary", "arbitrary")),
        name="cache_kv",
    )(cache_mla_kv, wkk_all, wv_all)


def _s5_disc_kernel(lr_ref, li_ref, ldt_ref, bre_ref, bim_ref, a_ref, bb_ref):
    lr = lr_ref[...]
    li = li_ref[...]
    dt = jnp.exp(ldt_ref[...])
    mag = jnp.exp(lr * dt)
    a_re = mag * jnp.cos(li * dt)
    a_im = mag * jnp.sin(li * dt)
    den = lr * lr + li * li
    nr = a_re - 1.0
    k_re = (nr * lr + a_im * li) / den
    k_im = (a_im * lr - nr * li) / den
    a_ref[0] = a_re
    a_ref[1] = a_im
    bre = bre_ref[...]
    bim = bim_ref[...]
    bb_ref[0] = k_re * bre - k_im * bim
    bb_ref[1] = k_re * bim + k_im * bre


def _s5_discretize(lam_re, lam_im, log_dt, b_re, b_im):
    g2 = 2 * S5_G
    a, bb = pl.pallas_call(
        _s5_disc_kernel,
        out_shape=[jax.ShapeDtypeStruct((2, g2, 1, S5_P), F32),
                   jax.ShapeDtypeStruct((2, g2, S5_C, S5_P), F32)],
        name="s5_disc",
    )(lam_re.reshape(g2, 1, S5_P), lam_im.reshape(g2, 1, S5_P), log_dt.reshape(g2, 1, 1),
      b_re.reshape(g2, S5_P, S5_C).transpose(0, 2, 1), b_im.reshape(g2, S5_P, S5_C).transpose(0, 2, 1))
    return a, bb


def _s5_matrices(a, bb, c_re, c_im):
    gp = S5_G * S5_P
    a_lanes = a.reshape(2, 2, S5_G, S5_P).transpose(1, 0, 2, 3).reshape(2, 2, gp)
    bbar = bb.reshape(2, 2, S5_G, S5_C, S5_P)
    eye = jnp.eye(S5_G, dtype=F32)
    wb = jnp.einsum('xdgcp,gh->dgcxhp', bbar, eye).reshape(2, S5_W, 2 * gp)
    cc = jnp.stack([c_re, -c_im], axis=1)
    wc = jnp.einsum('dxgcp,gh->dxhpgc', cc, eye).reshape(2, 2 * gp, S5_W)
    return a_lanes, wb.astype(BF16), wc.astype(BF16)


def _scan_kernel(u_ref, wb_ref, wc_ref, a_ref, h0_ref, y_ref, ht_ref, x_s, h_s, st_s):
    d = pl.program_id(0)
    c = pl.program_id(2)
    nc = pl.num_programs(2)
    gp = S5_G * S5_P

    @pl.when(c == 0)
    def _():
        st_s[...] = h0_ref[0]

    u = u_ref[...].reshape(SCAN_CH * SCAN_ROWS, S5_W).astype(BF16)
    x_s[...] = jnp.dot(u, wb_ref[0], preferred_element_type=F32)
    a_re = a_ref[0, 0:1, :]
    a_im = a_ref[0, 1:2, :]
    unroll = 4

    def outer(io, carry):
        h_re, h_im = carry
        for ii in range(unroll):
            i = io * unroll + ii
            t = jnp.where(d == 0, i, SCAN_CH - 1 - i)
            r = pl.multiple_of(t * SCAN_ROWS, SCAN_ROWS)
            x_re = x_s[pl.ds(r, SCAN_ROWS), pl.ds(0, gp)]
            x_im = x_s[pl.ds(r, SCAN_ROWS), pl.ds(gp, gp)]
            n_re = a_re * h_re - a_im * h_im + x_re
            n_im = a_re * h_im + a_im * h_re + x_im
            h_s[pl.ds(r, SCAN_ROWS), pl.ds(0, gp)] = n_re
            h_s[pl.ds(r, SCAN_ROWS), pl.ds(gp, gp)] = n_im
            h_re, h_im = n_re, n_im
        return h_re, h_im

    h_re, h_im = lax.fori_loop(0, SCAN_CH // unroll, outer, (st_s[:, pl.ds(0, gp)], st_s[:, pl.ds(gp, gp)]))
    st_s[:, pl.ds(0, gp)] = h_re
    st_s[:, pl.ds(gp, gp)] = h_im
    y = jnp.dot(h_s[...].astype(BF16), wc_ref[0], preferred_element_type=F32)
    y_ref[0] = y.reshape(SCAN_CH, SCAN_ROWS, S5_W)

    @pl.when(c == nc - 1)
    def _():
        ht_ref[0] = st_s[...]


def _s5_scan(u_tm, wb, wc, a_lanes, h0):
    L, n, _ = u_tm.shape
    nc = L // SCAN_CH
    ng = n // SCAN_ROWS

    def ceff(d, c):
        return c + d * (nc - 1 - 2 * c)

    return pl.pallas_call(
        _scan_kernel,
        grid=(2, ng, nc),
        in_specs=[pl.BlockSpec((SCAN_CH, SCAN_ROWS, S5_W), lambda d, g, c: (ceff(d, c), g, 0)),
                  pl.BlockSpec((1, S5_W, S5_STATE_W), lambda d, g, c: (d, 0, 0)),
                  pl.BlockSpec((1, S5_STATE_W, S5_W), lambda d, g, c: (d, 0, 0)),
                  pl.BlockSpec((1, 2, S5_G * S5_P), lambda d, g, c: (d, 0, 0)),
                  pl.BlockSpec((1, SCAN_ROWS, S5_STATE_W), lambda d, g, c: (d, g, 0))],
        out_specs=[pl.BlockSpec((1, SCAN_CH, SCAN_ROWS, S5_W), lambda d, g, c: (d, ceff(d, c), g, 0)),
                   pl.BlockSpec((1, SCAN_ROWS, S5_STATE_W), lambda d, g, c: (d, g, 0))],
        out_shape=[jax.ShapeDtypeStruct((2, L, n, S5_W), F32),
                   jax.ShapeDtypeStruct((2, n, S5_STATE_W), F32)],
        scratch_shapes=[pltpu.VMEM((SCAN_CH * SCAN_ROWS, S5_STATE_W), F32),
                        pltpu.VMEM((SCAN_CH * SCAN_ROWS, S5_STATE_W), F32),
                        pltpu.VMEM((SCAN_ROWS, S5_STATE_W), F32)],
        compiler_params=_cparams(("arbitrary", "arbitrary", "arbitrary")),
        name="s5_scan",
    )(u_tm, wb, wc, a_lanes, h0)


def _s5_pow_kernel(a_ref, p_ref):
    gp = S5_G * S5_P
    a_re = a_ref[0, 0:1, :]
    a_im = a_ref[0, 1:2, :]
    row = lax.broadcasted_iota(jnp.int32, (8, gp), 0)

    def build(rev):
        blk = (lambda m: slice(S5_SEG - m, S5_SEG)) if rev else (lambda m: slice(0, m))
        nxt = (lambda m: slice(S5_SEG - 2 * m, S5_SEG - m)) if rev else (lambda m: slice(m, 2 * m))
        top = (lambda m: slice(S5_SEG - m, S5_SEG - m + 1)) if rev else (lambda m: slice(m - 1, m))
        cur_re = jnp.broadcast_to(a_re, (8, gp))
        cur_im = jnp.broadcast_to(a_im, (8, gp))
        pr, pi = a_re, a_im
        for k in range(1, 8):
            pr, pi = pr * a_re - pi * a_im, pr * a_im + pi * a_re
            here = row == (7 - k if rev else k)
            cur_re = jnp.where(here, pr, cur_re)
            cur_im = jnp.where(here, pi, cur_im)
        if rev:
            cur_re = jnp.where(row == 7, a_re, cur_re)
            cur_im = jnp.where(row == 7, a_im, cur_im)
        p_ref[0, 0, blk(8), :] = cur_re
        p_ref[0, 1, blk(8), :] = cur_im
        m = 8
        while m < S5_SEG:
            am_re = p_ref[0, 0, top(m), :]
            am_im = p_ref[0, 1, top(m), :]
            lo_re = p_ref[0, 0, blk(m), :]
            lo_im = p_ref[0, 1, blk(m), :]
            p_ref[0, 0, nxt(m), :] = lo_re * am_re - lo_im * am_im
            p_ref[0, 1, nxt(m), :] = lo_re * am_im + lo_im * am_re
            m *= 2

    @pl.when(pl.program_id(0) == 0)
    def _():
        build(False)

    @pl.when(pl.program_id(0) == 1)
    def _():
        build(True)


def _s5_powers(a_lanes):
    gp = S5_G * S5_P
    return pl.pallas_call(
        _s5_pow_kernel,
        grid=(2,),
        in_specs=[pl.BlockSpec((1, 2, gp), lambda d: (d, 0, 0))],
        out_specs=pl.BlockSpec((1, 2, S5_SEG, gp), lambda d: (d, 0, 0, 0)),
        out_shape=jax.ShapeDtypeStruct((2, 2, S5_SEG, gp), F32),
        compiler_params=_cparams(("arbitrary",)),
        name="s5_powers",
    )(a_lanes)


def _s5_fix_kernel(y_in, p_ref, ht_ref, wc_ref, y_out, hin_s):
    gp = S5_G * S5_P
    d = pl.program_id(0)
    r = pl.program_id(1)

    @pl.when(r == 0)
    def _():
        hin_s[...] = jnp.zeros_like(hin_s)

        def chain(order, pw_row):
            aw_re = p_ref[0, 0, pw_row:pw_row + 1, :]
            aw_im = p_ref[0, 1, pw_row:pw_row + 1, :]
            for n in range(N_LAT):
                h_re = jnp.zeros((1, gp), F32)
                h_im = jnp.zeros((1, gp), F32)
                for s_prev, s_cur in zip(order[:-1], order[1:]):
                    e_re = ht_ref[0, n * N_SEG + s_prev:n * N_SEG + s_prev + 1, 0:gp]
                    e_im = ht_ref[0, n * N_SEG + s_prev:n * N_SEG + s_prev + 1, gp:2 * gp]
                    h_re, h_im = aw_re * h_re - aw_im * h_im + e_re, aw_re * h_im + aw_im * h_re + e_im
                    hin_s[n * N_SEG + s_cur, :, 0:gp] = h_re
                    hin_s[n * N_SEG + s_cur, :, gp:2 * gp] = h_im

        @pl.when(d == 0)
        def _():
            chain(list(range(N_SEG)), S5_SEG - 1)

        @pl.when(d == 1)
        def _():
            chain(list(range(N_SEG - 1, -1, -1)), 0)

    hin = hin_s[r]
    h_re = hin[:, 0:gp]
    h_im = hin[:, gp:2 * gp]
    p_re = p_ref[0, 0]
    p_im = p_ref[0, 1]
    z = jnp.concatenate([p_re * h_re - p_im * h_im, p_re * h_im + p_im * h_re], axis=-1).astype(BF16)
    y_out[0] = y_in[0] + jnp.dot(z, wc_ref[0], preferred_element_type=F32)


def _s5_fix(y_tok, ptab, ht_loc, wc):
    gp = S5_G * S5_P
    return pl.pallas_call(
        _s5_fix_kernel,
        grid=(2, SCAN_ROWS),
        in_specs=[pl.BlockSpec((1, S5_SEG, S5_W), lambda d, r: (d, r, 0)),
                  pl.BlockSpec((1, 2, S5_SEG, gp), lambda d, r: (d, 0, 0, 0)),
                  pl.BlockSpec((1, SCAN_ROWS, S5_STATE_W), lambda d, r: (d, 0, 0)),
                  pl.BlockSpec((1, S5_STATE_W, S5_W), lambda d, r: (d, 0, 0))],
        out_specs=pl.BlockSpec((1, S5_SEG, S5_W), lambda d, r: (d, r, 0)),
        out_shape=jax.ShapeDtypeStruct(y_tok.shape, F32),
        scratch_shapes=[pltpu.VMEM((SCAN_ROWS, 1, S5_STATE_W), F32)],
        input_output_aliases={0: 0},
        compiler_params=_cparams(("arbitrary", "arbitrary")),
        name="s5_fix",
    )(y_tok, ptab, ht_loc, wc)


def _qk(q, k):
    return lax.dot_general(q, k, (((1,), (1,)), ((), ())), preferred_element_type=F32)


def _softmax_pv(s_list, v_list):
    m = s_list[0].max(axis=-1, keepdims=True)
    for s in s_list[1:]:
        m = jnp.maximum(m, s.max(axis=-1, keepdims=True))
    den = None
    acc = None
    for s, v in zip(s_list, v_list):
        e = jnp.exp(s - m)
        ds = e.sum(axis=-1, keepdims=True)
        pv = jnp.dot(e.astype(BF16), v, preferred_element_type=F32)
        den = ds if den is None else den + ds
        acc = pv if acc is None else acc + pv
    return acc / den


def _ctx_attn_kernel(nq_ref, nk_ref, nv_ref, mq_ref, mk_ref, mv_ref, ckv_ref, fk_in, fv_in, fm_in,
                     na_ref, mla_ref, fk_ref, fv_ref, fm_ref):
    del fk_in, fv_in, fm_in
    fk_ref[0, 0] = nk_ref[0]
    fv_ref[0, 0] = nv_ref[0]
    fm_ref[0, 0] = ckv_ref[...]
    for hh in range(NA_H):
        s = _qk(nq_ref[0, hh], nk_ref[0, hh].astype(BF16)) * NA_SCALE
        na_ref[:, hh * NA_D:(hh + 1) * NA_D] = _softmax_pv([s], [nv_ref[0, hh].astype(BF16)])
    for hh in range(MLA_H):
        s = _qk(mq_ref[0, hh], mk_ref[0, hh]) * MLA_SCALE
        mla_ref[:, hh * MLA_V:(hh + 1) * MLA_V] = _softmax_pv([s], [mv_ref[0, hh]])


def _ctx_attention(naq, nak, nav, mq, mk, mv, ckv, fin_k, fin_v, fin_m, layer):
    head4 = lambda i: (i, 0, 0, 0)
    tile2 = lambda i: (i, 0)
    anyspec = pl.BlockSpec(memory_space=pl.ANY)
    return pl.pallas_call(
        _ctx_attn_kernel,
        grid=(CTX_TILES,),
        in_specs=[pl.BlockSpec((1, NA_H, TM, NA_D), head4)] * 3
                 + [pl.BlockSpec((1, MLA_H, TM, MLA_PAD), head4)] * 2
                 + [pl.BlockSpec((1, MLA_H, TM, MLA_V), head4),
                    pl.BlockSpec((TM, MLA_KVL + MLA_ROPE), tile2), anyspec, anyspec, anyspec],
        out_specs=[pl.BlockSpec((TM, NA_H * NA_D), tile2), pl.BlockSpec((TM, MLA_H * MLA_V), tile2),
                   pl.BlockSpec((1, 1, NA_H, TM, NA_D), lambda i: (i, layer, 0, 0, 0)),
                   pl.BlockSpec((1, 1, NA_H, TM, NA_D), lambda i: (i, layer, 0, 0, 0)),
                   pl.BlockSpec((1, 1, TM, MLA_KVL + MLA_ROPE), lambda i: (i, layer, 0, 0))],
        out_shape=[jax.ShapeDtypeStruct((T_CTX, NA_H * NA_D), F32),
                   jax.ShapeDtypeStruct((T_CTX, MLA_H * MLA_V), F32),
                   jax.ShapeDtypeStruct(fin_k.shape, F32),
                   jax.ShapeDtypeStruct(fin_v.shape, F32),
                   jax.ShapeDtypeStruct(fin_m.shape, F32)],
        input_output_aliases={7: 2, 8: 3, 9: 4},
        compiler_params=_cparams(("arbitrary",)),
        name="ctx_attn",
    )(naq, nak, nav, mq, mk, mv, ckv, fin_k, fin_v, fin_m)


def _na_lat_kernel(q_ref, k0_ref, k1_ref, k2_ref, v0_ref, v1_ref, v2_ref, kc_ref, vc_ref, b_ref, o_ref):
    for hh in range(NA_H):
        q = q_ref[0, hh]
        kb = jnp.concatenate([k0_ref[0, hh], k1_ref[0, hh], k2_ref[0, hh]], axis=0).astype(BF16)
        vb = jnp.concatenate([v0_ref[0, hh], v1_ref[0, hh], v2_ref[0, hh]], axis=0).astype(BF16)
        s_win = _qk(q, kb) * NA_SCALE + b_ref[0, 0, hh]
        s_ctx = _qk(q, kc_ref[0, 0, hh].astype(BF16)) * NA_SCALE
        o_ref[:, hh * NA_D:(hh + 1) * NA_D] = _softmax_pv([s_win, s_ctx], [vb, vc_ref[0, 0, hh].astype(BF16)])


def _band_start(j):
    return jnp.clip(j - 1, 0, LAT_TILES - 3)


def _na_latent(naq, nak, nav, cache_k, cache_v, bias, layer):
    def qmap(n, j):
        return (CTX_TILES + n * LAT_TILES + j, 0, 0, 0)

    def kmap(off):
        return lambda n, j: (CTX_TILES + n * LAT_TILES + _band_start(j) + off, 0, 0, 0)

    def bmap(n, j):
        return (layer, jnp.where(j == 0, 0, jnp.where(j == LAT_TILES - 1, 2, 1)), 0, 0, 0)

    blk = (1, NA_H, TM, NA_D)
    return pl.pallas_call(
        _na_lat_kernel,
        grid=(N_LAT, LAT_TILES),
        in_specs=[pl.BlockSpec(blk, qmap)]
                 + [pl.BlockSpec(blk, kmap(o)) for o in range(3)]
                 + [pl.BlockSpec(blk, kmap(o)) for o in range(3)]
                 + [pl.BlockSpec((1, 1, NA_H, PAST, NA_D), lambda n, j: (n, layer, 0, 0, 0))] * 2
                 + [pl.BlockSpec((1, 1, NA_H, TM, 3 * TM), bmap)],
        out_specs=pl.BlockSpec((TM, NA_H * NA_D), lambda n, j: (n * LAT_TILES + j, 0)),
        out_shape=jax.ShapeDtypeStruct((T_LAT, NA_H * NA_D), F32),
        compiler_params=_cparams(("arbitrary", "arbitrary")),
        name="na_latent",
    )(naq, nak, nak, nak, nav, nav, nav, cache_k, cache_v, bias)


def _na_bias_tables(rpb_all):
    rows = L_LAT // GRID_W
    nr, ncol = 2 * WIN_H - 1, 2 * WIN_W - 1
    w = np.arange(GRID_W)
    cs = np.clip(w - WIN_W // 2, 0, GRID_W - WIN_W)
    cc = np.arange(GRID_W)
    col_ok = (cc[None, :] >= cs[:, None]) & (cc[None, :] < cs[:, None] + WIN_W)
    dc = cc[None, :] - w[:, None] + WIN_W - 1
    col_sel = (dc[:, :, None] == np.arange(ncol)) & col_ok[:, :, None]
    row_sel, oks = [], []
    for j, s in ((0, 0), (1, 0), (LAT_TILES - 1, LAT_TILES - 3)):
        r = 4 * j + np.arange(4)
        rs = np.clip(r - WIN_H // 2, 0, rows - WIN_H)
        krow = 4 * s + np.arange(12)
        row_ok = (krow[None, :] >= rs[:, None]) & (krow[None, :] < rs[:, None] + WIN_H)
        dr = krow[None, :] - r[:, None] + WIN_H - 1
        row_sel.append((dr[:, :, None] == np.arange(nr)) & row_ok[:, :, None])
        oks.append(row_ok[:, None, :, None] & col_ok[None, :, None, :])
    row_sel = jnp.asarray(np.stack(row_sel), F32)
    col_sel = jnp.asarray(col_sel, F32)
    mask = jnp.asarray(np.where(np.stack(oks), 0.0, NEG).reshape(3, TM, 3 * TM), F32)
    b = jnp.einsum('prka,lhab,wcb->lphrwkc', row_sel, rpb_all, col_sel, precision=lax.Precision.HIGHEST)
    return b.reshape(DEPTH, 3, NA_H, TM, 3 * TM) + mask[None, :, None]


def _mla_lat_kernel(q_ref, k_ref, v_ref, kc_ref, vc_ref, o_ref):
    for hh in range(MLA_H):
        q = q_ref[0, hh]
        k = k_ref[:, hh].reshape(L_LAT, MLA_PAD)
        v = v_ref[:, hh].reshape(L_LAT, MLA_V)
        s_lat = _qk(q, k) * MLA_SCALE
        s_ctx = _qk(q, kc_ref[0, 0, hh]) * MLA_SCALE
        o_ref[:, hh * MLA_V:(hh + 1) * MLA_V] = _softmax_pv([s_lat, s_ctx], [v, vc_ref[0, 0, hh]])


def _mla_latent(mq, mk, mv, kc, vc, layer):
    seq_blk = CTX_TILES // LAT_TILES
    return pl.pallas_call(
        _mla_lat_kernel,
        grid=(N_LAT, LAT_TILES),
        in_specs=[pl.BlockSpec((1, MLA_H, TM, MLA_PAD), lambda n, j: (CTX_TILES + n * LAT_TILES + j, 0, 0, 0)),
                  pl.BlockSpec((LAT_TILES, MLA_H, TM, MLA_PAD), lambda n, j: (seq_blk + n, 0, 0, 0)),
                  pl.BlockSpec((LAT_TILES, MLA_H, TM, MLA_V), lambda n, j: (seq_blk + n, 0, 0, 0)),
                  pl.BlockSpec((1, 1, MLA_H, PAST, MLA_PAD), lambda n, j: (n, layer, 0, 0, 0)),
                  pl.BlockSpec((1, 1, MLA_H, PAST, MLA_V), lambda n, j: (n, layer, 0, 0, 0))],
        out_specs=pl.BlockSpec((TM, MLA_H * MLA_V), lambda n, j: (n * LAT_TILES + j, 0)),
        out_shape=jax.ShapeDtypeStruct((T_LAT, MLA_H * MLA_V), F32),
        compiler_params=_cparams(("arbitrary", "arbitrary")),
        name="mla_latent",
    )(mq, mk, mv, kc, vc)


def _post_kernel(y_ref, u_ref, d_ref, gw_ref, gb_ref, nac_ref, nal_ref, mlac_ref, mlal_ref, wo_ref, x_ref, mod_ref,
                 g2_ref, rw_ref, rb_ref, x1_ref, h2_ref, te_ref, tg_ref, rk_ref, cnt_ref):
    ys = y_ref[0] + y_ref[1] + u_ref[...] * d_ref[...]
    g = jax.nn.gelu(ys)
    s5o = g * jax.nn.sigmoid(jnp.dot(g.astype(BF16), gw_ref[...], preferred_element_type=F32) + gb_ref[...])
    is_ctx = pl.program_id(0) < CTX_TILES
    na = jnp.where(is_ctx, nac_ref[...], nal_ref[...])
    mla = jnp.where(is_ctx, mlac_ref[...], mlal_ref[...])
    out = (jnp.dot(s5o.astype(BF16), wo_ref[0:S5_W, :], preferred_element_type=F32)
           + jnp.dot(na.astype(BF16), wo_ref[S5_W:S5_W + NA_H * NA_D, :], preferred_element_type=F32)
           + jnp.dot(mla.astype(BF16), wo_ref[S5_W + NA_H * NA_D:, :], preferred_element_type=F32))
    mod = mod_ref[0]
    x1 = x_ref[...] + mod[2:3, :] * out
    x1_ref[...] = x1
    h2 = _rms(x1, g2_ref[...]) * (1.0 + mod[4:5, :]) + mod[3:4, :]
    h2_ref[...] = h2
    h_hi = h2.astype(BF16)
    h_lo = (h2 - h_hi.astype(F32)).astype(BF16)
    logits = (jnp.dot(h_hi, rw_ref[0], preferred_element_type=F32)
              + jnp.dot(h_hi, rw_ref[1], preferred_element_type=F32)
              + jnp.dot(h_lo, rw_ref[0], preferred_element_type=F32)) + rb_ref[...]
    lane_i = lax.broadcasted_iota(jnp.int32, (TM, 128), 1)
    lane = lane_i.astype(F32)
    cur = jnp.where(lane_i < N_EXP, logits, -jnp.inf)
    te = jnp.zeros((TM, 128), F32)
    tv = jnp.zeros((TM, 128), F32)
    hot = jnp.zeros((TM, 128), F32)
    idxs = []
    top = None
    for k in range(TOP_K):
        m = cur.max(axis=-1, keepdims=True)
        idx = jnp.where(cur == m, lane, 128.0).min(axis=-1, keepdims=True)
        sel = lane == idx
        top = m if top is None else top
        idxs.append(idx)
        te = jnp.where(lane_i == k, idx, te)
        tv = jnp.where(lane_i == k, jnp.exp(m - top), tv)
        hot = jnp.where(sel, 1.0, hot)
        cur = jnp.where(sel, -jnp.inf, cur)
    te_ref[...] = te.astype(jnp.int32)
    tg_ref[...] = tv / tv.sum(axis=-1, keepdims=True)
    row = lax.broadcasted_iota(jnp.int32, (TM, TM), 0)
    col = lax.broadcasted_iota(jnp.int32, (TM, TM), 1)
    tri = jnp.where(col < row, 1.0, 0.0).astype(BF16)
    before = jnp.dot(tri, hot.astype(BF16), preferred_element_type=F32)
    rk = jnp.zeros((TM, 128), F32)
    for k in range(TOP_K):
        rk = jnp.where(lane_i == k, jnp.where(lane == idxs[k], before, 0.0).sum(axis=-1, keepdims=True), rk)
    rk_ref[...] = rk.astype(jnp.int32)
    cnt_ref[0] = jnp.broadcast_to(hot.sum(axis=0, keepdims=True), (8, 128)).astype(jnp.int32)


def _post(y2, u, s5_d, glu_w, glu_b, na_ctx, na_lat, mla_ctx, mla_lat, w_out, x, mod, g2, rw, rb, layer):
    lay3 = lambda i: (layer, 0, 0)
    tile2 = lambda i: (i, 0)
    ctx2 = lambda i: (jnp.minimum(i, CTX_TILES - 1), 0)
    lat2 = lambda i: (jnp.maximum(i - CTX_TILES, 0), 0)
    return pl.pallas_call(
        _post_kernel,
        grid=(N_TILES,),
        in_specs=[pl.BlockSpec((2, TM, S5_W), lambda i: (0, i, 0)),
                  pl.BlockSpec((TM, S5_W), tile2),
                  pl.BlockSpec((None, 1, S5_W), lay3),
                  pl.BlockSpec((None, S5_W, S5_W), lay3),
                  pl.BlockSpec((None, 1, S5_W), lay3),
                  pl.BlockSpec((TM, NA_H * NA_D), ctx2),
                  pl.BlockSpec((TM, NA_H * NA_D), lat2),
                  pl.BlockSpec((TM, MLA_H * MLA_V), ctx2),
                  pl.BlockSpec((TM, MLA_H * MLA_V), lat2),
                  pl.BlockSpec((None, D_MODEL, D_MODEL), lay3),
                  pl.BlockSpec((TM, D_MODEL), tile2),
                  pl.BlockSpec((None, 1, 6, D_MODEL), lambda i: (layer, _mod_row(i), 0, 0)),
                  pl.BlockSpec((None, 1, D_MODEL), lay3),
                  pl.BlockSpec((None, 2, D_MODEL, 128), lambda i: (layer, 0, 0, 0)),
                  pl.BlockSpec((None, 1, 128), lay3)],
        out_specs=[pl.BlockSpec((TM, D_MODEL), tile2),
                   pl.BlockSpec((TM, D_MODEL), tile2),
                   pl.BlockSpec((TM, 128), tile2),
                   pl.BlockSpec((TM, 128), tile2),
                   pl.BlockSpec((TM, 128), tile2),
                   pl.BlockSpec((1, 8, 128), lambda i: (i, 0, 0))],
        out_shape=[jax.ShapeDtypeStruct((T_ALL, D_MODEL), F32),
                   jax.ShapeDtypeStruct((T_ALL, D_MODEL), F32),
                   jax.ShapeDtypeStruct((T_ALL, 128), jnp.int32),
                   jax.ShapeDtypeStruct((T_ALL, 128), F32),
                   jax.ShapeDtypeStruct((T_ALL, 128), jnp.int32),
                   jax.ShapeDtypeStruct((N_TILES, 8, 128), jnp.int32)],
        compiler_params=_cparams(("arbitrary",)),
        name="post",
    )(y2, u, s5_d, glu_w, glu_b, na_ctx, na_lat, mla_ctx, mla_lat, w_out, x, mod, g2, rw, rb)


def _moe_fused_kernel(be_ref, first_ref, nused_ref, pos_ref,
                      dflt_hbm, h_hbm, wgu_ref, bgu_ref, wd_ref, bd_ref, ys_hbm,
                      asg_s, xs0, xs1, yo0, yo1, wgu_s, wd_s, act_s, gsem, ssem, tsem):
    b = pl.program_id(0)
    nused = nused_ref[0]
    xs = (xs0, xs1)
    yo = (yo0, yo1)

    def wait_gather(p):
        pltpu.make_async_copy(h_hbm.at[pl.ds(0, MOE_BM), :], xs[p], gsem.at[p]).wait()

    def wait_scatter(p):
        pltpu.make_async_copy(yo[p], ys_hbm.at[pl.ds(0, MOE_BM), :], ssem.at[p]).wait()

    def issue_gather(blk, p):
        r0 = (blk + 1) * MOE_BM
        for j in range(MOE_BM):
            tok = asg_s[r0 + j] >> 16
            pltpu.make_async_copy(h_hbm.at[pl.ds(tok, 1), :], xs[p].at[pl.ds(j, 1), :],
                                  gsem.at[p]).start(priority=j % 2)

    def issue_scatter(blk, p, lo=0, hi=MOE_BM):
        r0 = (blk + 1) * MOE_BM
        for j in range(lo, hi):
            d = asg_s[r0 + j] & 0xFFFF
            pltpu.make_async_copy(yo[p].at[pl.ds(j, 1), :], ys_hbm.at[pl.ds(d, 1), :],
                                  ssem.at[p]).start(priority=j % 2)

    @pl.when(b == 0)
    def _():
        cp = pltpu.make_async_copy(dflt_hbm, asg_s, tsem)
        cp.start()
        cp.wait()

        def inv(i, carry):
            a0 = i * 8
            v0 = a0 | ((((a0 >> 10) << 8) + (a0 & (TM - 1))) << 16)
            for uu in range(8):
                asg_s[pos_ref[a0 + uu]] = v0 + uu * ((1 << 16) + 1)
            return carry
        lax.fori_loop(0, MOE_SLOTS // 8, inv, 0)
        yo1[...] = jnp.zeros_like(yo1)
        zc = pltpu.make_async_copy(yo1, ys_hbm.at[pl.ds(MOE_SLOTS, MOE_BM), :], tsem)
        zc.start()
        zc.wait()
        issue_gather(0, 0)

    def step(p):
        q = 1 - p
        wait_gather(p)
        if p == 0:
            @pl.when(b >= 1)
            def _():
                wait_scatter(p)
        else:
            wait_scatter(p)

        @pl.when(first_ref[b] == 1)
        def _():
            wgu_s[...] = wgu_ref[0, 0].astype(BF16)
            wd_s[...] = wd_ref[0, 0].astype(BF16)

        issue_gather(jnp.minimum(b + 1, nused - 1), q)
        issue_scatter(b - 1, q, 0, MOE_BM // 2)
        gu = jnp.dot(xs[p][...].astype(BF16), wgu_s[...], preferred_element_type=F32) + bgu_ref[0, 0]
        gate = jnp.minimum(gu[:, :D_FF], SWIGLU_LIMIT)
        up = jnp.clip(gu[:, D_FF:], -SWIGLU_LIMIT, SWIGLU_LIMIT)
        act_s[...] = ((up + 1.0) * (gate * jax.nn.sigmoid(SWIGLU_ALPHA * gate))).astype(BF16)

        @pl.when(b >= 0)
        def _():
            issue_scatter(b - 1, q, MOE_BM // 2, MOE_BM)
            yo[p][...] = jnp.dot(act_s[...], wd_s[...], preferred_element_type=F32) + bd_ref[0, 0]

    def drain(p):
        q = 1 - p
        wait_gather(p)
        wait_scatter(p)
        issue_scatter(nused - 1, q)
        wait_scatter(q)

    for p in range(2):
        @pl.when((b < nused) & (b % 2 == p))
        def _():
            step(p)

        @pl.when((b == nused) & (b % 2 == p))
        def _():
            drain(p)


def _moe_fused(block_e, first, nused, pos_km, h2, w_gu, b_gu, w_down, b_down, layer):
    def wmap(b, be, fi, nu, pos):
        return (layer, be[b], 0, 0)

    r = jnp.arange(ASG_ROWS, dtype=jnp.int32)
    dflt = MOE_SLOTS + (((r >> 8) + 1) & 1) * MOE_BM + (r & (MOE_BM - 1))

    return pl.pallas_call(
        _moe_fused_kernel,
        grid_spec=pltpu.PrefetchScalarGridSpec(
            num_scalar_prefetch=4, grid=(MOE_BLOCKS + 1,),
            in_specs=[pl.BlockSpec(memory_space=pl.ANY), pl.BlockSpec(memory_space=pl.ANY),
                      pl.BlockSpec((1, 1, D_MODEL, 2 * D_FF), wmap),
                      pl.BlockSpec((1, 1, 1, 2 * D_FF), wmap),
                      pl.BlockSpec((1, 1, D_FF, D_MODEL), wmap),
                      pl.BlockSpec((1, 1, 1, D_MODEL), wmap)],
            out_specs=pl.BlockSpec(memory_space=pl.ANY),
            scratch_shapes=[pltpu.SMEM((ASG_ROWS,), jnp.int32),
                            pltpu.VMEM((MOE_BM, D_MODEL), F32), pltpu.VMEM((MOE_BM, D_MODEL), F32),
                            pltpu.VMEM((MOE_BM, D_MODEL), F32), pltpu.VMEM((MOE_BM, D_MODEL), F32),
                            pltpu.VMEM((D_MODEL, 2 * D_FF), BF16), pltpu.VMEM((D_FF, D_MODEL), BF16),
                            pltpu.VMEM((MOE_BM, D_FF), BF16),
                            pltpu.SemaphoreType.DMA((2,)), pltpu.SemaphoreType.DMA((2,)),
                            pltpu.SemaphoreType.DMA(())]),
        out_shape=jax.ShapeDtypeStruct((MOE_SLOTS + 2 * MOE_BM, D_MODEL), F32),
        compiler_params=_cparams(("arbitrary",)),
        name="moe_fused",
    )(block_e, first, nused, pos_km, dflt, h2, w_gu, b_gu.reshape(DEPTH, N_EXP, 1, 2 * D_FF), w_down,
      b_down.reshape(DEPTH, N_EXP, 1, D_MODEL))


def _combine_kernel(ys_ref, tg_ref, x1_ref, mod_ref, *rest):
    tg = tg_ref[...]
    y = ys_ref[0:TM, :] * tg[:, 0:1]
    for k in range(1, TOP_K):
        y = y + ys_ref[k * TM:(k + 1) * TM, :] * tg[:, k:k + 1]
    x2 = x1_ref[...] + mod_ref[0][5:6, :] * y
    if len(rest) == 2:
        g_ref, o_ref = rest
        o_ref[...] = _rms(x2, g_ref[...])
    else:
        rest[0][...] = x2


def _combine(ys, tg, x1, mod, layer, tile0=0, ntiles=N_TILES, final_g=None):
    off = lambda i: (i + tile0, 0)
    in_specs = [pl.BlockSpec((TOP_K * TM, D_MODEL), off),
                pl.BlockSpec((TM, 128), off),
                pl.BlockSpec((TM, D_MODEL), off),
                pl.BlockSpec((None, 1, 6, D_MODEL), lambda i: (layer, _mod_row(i + tile0), 0, 0))]
    args = [ys, tg, x1, mod]
    if final_g is not None:
        in_specs.append(pl.BlockSpec((1, D_MODEL), lambda i: (0, 0)))
        args.append(final_g)
    return pl.pallas_call(
        _combine_kernel,
        grid=(ntiles,),
        in_specs=in_specs,
        out_specs=pl.BlockSpec((TM, D_MODEL), lambda i: (i, 0)),
        out_shape=jax.ShapeDtypeStruct((ntiles * TM, D_MODEL), F32),
        compiler_params=_cparams(("arbitrary",)),
        name="moe_combine",
    )(*args)


def _moe_plan(te, rk, cnt):
    tile_cnt = cnt[:, 0, :N_EXP]
    counts = tile_cnt.sum(axis=0)
    padded = (counts + MOE_BM - 1) // MOE_BM * MOE_BM
    pad_end = jnp.cumsum(padded)
    pad_start = pad_end - padded
    tile_off = jnp.cumsum(tile_cnt, axis=0) - tile_cnt
    base = (pad_start[None, :] + tile_off).astype(jnp.int32)
    e = te[:, :TOP_K].reshape(N_TILES, TM, TOP_K)
    hot = e[..., None] == jnp.arange(N_EXP, dtype=jnp.int32)
    pos = jnp.sum(jnp.where(hot, base[:, None, None, :], 0), axis=-1) + rk[:, :TOP_K].reshape(N_TILES, TM, TOP_K)
    pos_km = (pos.transpose(0, 2, 1).reshape(-1) + MOE_BM).astype(jnp.int32)
    nused = (pad_end[-1] // MOE_BM).astype(jnp.int32)
    blk = jnp.arange(MOE_BLOCKS + 1, dtype=jnp.int32)
    be = jnp.minimum(jnp.sum((pad_end[None, :] <= (blk * MOE_BM)[:, None]).astype(jnp.int32), axis=1), N_EXP - 1)
    be = jnp.where(blk < nused, be, be[nused - 1])
    first = jnp.concatenate([jnp.ones((1,), jnp.int32), (be[1:] != be[:-1]).astype(jnp.int32)])
    return pos_km, be, first, nused.reshape(1)


_ROPE_PERM = np.concatenate([np.arange(8, 16), np.arange(0, 8), np.arange(24, 32), np.arange(16, 24)])


def _rope_tables():
    half = MLA_ROPE // 2
    t = jnp.arange(L_LAT)
    row = (t // GRID_W).astype(F32)
    col = (t % GRID_W).astype(F32)
    inv = ROPE_BASE ** (-jnp.arange(0, half, 2, dtype=F32) / half)

    def part(pos):
        ang = pos[:, None] * inv[None, :]
        c, s = jnp.cos(ang), jnp.sin(ang)
        return jnp.concatenate([c, c], axis=-1), jnp.concatenate([-s, s], axis=-1)

    cr, sr = part(row)
    cc, sc = part(col)
    cos32 = jnp.concatenate([cr, cc], axis=-1)
    sin32 = jnp.concatenate([sr, sc], axis=-1)
    ck = jnp.concatenate([jnp.ones((TM, MLA_ROPE), F32), cos32], axis=0)
    sk = jnp.concatenate([jnp.zeros((TM, MLA_ROPE), F32), sin32], axis=0)
    padw = MLA_PAD - MLA_NOPE - MLA_ROPE
    cq = jnp.concatenate([jnp.ones((TM + L_LAT, MLA_NOPE), F32), ck, jnp.ones((TM + L_LAT, padw), F32)], axis=-1)
    sq = jnp.concatenate([jnp.zeros((TM + L_LAT, MLA_NOPE), F32), sk, jnp.zeros((TM + L_LAT, padw), F32)], axis=-1)
    return cq, sq, ck, sk


def _mla_weights(w_qb, w_kvb):
    dq = MLA_NOPE + MLA_ROPE
    wq = w_qb.reshape(MLA_QL, MLA_H, dq)
    zpad = jnp.zeros((MLA_QL, MLA_H, MLA_PAD - dq), F32)
    q_main = jnp.concatenate([wq, zpad], axis=-1)
    q_part = jnp.concatenate([jnp.zeros((MLA_QL, MLA_H, MLA_NOPE), F32), wq[:, :, MLA_NOPE + _ROPE_PERM], zpad], axis=-1)
    wq2 = jnp.concatenate([q_main.reshape(MLA_QL, -1), q_part.reshape(MLA_QL, -1)], axis=-1).astype(BF16)
    wkv = w_kvb.reshape(MLA_KVL, MLA_H, MLA_NOPE + MLA_V)
    k_top = jnp.concatenate([wkv[:, :, :MLA_NOPE], jnp.zeros((MLA_KVL, MLA_H, MLA_PAD - MLA_NOPE), F32)], axis=-1)
    place = jnp.concatenate([jnp.zeros((MLA_ROPE, MLA_NOPE), F32), jnp.eye(MLA_ROPE, dtype=F32),
                             jnp.zeros((MLA_ROPE, MLA_PAD - dq), F32)], axis=-1)
    k_bot = jnp.broadcast_to(place[:, None, :], (MLA_ROPE, MLA_H, MLA_PAD))
    wkk = jnp.concatenate([k_top, k_bot], axis=0).reshape(MLA_KVL + MLA_ROPE, MLA_H * MLA_PAD).astype(BF16)
    wv = wkv[:, :, MLA_NOPE:].reshape(MLA_KVL, MLA_H * MLA_V).astype(BF16)
    return wq2, wkk, wv


def kernel(x_prompt, x_sample, cache_na_k, cache_na_v, cache_mla_kv, state_s5_re, state_s5_im, c, c_ctx, w_mod, b_mod, norm1_g, norm2_g, w_in, w_out, s5_lambda_re, s5_lambda_im, s5_log_dt, s5_b_re, s5_b_im, s5_c_re, s5_c_im, s5_d, s5_glu_w, s5_glu_b, na_rpb, mla_q_norm_g, mla_w_qb, mla_kv_norm_g, mla_w_kvb, router_w, router_b, moe_w_gu, moe_b_gu, moe_w_down, moe_b_down, final_norm_g):
    x = jnp.concatenate([x_prompt.reshape(T_CTX, D_MODEL), x_sample.reshape(T_LAT, D_MODEL)], axis=0)
    cv = jnp.concatenate([c_ctx[None, :], c, jnp.zeros((MOD_ROWS - 1 - N_LAT, D_MODEL), F32)], axis=0)
    mod = _modulation(cv, w_mod, b_mod).reshape(DEPTH, MOD_ROWS, 6, D_MODEL)
    cq, sq, ck, sk = _rope_tables()
    wq2, wkk, wv = jax.vmap(_mla_weights)(mla_w_qb, mla_w_kvb)
    kc_all, vc_all = _cache_kv(cache_mla_kv, wkk, wv)
    w_in_ext = jnp.concatenate([w_in, w_in[:, :, OFF_KVA + MLA_KVL + _ROPE_PERM]], axis=-1).astype(BF16)
    w_out_bf = w_out.astype(BF16)
    glu_w_bf = s5_glu_w.astype(BF16)
    rw = jnp.pad(router_w, ((0, 0), (0, 0), (0, 128 - N_EXP)))
    rw_hi = rw.astype(BF16)
    rw = jnp.stack([rw_hi, (rw - rw_hi.astype(F32)).astype(BF16)], axis=1)
    rb = jnp.pad(router_b, ((0, 0), (0, 128 - N_EXP)))[:, None, :]
    vec = lambda v: v[:, None, :]
    gp = S5_G * S5_P
    na_bias = _na_bias_tables(na_rpb)
    fin_k = jnp.zeros((N_CTX, DEPTH, NA_H, L_CTX, NA_D), F32)
    fin_v = jnp.zeros((N_CTX, DEPTH, NA_H, L_CTX, NA_D), F32)
    fin_m = jnp.zeros((N_CTX, DEPTH, L_CTX, MLA_KVL + MLA_ROPE), F32)

    s5_re_list, s5_im_list = [], []
    for l in range(DEPTH):
        u, naq, nak, nav, mq, mk, mv, ckv = _project(
            x, mod, vec(norm1_g), w_in_ext, vec(mla_q_norm_g), wq2, vec(mla_kv_norm_g), wkk, wv,
            cq, sq, ck, sk, l)

        a, bb = _s5_discretize(s5_lambda_re[l], s5_lambda_im[l], s5_log_dt[l], s5_b_re[l], s5_b_im[l])
        a_lanes, wb, wc = _s5_matrices(a, bb, s5_c_re[l], s5_c_im[l])
        u_ctx = u[:T_CTX].reshape(N_CTX, L_CTX, S5_W).transpose(1, 0, 2)
        y_ctx, ht_ctx = _s5_scan(u_ctx, wb, wc, a_lanes, jnp.zeros((2, N_CTX, S5_STATE_W), F32))
        u_lat = u[T_CTX:].reshape(SCAN_ROWS, S5_SEG, S5_W).transpose(1, 0, 2)
        h0 = jnp.concatenate([state_s5_re[:, l].reshape(N_LAT, 2, gp), state_s5_im[:, l].reshape(N_LAT, 2, gp)], axis=-1)
        h0z = jnp.zeros((N_LAT, N_SEG - 1, S5_STATE_W), F32)
        h0 = jnp.stack([jnp.concatenate([h0[:, 0:1], h0z], axis=1).reshape(SCAN_ROWS, S5_STATE_W),
                        jnp.concatenate([h0z, h0[:, 1:2]], axis=1).reshape(SCAN_ROWS, S5_STATE_W)], axis=0)
        y_loc, ht_loc = _s5_scan(u_lat, wb, wc, a_lanes, h0)
        ptab = _s5_powers(a_lanes)
        y_lat = _s5_fix(y_loc.transpose(0, 2, 1, 3).reshape(2, T_LAT, S5_W), ptab, ht_loc, wc)
        y2 = jnp.concatenate([y_ctx.transpose(0, 2, 1, 3).reshape(2, T_CTX, S5_W), y_lat], axis=1)
        st = ht_ctx.reshape(2, N_CTX, 2, S5_G, S5_P).transpose(1, 0, 2, 3, 4)
        s5_re_list.append(st[:, :, 0])
        s5_im_list.append(st[:, :, 1])

        na_ctx, mla_ctx, fin_k, fin_v, fin_m = _ctx_attention(naq, nak, nav, mq, mk, mv, ckv, fin_k, fin_v, fin_m, l)
        na_lat = _na_latent(naq, nak, nav, cache_na_k, cache_na_v, na_bias, l)
        mla_lat = _mla_latent(mq, mk, mv, kc_all, vc_all, l)

        x1, h2, te, tg, rk, cnt = _post(y2, u, vec(s5_d), glu_w_bf, vec(s5_glu_b), na_ctx, na_lat, mla_ctx, mla_lat,
                                        w_out_bf, x, mod, vec(norm2_g), rw, rb, l)
        pos_km, be, first, nused = _moe_plan(te, rk, cnt)
        ys = _moe_fused(be, first, nused, pos_km, h2, moe_w_gu, moe_b_gu, moe_w_down, moe_b_down, l)
        if l < DEPTH - 1:
            x = _combine(ys, tg, x1, mod, l)

    g_fin = final_norm_g[None]
    y_prompt = _combine(ys, tg, x1, mod, DEPTH - 1, 0, CTX_TILES, g_fin).reshape(N_CTX, L_CTX, D_MODEL)
    y_sample = _combine(ys, tg, x1, mod, DEPTH - 1, CTX_TILES, N_TILES - CTX_TILES, g_fin).reshape(N_LAT, L_LAT, D_MODEL)
    return (y_prompt, y_sample, fin_k, fin_v, fin_m, jnp.stack(s5_re_list, axis=1), jnp.stack(s5_im_list, axis=1))
```
